```python
import math
import jax
import jax.numpy as jnp
from jax import lax
import numpy as np

D_MODEL = 1024
BATCH = 2
SEQ = 16384
DEPTH = 2

F32 = jnp.float32
CTX_LEN = 256
GRID_W = 64
HEAD_DIM = 64
GROUP_WIDTH = D_MODEL // 4
BLOCK = 128
NORM_EPS = 1e-6
ROPE_BASE = 10000.0
SWA_HEADS = GROUP_WIDTH // HEAD_DIM
SWA_KV_HEADS = SWA_HEADS // 2
WINDOW = 128
RET_HEADS = GROUP_WIDTH // HEAD_DIM
RET_CHUNK = 128
DIFF_HEADS = GROUP_WIDTH // HEAD_DIM
DIFF_DIM = HEAD_DIM // 2
LRU_WIDTH = GROUP_WIDTH
LRU_BLOCKS = 4
LRU_BLOCK = LRU_WIDTH // LRU_BLOCKS
CONV_WIDTH = 4
LRU_C = 8.0
D_FF = 2816
N_EXPERTS = 8
TOP_K = 2
D_FF_EXPERT = 3584

PROJ_WIDTHS = (
    SWA_HEADS * HEAD_DIM, SWA_KV_HEADS * HEAD_DIM, SWA_KV_HEADS * HEAD_DIM,
    RET_HEADS * HEAD_DIM, RET_HEADS * HEAD_DIM, RET_HEADS * HEAD_DIM, RET_HEADS * HEAD_DIM,
    DIFF_HEADS * 2 * DIFF_DIM, DIFF_HEADS * 2 * DIFF_DIM, DIFF_HEADS * HEAD_DIM,
    LRU_WIDTH, LRU_WIDTH)
D_IN = sum(PROJ_WIDTHS)

kernel_name = 'hybrid_parallel_heads_dit_block'


def rms_norm(x, g):
    xf = x.astype(F32)
    y = xf * lax.rsqrt(jnp.mean(xf * xf, axis=-1, keepdims=True) + NORM_EPS)
    return (y * g.astype(F32)).astype(x.dtype)


def head_norm(y, gain):
    yf = y.astype(F32)
    yn = yf * lax.rsqrt(jnp.mean(yf * yf, axis=-1, keepdims=True) + NORM_EPS)
    return yn * gain.astype(F32).reshape(y.shape[-2], y.shape[-1])


def modulate(h, shift, scale):
    return h * (1.0 + scale) + shift


def axial_rope(rows, dim):
    row = jnp.repeat(jnp.arange(rows, dtype=F32), GRID_W)
    col = jnp.tile(jnp.arange(GRID_W, dtype=F32), rows)
    n_freq = dim // 4
    inv = ROPE_BASE ** (-jnp.arange(n_freq, dtype=F32) / n_freq)
    ang = jnp.concatenate([row[:, None] * inv, col[:, None] * inv], axis=-1)
    return jnp.cos(ang), jnp.sin(ang)


def line_rope(n, dim):
    n_freq = dim // 2
    inv = ROPE_BASE ** (-jnp.arange(n_freq, dtype=F32) / n_freq)
    ang = jnp.arange(n, dtype=F32)[:, None] * inv
    return jnp.cos(ang), jnp.sin(ang)


def apply_rope(x, cos, sin):
    shape = (1, cos.shape[0]) + (1,) * (x.ndim - 3) + (cos.shape[-1],)
    cos = cos.reshape(shape)
    sin = sin.reshape(shape)
    x1, x2 = jnp.split(x, 2, axis=-1)
    return jnp.concatenate([x1 * cos - x2 * sin, x2 * cos + x1 * sin], axis=-1)


def _flip(t):
    return jnp.flip(t, axis=1)


def _ident(t):
    return t


def split_proj(p):
    cuts = np.cumsum(PROJ_WIDTHS)[:-1].tolist()
    return jnp.split(p, cuts, axis=-1)


def _heads(t, h):
    return t.reshape(t.shape[:2] + (h, -1))


def _diff_heads(t):
    return t.reshape(t.shape[:2] + (DIFF_HEADS, 2, DIFF_DIM))


def swa_mixer(q, k, v, qc, kc, vc, sink, rope, need_ctx):
    B, N = q.shape[:2]
    M = kc.shape[1]
    G, R, d = SWA_KV_HEADS, SWA_HEADS // SWA_KV_HEADS, HEAD_DIM
    nb = N // BLOCK
    scale = d ** -0.5
    q = apply_rope(q.astype(F32), *rope) * scale
    k = apply_rope(k.astype(F32), *rope)
    v = v.astype(F32)
    kc = kc.astype(F32)
    vc = vc.astype(F32)
    sink = sink.astype(F32).reshape(G, R, 1, 1)
    pad = ((0, 0), (BLOCK, BLOCK), (0, 0), (0, 0))
    kp = jnp.pad(k, pad).reshape(B, nb + 2, BLOCK, G, d)
    vp = jnp.pad(v, pad).reshape(B, nb + 2, BLOCK, G, d)
    kw = jnp.concatenate([kp[:, :-2], kp[:, 1:-1], kp[:, 2:]], axis=2)
    vw = jnp.concatenate([vp[:, :-2], vp[:, 1:-1], vp[:, 2:]], axis=2)
    qpos = (jnp.arange(nb) * BLOCK)[:, None, None] + jnp.arange(BLOCK)[None, :, None]
    kpos = (jnp.arange(nb) * BLOCK - BLOCK)[:, None, None] + jnp.arange(3 * BLOCK)[None, None, :]
    valid = (jnp.abs(qpos - kpos) <= WINDOW) & (kpos >= 0) & (kpos < N)
    qb = q.reshape(B, nb, BLOCK, G, R, d)
    s_win = jnp.einsum('bnqgrd,bnkgd->bngrqk', qb, kw)
    s_win = jnp.where(valid[None, :, None, None], s_win, -jnp.inf)
    s_ctx = jnp.einsum('bnqgrd,bmgd->bngrqm', qb, kc)
    s_sink = jnp.broadcast_to(sink, s_win.shape[:-1] + (1,))
    p = jax.nn.softmax(jnp.concatenate([s_win, s_ctx, s_sink], axis=-1), axis=-1)
    nw = 3 * BLOCK
    o = (jnp.einsum('bngrqk,bnkgd->bnqgrd', p[..., :nw], vw)
         + jnp.einsum('bngrqm,bmgd->bnqgrd', p[..., nw:nw + M], vc))
    o = o.reshape(B, N, G * R * d)
    oc = None
    if need_ctx:
        qcg = (qc.astype(F32) * scale).reshape(B, M, G, R, d)
        s = jnp.einsum('bmgrd,bngd->bgrmn', qcg, kc)
        s = jnp.concatenate([s, jnp.broadcast_to(sink, s.shape[:-1] + (1,))], axis=-1)
        pc = jax.nn.softmax(s, axis=-1)
        oc = jnp.einsum('bgrmn,bngd->bmgrd', pc[..., :M], vc).reshape(B, M, G * R * d)
    return o, oc


def _ret_summaries(k, v, lg):
    B, L, H, d = k.shape
    C = RET_CHUNK
    kc = k.reshape(B, L // C, C, H, d)
    vc = v.reshape(B, L // C, C, H, v.shape[-1])
    idx = jnp.arange(C)
    zeta = jnp.exp(lg[:, None] * (C - 1 - idx)[None, :])
    return jnp.einsum('bcmhd,hm,bcmhe->bchde', kc, zeta, vc)


def _ret_scan(U, lg, s0):
    decay = jnp.exp(lg * RET_CHUNK)[None, :, None, None]

    def step(S, u):
        return decay * S + u, S

    s_fin, s_prev = lax.scan(step, s0, jnp.moveaxis(U, 1, 0))
    return s_fin, jnp.moveaxis(s_prev, 0, 1)


def _ret_output(q, k, v, s_prev, lg):
    B, L, H, d = q.shape
    C = RET_CHUNK
    nc = L // C
    e = v.shape[-1]
    qc = q.reshape(B, nc, C, H, d)
    kc = k.reshape(B, nc, C, H, d)
    vc = v.reshape(B, nc, C, H, e)
    idx = jnp.arange(C)
    diff = idx[:, None] - idx[None, :]
    dmask = jnp.where(diff >= 0, jnp.exp(lg[:, None, None] * jnp.maximum(diff, 0)), 0.0)
    s = jnp.einsum('bcnhd,bcmhd->bchnm', qc, kc) * dmask
    inner = jnp.einsum('bchnm,bcmhe->bcnhe', s, vc)
    xi = jnp.exp(lg[:, None] * (idx + 1)[None, :])
    cross = jnp.einsum('bcnhd,bchde->bcnhe', qc, s_prev) * xi.T[None, None, :, :, None]
    return (inner + cross).reshape(B, L, H, e)


def retention_mixer(q, k, v, g, qc, kc, vc, gc, decay_logit, gn, rope, need_ctx):
    B, N, H, d = q.shape
    scale = d ** -0.5
    q = apply_rope(q.astype(F32), *rope)
    k = apply_rope(k.astype(F32), *rope) * scale
    v = v.astype(F32)
    qc = qc.astype(F32)
    kc = kc.astype(F32) * scale
    vc = vc.astype(F32)
    log_gamma = -jax.nn.softplus(-decay_logit.astype(F32))
    y = 0.0
    yc = 0.0
    for direction in range(2):
        lg = log_gamma[direction]
        fl = _flip if direction == 1 else _ident
        s0 = jnp.zeros((B, H, d, v.shape[-1]), F32)
        sc_fin, sc_prev = _ret_scan(_ret_summaries(fl(kc), fl(vc), lg), lg, s0)
        _, s_prev = _ret_scan(_ret_summaries(fl(k), fl(v), lg), lg, sc_fin)
        y = y + fl(_ret_output(fl(q), fl(k), fl(v), s_prev, lg))
        if need_ctx:
            yc = yc + fl(_ret_output(fl(qc), fl(kc), fl(vc), sc_prev, lg))
    o = head_norm(y, gn).reshape(B, N, H * d) * jax.nn.silu(g.astype(F32))
    oc = None
    if need_ctx:
        oc = head_norm(yc, gn).reshape(B, kc.shape[1], H * d) * jax.nn.silu(gc.astype(F32))
    return o, oc


def _diff_attend(qb, k, v, lam):
    s = jnp.einsum('bqhmd,bkhmd->bhmqk', qb, k)
    p = jax.nn.softmax(s, axis=-1)
    w = p[:, :, 0] - lam * p[:, :, 1]
    return jnp.einsum('bhqk,bkhe->bqhe', w, v)


def diff_mixer(q, k, v, qc, kc, vc, lam_vecs, gn, lam_init, rope, need_ctx):
    B, N, H = q.shape[:3]
    M = kc.shape[1]
    scale = DIFF_DIM ** -0.5
    q = apply_rope(q.astype(F32), *rope) * scale
    k = apply_rope(k.astype(F32), *rope)
    v = v.astype(F32)
    kc = kc.astype(F32)
    vc = vc.astype(F32)
    lv = lam_vecs.astype(F32)
    lam = jnp.exp(jnp.sum(lv[0] * lv[1])) - jnp.exp(jnp.sum(lv[2] * lv[3])) + lam_init
    k_all = jnp.concatenate([k, kc], axis=1)
    v_all = jnp.concatenate([v, vc], axis=1)
    nb = N // BLOCK
    qblocks = jnp.moveaxis(q.reshape(B, nb, BLOCK, H, 2, DIFF_DIM), 1, 0)
    o = lax.map(lambda qb: _diff_attend(qb, k_all, v_all, lam), qblocks)
    o = jnp.moveaxis(o, 0, 1).reshape(B, N, H, v.shape[-1])
    o = (head_norm(o, gn) * (1.0 - lam_init)).reshape(B, N, -1)
    oc = None
    if need_ctx:
        occ = _diff_attend(qc.astype(F32) * scale, kc, vc, lam)
        oc = (head_norm(occ, gn) * (1.0 - lam_init)).reshape(B, M, -1)
    return o, oc


def dwconv(x, w, b):
    y = lax.conv_general_dilated(x, w[:, None, :].astype(x.dtype), (1,), [(1, 2)],
                                 dimension_numbers=('NWC', 'WIO', 'NWC'),
                                 feature_group_count=x.shape[-1])
    return y + b


def _lin_combine(left, right):
    a_l, b_l = left
    a_r, b_r = right
    return a_l * a_r, a_r * b_l + b_r


def rglru_scan(u, wa, ba, wx, bx, lam, h0):
    B, L, W = u.shape
    ub = u.reshape(B, L, LRU_BLOCKS, LRU_BLOCK)
    r = jax.nn.sigmoid(jnp.einsum('blhi,hij->blhj', ub, wa.astype(F32)).reshape(B, L, W) + ba.astype(F32))
    i = jax.nn.sigmoid(jnp.einsum('blhi,hij->blhj', ub, wx.astype(F32)).reshape(B, L, W) + bx.astype(F32))
    log_a = -LRU_C * r * jax.nn.softplus(-lam.astype(F32))
    a = jnp.exp(log_a)
    b = jnp.sqrt(-jnp.expm1(2.0 * log_a)) * (i * u)
    b = b.at[:, 0].add(a[:, 0] * h0)
    _, h = lax.associative_scan(_lin_combine, (a, b), axis=1)
    return h


def rglru_mixer(xb, yb, xbc, ybc, conv_w, conv_b, wa, ba, wx, bx, lam, need_ctx):
    u = dwconv(xb, conv_w, conv_b).astype(F32)
    uc = dwconv(xbc, conv_w, conv_b).astype(F32)
    B = u.shape[0]
    h = 0.0
    hc = 0.0
    for direction in range(2):
        fl = _flip if direction == 1 else _ident
        prm = (wa[direction], ba[direction], wx[direction], bx[direction], lam[direction])
        hc_seq = rglru_scan(fl(uc), *prm, jnp.zeros((B, LRU_WIDTH), F32))
        h = h + fl(rglru_scan(fl(u), *prm, hc_seq[:, -1]))
        if need_ctx:
            hc = hc + fl(hc_seq)
    o = h * jax.nn.gelu(yb.astype(F32))
    oc = hc * jax.nn.gelu(ybc.astype(F32)) if need_ctx else None
    return o, oc


def hybrid_mixer(p, pc, sink, ret_logit, ret_g, diff_lam, diff_g, lam_init,
                 conv_w, conv_b, lru_wa, lru_ba, lru_wx, lru_bx, lru_lambda,
                 rope_attn, rope_diff, rope_ret, need_ctx):
    (aq, ak, av, rq, rk, rv, rg, dq, dk, dv, lx, ly) = split_proj(p)
    (aqc, akc, avc, rqc, rkc, rvc, rgc, dqc, dkc, dvc, lxc, lyc) = split_proj(pc)
    oa, oac = swa_mixer(_heads(aq, SWA_HEADS), _heads(ak, SWA_KV_HEADS), _heads(av, SWA_KV_HEADS),
                        _heads(aqc, SWA_HEADS), _heads(akc, SWA_KV_HEADS), _heads(avc, SWA_KV_HEADS),
                        sink, rope_attn, need_ctx)
    ob, obc = retention_mixer(_heads(rq, RET_HEADS), _heads(rk, RET_HEADS), _heads(rv, RET_HEADS), rg,
                              _heads(rqc, RET_HEADS), _heads(rkc, RET_HEADS), _heads(rvc, RET_HEADS), rgc,
                              ret_logit, ret_g, rope_ret, need_ctx)
    od, odc = diff_mixer(_diff_heads(dq), _diff_heads(dk), _heads(dv, DIFF_HEADS),
                         _diff_heads(dqc), _diff_heads(dkc), _heads(dvc, DIFF_HEADS),
                         diff_lam, diff_g, lam_init, rope_diff, need_ctx)
    ol, olc = rglru_mixer(lx, ly, lxc, lyc, conv_w, conv_b, lru_wa, lru_ba, lru_wx, lru_bx,
                          lru_lambda, need_ctx)
    o = jnp.concatenate([oa, ob, od, ol], axis=-1).astype(p.dtype)
    o_ctx = jnp.concatenate([oac, obc, odc, olc], axis=-1).astype(pc.dtype) if need_ctx else None
    return o, o_ctx


def swiglu(h, w1, w3, w2):
    return (jax.nn.silu(h @ w1) * (h @ w3)) @ w2


def moe_swiglu(h, router, w1, w3, w2):
    logits = jnp.einsum('bld,de->ble', h, router).astype(F32)
    top_val, top_idx = lax.top_k(logits, TOP_K)
    top_w = jax.nn.softmax(top_val, axis=-1)
    gate = jnp.sum(jax.nn.one_hot(top_idx, N_EXPERTS, dtype=F32) * top_w[..., None], axis=-2)
    out = jnp.zeros(h.shape, F32)
    for e in range(N_EXPERTS):
        out = out + gate[..., e:e + 1] * swiglu(h, w1[e], w3[e], w2[e])
    return out.astype(h.dtype)


def setup_inputs(seed: int = 0) -> dict:
    key = jax.random.key(seed)
    keys = jax.random.split(key, 32)
    counter = [0]

    def nrm(shape, std):
        k = keys[counter[0]]
        counter[0] += 1
        return jax.random.normal(k, shape, F32) * std

    D = D_MODEL
    n_dense = (DEPTH + 1) // 2
    n_moe = DEPTH // 2
    gam = 1.0 - 2.0 ** (-5.0 - jnp.arange(RET_HEADS, dtype=F32))
    ret_base = jnp.log(gam) - jnp.log1p(-gam)
    a_c = jax.random.uniform(keys[31], (DEPTH, 2, LRU_WIDTH), F32, 0.9, 0.999)
    a = a_c ** (1.0 / LRU_C)
    return {
        'x': nrm((BATCH, SEQ, D), 1.0),
        'c': nrm((BATCH, D), 1.0),
        'ctx': nrm((BATCH, CTX_LEN, D), 1.0),
        'c_ctx': nrm((D,), 1.0),
        'w_ada': nrm((DEPTH, D, 6 * D), 0.5 * D ** -0.5),
        'b_ada': nrm((DEPTH, 6 * D), 0.02),
        'g_norm1': 1.0 + nrm((DEPTH, D), 0.05),
        'g_norm2': 1.0 + nrm((DEPTH, D), 0.05),
        'w_in': nrm((DEPTH, D, D_IN), D ** -0.5),
        'w_out': nrm((DEPTH, D, D), D ** -0.5),
        'attn_sink': nrm((DEPTH, SWA_HEADS), 0.5),
        'ret_decay_logit': ret_base + nrm((DEPTH, 2, RET_HEADS), 0.1),
        'ret_gn': 1.0 + nrm((DEPTH, RET_HEADS * HEAD_DIM), 0.05),
        'diff_lambda': nrm((DEPTH, 4, DIFF_DIM), 0.1),
        'diff_gn': 1.0 + nrm((DEPTH, DIFF_HEADS * HEAD_DIM), 0.05),
        'conv_w': nrm((DEPTH, CONV_WIDTH, LRU_WIDTH), CONV_WIDTH ** -0.5),
        'conv_b': nrm((DEPTH, LRU_WIDTH), 0.02),
        'lru_wa': nrm((DEPTH, 2, LRU_BLOCKS, LRU_BLOCK, LRU_BLOCK), LRU_BLOCK ** -0.5),
        'lru_ba': nrm((DEPTH, 2, LRU_WIDTH), 0.02),
        'lru_wx': nrm((DEPTH, 2, LRU_BLOCKS, LRU_BLOCK, LRU_BLOCK), LRU_BLOCK ** -0.5),
        'lru_bx': nrm((DEPTH, 2, LRU_WIDTH), 0.02),
        'lru_lambda': jnp.log(a) - jnp.log1p(-a),
        'ffn_w1': nrm((n_dense, D, D_FF), D ** -0.5),
        'ffn_w3': nrm((n_dense, D, D_FF), D ** -0.5),
        'ffn_w2': nrm((n_dense, D_FF, D), D_FF ** -0.5),
        'moe_router': nrm((n_moe, D, N_EXPERTS), D ** -0.5),
        'moe_w1': nrm((n_moe, N_EXPERTS, D, D_FF_EXPERT), D ** -0.5),
        'moe_w3': nrm((n_moe, N_EXPERTS, D, D_FF_EXPERT), D ** -0.5),
        'moe_w2': nrm((n_moe, N_EXPERTS, D_FF_EXPERT, D), D_FF_EXPERT ** -0.5),
        'final_norm': 1.0 + nrm((D,), 0.05),
    }


def reference(x, c, ctx, c_ctx, w_ada, b_ada, g_norm1, g_norm2, w_in, w_out,
              attn_sink, ret_decay_logit, ret_gn, diff_lambda, diff_gn,
              conv_w, conv_b, lru_wa, lru_ba, lru_wx, lru_bx, lru_lambda,
              ffn_w1, ffn_w3, ffn_w2, moe_router, moe_w1, moe_w3, moe_w2, final_norm):
    n_lat = x.shape[1]
    rows = n_lat // GRID_W
    rope_attn = axial_rope(rows, HEAD_DIM)
    rope_diff = axial_rope(rows, DIFF_DIM)
    rope_ret = line_rope(n_lat, HEAD_DIM)
    xc = ctx
    for layer in range(DEPTH):
        need_ctx = layer < DEPTH - 1
        lam_init = 0.8 - 0.6 * math.exp(-0.3 * layer)
        mod = (jax.nn.silu(c) @ w_ada[layer] + b_ada[layer])[:, None, :]
        mod_c = jax.nn.silu(c_ctx) @ w_ada[layer] + b_ada[layer]
        sh1, sc1, gt1, sh2, sc2, gt2 = jnp.split(mod, 6, axis=-1)
        sh1c, sc1c, gt1c, sh2c, sc2c, gt2c = jnp.split(mod_c, 6, axis=-1)
        h = modulate(rms_norm(x, g_norm1[layer]), sh1, sc1)
        hc = modulate(rms_norm(xc, g_norm1[layer]), sh1c, sc1c)
        o, o_ctx = hybrid_mixer(h @ w_in[layer], hc @ w_in[layer],
                                attn_sink[layer], ret_decay_logit[layer], ret_gn[layer],
                                diff_lambda[layer], diff_gn[layer], lam_init,
                                conv_w[layer], conv_b[layer], lru_wa[layer], lru_ba[layer],
                                lru_wx[layer], lru_bx[layer], lru_lambda[layer],
                                rope_attn, rope_diff, rope_ret, need_ctx)
        x = x + gt1 * (o @ w_out[layer])
        j = layer // 2
        if layer % 2 == 0:
            ffn = lambda t: swiglu(t, ffn_w1[j], ffn_w3[j], ffn_w2[j])
        else:
            ffn = lambda t: moe_swiglu(t, moe_router[j], moe_w1[j], moe_w3[j], moe_w2[j])
        x = x + gt2 * ffn(modulate(rms_norm(x, g_norm2[layer]), sh2, sc2))
        if need_ctx:
            xc = xc + gt1c * (o_ctx @ w_out[layer])
            xc = xc + gt2c * ffn(modulate(rms_norm(xc, g_norm2[layer]), sh2c, sc2c))
    return rms_norm(x, final_norm)
```

```python
import functools
import math

import jax
import jax.numpy as jnp
from jax import lax
from jax.experimental import pallas as pl
from jax.experimental.pallas import tpu as pltpu

F32 = jnp.float32
BF16 = jnp.bfloat16

D_MODEL = 1024
DEPTH = 2
GRID_W = 64
HEAD_DIM = 64
GROUP = D_MODEL // 4
N_HEADS = GROUP // HEAD_DIM
DIFF_DIM = HEAD_DIM // 2
WINDOW = 128
NORM_EPS = 1e-6
ROPE_BASE = 10000.0
LRU_BLOCKS = 4
LRU_C = 8.0
N_EXPERTS = 8
LANE = 128
LOG2E = 1.4426950408889634

OFF_AQ, OFF_AK, OFF_AV = 0, 256, 384
OFF_RQ, OFF_RK, OFF_RV, OFF_RG = 512, 768, 1024, 1280
OFF_DQ, OFF_DK, OFF_DV = 1536, 1792, 2048
OFF_LX, OFF_LY = 2304, 2560
D_IN = 2816

VMEM_LIMIT = 56 * 1024 * 1024


def _params(*sem):
    return pltpu.CompilerParams(dimension_semantics=sem, vmem_limit_bytes=VMEM_LIMIT)


def _silu(x):
    return x * jax.nn.sigmoid(x)


def _softplus(x):
    return jnp.maximum(x, 0.0) + jnp.log(1.0 + jnp.exp(-jnp.abs(x)))


def _dot(a, b):
    return jnp.dot(a, b, preferred_element_type=F32)


def _dot_nt(a, b):
    return lax.dot_general(a, b, (((1,), (1,)), ((), ())), preferred_element_type=F32)


def _mod_kernel(c_ref, w_ref, b_ref, o_ref):
    s = _silu(c_ref[...])
    o_ref[0] = _dot(s.astype(BF16), w_ref[0].astype(BF16)) + b_ref[0]


def _mod_vectors(cc, w_ada, b_ada):
    depth, d, n6 = w_ada.shape
    tn = 1536
    return pl.pallas_call(
        _mod_kernel,
        out_shape=jax.ShapeDtypeStruct((depth, 8, n6), F32),
        grid=(depth, n6 // tn),
        in_specs=[pl.BlockSpec((8, d), lambda l, j: (0, 0)),
                  pl.BlockSpec((1, d, tn), lambda l, j: (l, 0, j)),
                  pl.BlockSpec((1, 1, tn), lambda l, j: (l, 0, j))],
        out_specs=pl.BlockSpec((1, 8, tn), lambda l, j: (l, 0, j)),
        compiler_params=_params("arbitrary", "arbitrary"),
        name="mod_vectors",
    )(cc, w_ada, b_ada.reshape(depth, 1, n6))


def _rope_tile(x, cos, sin_signed, d):
    lane = lax.broadcasted_iota(jnp.int32, x.shape, 1)
    first = (lane & (d - 1)) < (d // 2)
    partner = jnp.where(first, pltpu.roll(x, LANE - d // 2, axis=1), pltpu.roll(x, d // 2, axis=1))
    return x * cos + partner * sin_signed


def _inproj_kernel(*refs, rope, tm):
    if rope:
        (x_ref, g_ref, sh_ref, sc_ref, w_ref, ca_ref, sa_ref, cr_ref, sr_ref, cd_ref, sd_ref,
         qa_ref, ka_ref, va_ref, rq_ref, rk_ref, rv_ref, rg_ref,
         dqt_ref, dk_ref, dvt_ref, lx_ref, ly_ref) = refs
    else:
        (x_ref, g_ref, sh_ref, sc_ref, w_ref,
         qa_ref, ka_ref, va_ref, rq_ref, rk_ref, rv_ref, rg_ref,
         dqt_ref, dk_ref, dvt_ref, lx_ref, ly_ref) = refs
    x = x_ref[0]
    ms = jnp.mean(x * x, axis=-1, keepdims=True)
    y = x * lax.rsqrt(ms + NORM_EPS) * g_ref[...]
    h = y * (1.0 + sc_ref[0]) + sh_ref[0]
    hb = h.astype(BF16)

    def proj(off, width):
        return _dot(hb, w_ref[:, off:off + width])

    def roped(p, c_ref, s_ref, d, scale):
        tiles = []
        for t in range(p.shape[1] // LANE):
            pt = p[:, t * LANE:(t + 1) * LANE]
            if rope:
                pt = _rope_tile(pt, c_ref[...], s_ref[...], d)
            tiles.append(pt if scale == 1.0 else pt * scale)
        return tiles[0] if len(tiles) == 1 else jnp.concatenate(tiles, axis=1)

    ca = sa = cr = sr = cd = sd = None
    if rope:
        ca, sa, cr, sr, cd, sd = ca_ref, sa_ref, cr_ref, sr_ref, cd_ref, sd_ref
    qa_ref[0] = roped(proj(OFF_AQ, 256), ca, sa, HEAD_DIM, HEAD_DIM ** -0.5).astype(BF16)
    ka_ref[0] = roped(proj(OFF_AK, 128), ca, sa, HEAD_DIM, 1.0).astype(BF16)
    va_ref[0] = proj(OFF_AV, 128).astype(BF16)
    rq_ref[0] = roped(proj(OFF_RQ, 256), cr, sr, HEAD_DIM, 1.0).astype(BF16)
    rk_ref[0] = roped(proj(OFF_RK, 256), cr, sr, HEAD_DIM, HEAD_DIM ** -0.5).astype(BF16)
    rv_ref[0] = proj(OFF_RV, 256).astype(BF16)
    rg_ref[0] = proj(OFF_RG, 256)
    dq = roped(proj(OFF_DQ, 256), cd, sd, DIFF_DIM, DIFF_DIM ** -0.5 * LOG2E)
    dqt_ref[0] = dq.T.astype(BF16)
    dk_ref[0] = roped(proj(OFF_DK, 256), cd, sd, DIFF_DIM, 1.0).astype(BF16)
    dvt = proj(OFF_DV, 256).T.astype(BF16)
    for j in range(tm // 256):
        dvt_ref[0, j] = dvt[:, j * 256:(j + 1) * 256]
    lx_ref[0] = proj(OFF_LX, 256)
    ly_ref[0] = proj(OFF_LY, 256)


def _in_projection(x, g, sh, sc, w_in, tables):
    b, l, d = x.shape
    rope = tables is not None
    tm = 512 if l % 512 == 0 else 256
    nt = l // tm
    row = lambda shape: pl.BlockSpec(shape, lambda i, j: (i, j, 0))
    vec = pl.BlockSpec((1, 1, d), lambda i, j: (i, 0, 0))
    in_specs = [row((1, tm, d)), pl.BlockSpec((1, d), lambda i, j: (0, 0)), vec, vec,
                pl.BlockSpec((d, D_IN), lambda i, j: (0, 0))]
    args = [x, g.reshape(1, d), sh, sc, w_in]
    if rope:
        in_specs += [pl.BlockSpec((tm, LANE), lambda i, j: (j, 0))] * 6
        args += list(tables)
    sds = jax.ShapeDtypeStruct
    out_shape = [sds((b, l, 256), BF16), sds((b, l, 128), BF16), sds((b, l, 128), BF16),
                 sds((b, l, 256), BF16), sds((b, l, 256), BF16), sds((b, l, 256), BF16),
                 sds((b, l, 256), F32),
                 sds((b, 256, l), BF16), sds((b, l, 256), BF16), sds((b, l // 256, 256, 256), BF16),
                 sds((b, l, 256), F32), sds((b, l, 256), F32)]
    out_specs = [row((1, tm, 256)), row((1, tm, 128)), row((1, tm, 128)),
                 row((1, tm, 256)), row((1, tm, 256)), row((1, tm, 256)), row((1, tm, 256)),
                 pl.BlockSpec((1, 256, tm), lambda i, j: (i, 0, j)), row((1, tm, 256)),
                 pl.BlockSpec((1, tm // 256, 256, 256), lambda i, j: (i, j, 0, 0)),
                 row((1, tm, 256)), row((1, tm, 256))]
    return pl.pallas_call(
        functools.partial(_inproj_kernel, rope=rope, tm=tm),
        out_shape=out_shape, grid=(b, nt), in_specs=in_specs, out_specs=out_specs,
        compiler_params=_params("arbitrary", "arbitrary"),
        name="in_projection_rope" if rope else "in_projection_ctx",
    )(*args)


def _swa_kernel(*refs, window, tq, n):
    if window:
        sink_ref, q_ref, k_ref, v_ref, kc_ref, vc_ref, o_ref = refs
    else:
        sink_ref, q_ref, kc_ref, vc_ref, o_ref = refs
    kc = kc_ref[0]
    vc = vc_ref[0]
    m = kc.shape[0]
    if window:
        i = pl.program_id(1)
        w = tq + 2 * WINDOW
        t0 = i * tq
        start = pl.multiple_of(jnp.clip(t0 - WINDOW, 0, n - w), WINDOW)
        kcat = jnp.concatenate([k_ref[0, pl.ds(start, w), :], kc], axis=0)
        vcat = jnp.concatenate([v_ref[0, pl.ds(start, w), :], vc], axis=0)
        qpos = t0 + lax.broadcasted_iota(jnp.int32, (tq, w + m), 0)
        col = lax.broadcasted_iota(jnp.int32, (tq, w + m), 1)
        valid = (jnp.abs(qpos - (start + col)) <= WINDOW) | (col >= w)
    else:
        kcat, vcat, valid = kc, vc, None
    q = q_ref[0]
    lane = lax.broadcasted_iota(jnp.int32, (1, LANE), 1)
    out_tiles = []
    for t in range(2):
        qt = q[:, t * LANE:(t + 1) * LANE]
        out_t = None
        for half in range(2):
            in_half = (lane >= half * HEAD_DIM) & (lane < (half + 1) * HEAD_DIM)
            sink = sink_ref[2 * t + half]
            qm = jnp.where(in_half, qt, jnp.zeros_like(qt))
            s = _dot_nt(qm, kcat)
            if valid is not None:
                s = jnp.where(valid, s, -jnp.inf)
            mx = jnp.maximum(jnp.max(s, axis=1, keepdims=True), sink)
            e = jnp.exp(s - mx)
            den = jnp.sum(e, axis=1, keepdims=True) + jnp.exp(sink - mx)
            o = _dot(e.astype(BF16), vcat) / den
            out_t = o if out_t is None else jnp.where(in_half, o, out_t)
        out_tiles.append(out_t)
    o_ref[0] = jnp.concatenate(out_tiles, axis=1)


def _swa_attention(sink_perm, q, k, v, kc, vc):
    b, n, _ = q.shape
    m = kc.shape[1]
    tq = 256
    smem = pl.BlockSpec(memory_space=pltpu.SMEM)
    ctx = pl.BlockSpec((1, m, 128), lambda i, j: (i, 0, 0))
    in_specs = [smem, pl.BlockSpec((1, tq, 256), lambda i, j: (i, j, 0)),
                pl.BlockSpec((1, n, 128), lambda i, j: (i, 0, 0)),
                pl.BlockSpec((1, n, 128), lambda i, j: (i, 0, 0)), ctx, ctx]
    return pl.pallas_call(
        functools.partial(_swa_kernel, window=True, tq=tq, n=n),
        out_shape=jax.ShapeDtypeStruct((b, n, 256), F32),
        grid=(b, n // tq), in_specs=in_specs,
        out_specs=pl.BlockSpec((1, tq, 256), lambda i, j: (i, j, 0)),
        compiler_params=_params("arbitrary", "arbitrary"),
        name="swa_attention",
    )(sink_perm, q, k, v, kc, vc)


def _ctx_attention(sink_perm, qc, kc, vc):
    b, m, _ = qc.shape
    smem = pl.BlockSpec(memory_space=pltpu.SMEM)
    ctx = pl.BlockSpec((1, m, 128), lambda i: (i, 0, 0))
    return pl.pallas_call(
        functools.partial(_swa_kernel, window=False, tq=m, n=m),
        out_shape=jax.ShapeDtypeStruct((b, m, 256), F32),
        grid=(b,), in_specs=[smem, pl.BlockSpec((1, m, 256), lambda i: (i, 0, 0)), ctx, ctx],
        out_specs=pl.BlockSpec((1, m, 256), lambda i: (i, 0, 0)),
        compiler_params=_params("arbitrary"),
        name="ctx_attention",
    )(sink_perm, qc, kc, vc)


RET_CHUNK = 128


def _ret_kernel(dl_lane_ref, dl_head_ref, q_ref, k_ref, v_ref, s0_ref, y_ref, sfin_ref, s_scr, *, nc):
    c = RET_CHUNK
    d = pl.program_id(1)
    ci = pl.program_id(2)
    fwd = d == 0

    @pl.when(ci == 0)
    def _():
        s_scr[...] = s0_ref[0, 0]

    q = q_ref[0]
    k = k_ref[0]
    v = v_ref[0]
    lgl = -_softplus(-dl_lane_ref[0])
    row = lax.broadcasted_iota(jnp.int32, (c, 1), 0)
    xe = jnp.where(fwd, row + 1, c - row).astype(F32)
    ze = jnp.where(fwd, c - 1 - row, row).astype(F32)
    xi = jnp.exp(lgl * xe)
    zeta = jnp.exp(lgl * ze)
    cross = _dot(q, s_scr[...].astype(BF16)) * xi
    ni = lax.broadcasted_iota(jnp.int32, (c, c), 0)
    mi = lax.broadcasted_iota(jnp.int32, (c, c), 1)
    dd = jnp.where(fwd, ni - mi, mi - ni)
    ddf = jnp.maximum(dd, 0).astype(F32)
    lane_head = lax.broadcasted_iota(jnp.int32, (1, GROUP), 1) // HEAD_DIM
    inner = jnp.zeros((c, GROUP), F32)
    for h in range(N_HEADS):
        lgh = -_softplus(-dl_head_ref[0, h][0:1, :])
        dm = jnp.where(dd >= 0, jnp.exp(lgh * ddf), 0.0)
        qh = jnp.where(lane_head == h, q, jnp.zeros_like(q))
        s = _dot_nt(qh, k) * dm
        ih = _dot(s.astype(BF16), v)
        inner = jnp.where(lane_head == h, ih, inner)
    y_ref[0, 0] = inner + cross
    kz = (k.astype(F32) * zeta).T.astype(BF16)
    u = _dot(kz, v)
    dec = jnp.exp(lgl * float(c))
    rh = lax.broadcasted_iota(jnp.int32, (GROUP, GROUP), 0) // HEAD_DIM
    ch = lax.broadcasted_iota(jnp.int32, (GROUP, GROUP), 1) // HEAD_DIM
    s_scr[...] = jnp.where(rh == ch, dec * s_scr[...] + u, 0.0)

    @pl.when(ci == nc - 1)
    def _():
        sfin_ref[0, 0] = s_scr[...]


def _retention(dl_lane, dl_head, q, k, v, s0):
    b, l, _ = q.shape
    c = RET_CHUNK
    nc = l // c
    cidx = lambda i, d, j: (i, jnp.where(d == 0, j, nc - 1 - j), 0)
    tok = pl.BlockSpec((1, c, 256), cidx)
    st = pl.BlockSpec((1, 1, 256, 256), lambda i, d, j: (i, d, 0, 0))
    return pl.pallas_call(
        functools.partial(_ret_kernel, nc=nc),
        out_shape=[jax.ShapeDtypeStruct((b, 2, l, 256), F32),
                   jax.ShapeDtypeStruct((b, 2, 256, 256), F32)],
        grid=(b, 2, nc),
        in_specs=[pl.BlockSpec((1, 1, 256), lambda i, d, j: (d, 0, 0)),
                  pl.BlockSpec((1, N_HEADS, 8, c), lambda i, d, j: (d, 0, 0, 0)),
                  tok, tok, tok, st],
        out_specs=[pl.BlockSpec((1, 1, c, 256), lambda i, d, j: (i, d, jnp.where(d == 0, j, nc - 1 - j), 0)),
                   st],
        scratch_shapes=[pltpu.VMEM((GROUP, GROUP), F32)],
        compiler_params=_params("arbitrary", "arbitrary", "arbitrary"),
        name="retention",
    )(dl_lane, dl_head, q, k, v, s0)


DIFF_TQ = 256
DIFF_TK = 256


def _inner_tiles(tiles):
    for g in (8, 7, 6, 5, 4, 3, 2):
        if tiles % g == 0:
            return g
    return 1


def _diff_kernel(lam_ref, gn_ref, qt_ref, k_ref, vt_ref, o_ref, *, tiles, lam_init):
    h = pl.program_id(1)
    inner = _inner_tiles(tiles)
    outer = tiles // inner
    tq = qt_ref.shape[2]
    lv = lam_ref[...]
    lam = (jnp.exp(jnp.sum(lv[0:1] * lv[1:2], axis=1, keepdims=True))
           - jnp.exp(jnp.sum(lv[2:3] * lv[3:4], axis=1, keepdims=True)) + lam_init)
    qt = qt_ref[0]
    rowi = lax.broadcasted_iota(jnp.int32, (LANE, 1), 0)
    hoff = (h % 2) * HEAD_DIM
    maps = []
    for mp in range(2):
        lo = hoff + mp * DIFF_DIM
        qpad = jnp.where((rowi >= lo) & (rowi < lo + DIFF_DIM), qt, jnp.zeros_like(qt))

        def body(c, carry, qpad=qpad):
            m_run, l_run, acc = carry
            base = pl.multiple_of(c * (inner * DIFF_TK), DIFF_TK)
            s = _dot(k_ref[0, pl.ds(base, inner * DIFF_TK), :], qpad)
            m_new = jnp.maximum(m_run, jnp.max(s, axis=0, keepdims=True))
            alpha = jnp.exp2(m_run - m_new)
            p = jnp.exp2(s - m_new)
            l_new = alpha * l_run + jnp.sum(p, axis=0, keepdims=True)
            pb = p.astype(BF16)
            pv = _dot(vt_ref[0, c * inner], pb[0:DIFF_TK])
            for j in range(1, inner):
                pv = pv + _dot(vt_ref[0, c * inner + j], pb[j * DIFF_TK:(j + 1) * DIFF_TK])
            return m_new, l_new, alpha * acc + pv

        init = (jnp.full((1, tq), -jnp.inf, F32), jnp.zeros((1, tq), F32),
                jnp.zeros((HEAD_DIM, tq), F32))
        _, l_fin, acc = lax.fori_loop(0, outer, body, init)
        maps.append(acc / l_fin)
    o = maps[0] - lam * maps[1]
    ms = jnp.mean(o * o, axis=0, keepdims=True)
    o_ref[0] = o * lax.rsqrt(ms + NORM_EPS) * (gn_ref[0] * (1.0 - lam_init))


def _diff_attention(lam_vecs, gn_col, qt, k_all, vt_all, lam_init):
    b, _, lq = qt.shape
    lk = k_all.shape[1]
    tiles = lk // DIFF_TK
    tq = DIFF_TQ
    return pl.pallas_call(
        functools.partial(_diff_kernel, tiles=tiles, lam_init=lam_init),
        out_shape=jax.ShapeDtypeStruct((b, 256, lq), F32),
        grid=(b, N_HEADS, lq // tq),
        in_specs=[pl.BlockSpec((4, DIFF_DIM), lambda i, h, j: (0, 0)),
                  pl.BlockSpec((1, HEAD_DIM, 1), lambda i, h, j: (h, 0, 0)),
                  pl.BlockSpec((1, LANE, tq), lambda i, h, j: (i, h // 2, j)),
                  pl.BlockSpec((1, lk, LANE), lambda i, h, j: (i, 0, h // 2)),
                  pl.BlockSpec((1, tiles, HEAD_DIM, DIFF_TK), lambda i, h, j: (i, 0, h, 0))],
        out_specs=pl.BlockSpec((1, HEAD_DIM, tq), lambda i, h, j: (i, h, j)),
        compiler_params=_params("arbitrary", "arbitrary", "arbitrary"),
        name="diff_attention",
    )(lam_vecs, gn_col, qt, k_all, vt_all)


def _lru_kernel(x_ref, xp_ref, xn_ref, cw_ref, cb_ref, wa_ref, ba_ref, wx_ref, bx_ref, lam_ref,
                h0_ref, h_ref, carry, *, reverse, tm, nt):
    j = pl.program_id(1)
    tix = (nt - 1 - j) if reverse else j

    @pl.when(j == 0)
    def _():
        carry[...] = jnp.broadcast_to(h0_ref[0], carry.shape)

    x = x_ref[0]
    row = lax.broadcasted_iota(jnp.int32, (tm, 1), 0)
    prev = jnp.where(tix > 0, xp_ref[0][7:8, :], 0.0)
    nxt0 = jnp.where(tix < nt - 1, xn_ref[0][0:1, :], 0.0)
    nxt1 = jnp.where(tix < nt - 1, xn_ref[0][1:2, :], 0.0)
    xm1 = jnp.where(row == 0, prev, pltpu.roll(x, 1, axis=0))
    xp1 = jnp.where(row == tm - 1, nxt0, pltpu.roll(x, tm - 1, axis=0))
    xp2 = jnp.where(row == tm - 1, nxt1, jnp.where(row == tm - 2, nxt0, pltpu.roll(x, tm - 2, axis=0)))
    cw = cw_ref[...]
    u = cw[0:1] * xm1 + cw[1:2] * x + cw[2:3] * xp1 + cw[3:4] * xp2 + cb_ref[...]
    ub = u.astype(BF16)
    r = jax.nn.sigmoid(_dot(ub, wa_ref[0]) + ba_ref[0])
    gi = jax.nn.sigmoid(_dot(ub, wx_ref[0]) + bx_ref[0])
    log_a = -LRU_C * r * _softplus(-lam_ref[0])
    a = jnp.exp(log_a)
    th = jnp.tanh(log_a)
    bv = jnp.sqrt(-2.0 * th / (1.0 - th)) * (gi * u)
    k = 1
    while k < tm:
        if reverse:
            keep = row < tm - k
            shift = tm - k
        else:
            keep = row >= k
            shift = k
        a_sh = jnp.where(keep, pltpu.roll(a, shift, axis=0), 1.0)
        b_sh = jnp.where(keep, pltpu.roll(bv, shift, axis=0), 0.0)
        bv = a * b_sh + bv
        a = a * a_sh
        k *= 2
    hcur = a * carry[0:1, :] + bv
    h_ref[0] = hcur
    last = hcur[0:1, :] if reverse else hcur[tm - 1:tm, :]
    carry[...] = jnp.broadcast_to(last, carry.shape)


def _lru_scan(lx, conv_w, conv_b, wa_bd, ba, wx_bd, bx, lam, h0, reverse):
    b, l, w = lx.shape
    tm = 512 if l % 512 == 0 else 256
    nt = l // tm
    tb = tm // 8
    tidx = (lambda j: nt - 1 - j) if reverse else (lambda j: j)
    vec = pl.BlockSpec((1, w), lambda i, j: (0, 0))
    mat = pl.BlockSpec((1, w, w), lambda i, j: (0, 0, 0))
    return pl.pallas_call(
        functools.partial(_lru_kernel, reverse=reverse, tm=tm, nt=nt),
        out_shape=jax.ShapeDtypeStruct((b, l, w), F32),
        grid=(b, nt),
        in_specs=[pl.BlockSpec((1, tm, w), lambda i, j: (i, tidx(j), 0)),
                  pl.BlockSpec((1, 8, w), lambda i, j: (i, jnp.maximum(tidx(j) * tb - 1, 0), 0)),
                  pl.BlockSpec((1, 8, w), lambda i, j: (i, jnp.minimum((tidx(j) + 1) * tb, l // 8 - 1), 0)),
                  pl.BlockSpec((4, w), lambda i, j: (0, 0)), vec,
                  mat, vec, mat, vec, vec,
                  pl.BlockSpec((1, 1, w), lambda i, j: (i, 0, 0))],
        out_specs=pl.BlockSpec((1, tm, w), lambda i, j: (i, tidx(j), 0)),
        scratch_shapes=[pltpu.VMEM((8, w), F32)],
        compiler_params=_params("arbitrary", "arbitrary"),
        name="lru_scan_rev" if reverse else "lru_scan_fwd",
    )(lx, lx, lx, conv_w, conv_b.reshape(1, w), wa_bd.reshape(1, w, w), ba.reshape(1, w),
      wx_bd.reshape(1, w, w), bx.reshape(1, w), lam.reshape(1, w), h0)


def _outproj_kernel(x_ref, gt_ref, oa_ref, ry_ref, rg_ref, rgn_ref, dot_ref, hf_ref, hb_ref, ly_ref,
                    bd_ref, w_ref, o_ref):
    oa = oa_ref[0]
    y = ry_ref[0, 0] + ry_ref[0, 1]
    y2 = y * y
    hi = y2.astype(BF16)
    lo = (y2 - hi.astype(F32)).astype(BF16)
    ms = _dot(hi, bd_ref[...]) + _dot(lo, bd_ref[...])
    ob = y * lax.rsqrt(ms + NORM_EPS) * rgn_ref[...] * _silu(rg_ref[0])
    od = dot_ref[0].T
    ol = (hf_ref[0] + hb_ref[0]) * jax.nn.gelu(ly_ref[0])
    acc = _dot(oa.astype(BF16), w_ref[0:256, :])
    acc = acc + _dot(ob.astype(BF16), w_ref[256:512, :])
    acc = acc + _dot(od.astype(BF16), w_ref[512:768, :])
    acc = acc + _dot(ol.astype(BF16), w_ref[768:1024, :])
    o_ref[0] = x_ref[0] + gt_ref[0] * acc


def _out_projection(x, gt, oa, ry, rg, ret_gn, dot, hf, hb, ly, w_out):
    b, l, d = x.shape
    tm = 512 if l % 512 == 0 else 256
    row = lambda wdt: pl.BlockSpec((1, tm, wdt), lambda i, j: (i, j, 0))
    hd = jnp.arange(GROUP) // HEAD_DIM
    bd = jnp.where(hd[:, None] == hd[None, :], 1.0 / HEAD_DIM, 0.0).astype(BF16)
    return pl.pallas_call(
        _outproj_kernel,
        out_shape=jax.ShapeDtypeStruct((b, l, d), F32),
        grid=(b, l // tm),
        in_specs=[row(d), pl.BlockSpec((1, 1, d), lambda i, j: (i, 0, 0)), row(256),
                  pl.BlockSpec((1, 2, tm, 256), lambda i, j: (i, 0, j, 0)), row(256),
                  pl.BlockSpec((1, 256), lambda i, j: (0, 0)),
                  pl.BlockSpec((1, 256, tm), lambda i, j: (i, 0, j)),
                  row(256), row(256), row(256),
                  pl.BlockSpec((256, 256), lambda i, j: (0, 0)),
                  pl.BlockSpec((d, d), lambda i, j: (0, 0))],
        out_specs=row(d),
        compiler_params=_params("arbitrary", "arbitrary"),
        name="out_projection",
    )(x, gt, oa, ry, rg, ret_gn.reshape(1, 256), dot, hf, hb, ly, bd, w_out)


def _ffn_kernel(*refs, moe, n_e, n_f):
    if moe:
        (x_ref, g_ref, sh_ref, sc_ref, gt_ref, rt_ref, w1_ref, w3_ref, w2_ref, o_ref,
         hb_scr, acc_scr, gate_scr) = refs
    else:
        (x_ref, g_ref, sh_ref, sc_ref, gt_ref, w1_ref, w3_ref, w2_ref, o_ref, hb_scr, acc_scr) = refs
    e = pl.program_id(2)
    f = pl.program_id(3)

    @pl.when((e == 0) & (f == 0))
    def _():
        x = x_ref[0]
        ms = jnp.mean(x * x, axis=-1, keepdims=True)
        y = x * lax.rsqrt(ms + NORM_EPS) * g_ref[...]
        h = y * (1.0 + sc_ref[0]) + sh_ref[0]
        hb = h.astype(BF16)
        hb_scr[...] = hb
        acc_scr[...] = jnp.zeros_like(acc_scr)
        if moe:
            rt = rt_ref[...]
            rhi = rt.astype(BF16)
            rlo = (rt - rhi.astype(F32)).astype(BF16)
            hlo = (h - hb.astype(F32)).astype(BF16)
            logits = _dot(hb, rhi) + _dot(hb, rlo) + _dot(hlo, rhi)
            lane = lax.broadcasted_iota(jnp.int32, logits.shape, 1)
            logits = jnp.where(lane < N_EXPERTS, logits, -jnp.inf)
            m1 = jnp.max(logits, axis=1, keepdims=True)
            i1 = jnp.min(jnp.where(logits == m1, lane, LANE), axis=1, keepdims=True)
            rest = jnp.where(lane == i1, -jnp.inf, logits)
            m2 = jnp.max(rest, axis=1, keepdims=True)
            i2 = jnp.min(jnp.where(rest == m2, lane, LANE), axis=1, keepdims=True)
            e2 = jnp.exp(m2 - m1)
            den = 1.0 + e2
            gate_scr[...] = jnp.where(lane == i1, 1.0 / den, 0.0) + jnp.where(lane == i2, e2 / den, 0.0)

    hb = hb_scr[...]
    a = _silu(_dot(hb, w1_ref[0])) * _dot(hb, w3_ref[0])
    y = _dot(a.astype(BF16), w2_ref[0])
    if moe:
        lane = lax.broadcasted_iota(jnp.int32, gate_scr.shape, 1)
        gcol = jnp.sum(jnp.where(lane == e, gate_scr[...], 0.0), axis=1, keepdims=True)
        y = y * gcol
    acc_scr[...] += y

    @pl.when((e == n_e - 1) & (f == n_f - 1))
    def _():
        o_ref[0] = x_ref[0] + gt_ref[0] * acc_scr[...]


def _ffn(x, g, sh, sc, gt, w1, w3, w2, router=None):
    b, l, d = x.shape
    n_e, _, ff = w1.shape
    moe = router is not None
    tm = 1024 if l % 1024 == 0 else 256
    fc = ff // 2 if not moe else ff // 4
    n_f = ff // fc
    vec = pl.BlockSpec((1, 1, d), lambda i, j, e, f: (i, 0, 0))
    in_specs = [pl.BlockSpec((1, tm, d), lambda i, j, e, f: (i, j, 0)),
                pl.BlockSpec((1, d), lambda i, j, e, f: (0, 0)), vec, vec, vec]
    args = [x, g.reshape(1, d), sh, sc, gt]
    scratch = [pltpu.VMEM((tm, d), BF16), pltpu.VMEM((tm, d), F32)]
    if moe:
        in_specs.append(pl.BlockSpec((d, LANE), lambda i, j, e, f: (0, 0)))
        args.append(router)
        scratch.append(pltpu.VMEM((tm, LANE), F32))
    in_specs += [pl.BlockSpec((1, d, fc), lambda i, j, e, f: (e, 0, f)),
                 pl.BlockSpec((1, d, fc), lambda i, j, e, f: (e, 0, f)),
                 pl.BlockSpec((1, fc, d), lambda i, j, e, f: (e, f, 0))]
    args += [w1, w3, w2]
    return pl.pallas_call(
        functools.partial(_ffn_kernel, moe=moe, n_e=n_e, n_f=n_f),
        out_shape=jax.ShapeDtypeStruct((b, l, d), F32),
        grid=(b, l // tm, n_e, n_f),
        in_specs=in_specs,
        out_specs=pl.BlockSpec((1, tm, d), lambda i, j, e, f: (i, j, 0)),
        scratch_shapes=scratch,
        compiler_params=_params("arbitrary", "arbitrary", "arbitrary", "arbitrary"),
        name="moe_ffn" if moe else "dense_ffn",
    )(*args)


def _final_norm_kernel(x_ref, g_ref, o_ref):
    x = x_ref[0]
    ms = jnp.mean(x * x, axis=-1, keepdims=True)
    o_ref[0] = x * lax.rsqrt(ms + NORM_EPS) * g_ref[...]


def _final_norm(x, g):
    b, l, d = x.shape
    tm = 1024 if l % 1024 == 0 else 256
    return pl.pallas_call(
        _final_norm_kernel,
        out_shape=jax.ShapeDtypeStruct((b, l, d), F32),
        grid=(b, l // tm),
        in_specs=[pl.BlockSpec((1, tm, d), lambda i, j: (i, j, 0)),
                  pl.BlockSpec((1, d), lambda i, j: (0, 0))],
        out_specs=pl.BlockSpec((1, tm, d), lambda i, j: (i, j, 0)),
        compiler_params=_params("arbitrary", "arbitrary"),
        name="final_norm",
    )(x, g.reshape(1, d))


def _rope_tables(n):
    rows = n // GRID_W
    row = jnp.repeat(jnp.arange(rows, dtype=F32), GRID_W)
    col = jnp.tile(jnp.arange(GRID_W, dtype=F32), rows)

    def axial(dim):
        nf = dim // 4
        inv = ROPE_BASE ** (-jnp.arange(nf, dtype=F32) / nf)
        return jnp.concatenate([row[:, None] * inv, col[:, None] * inv], axis=-1)

    def line(dim):
        nf = dim // 2
        inv = ROPE_BASE ** (-jnp.arange(nf, dtype=F32) / nf)
        return jnp.arange(n, dtype=F32)[:, None] * inv

    def expand(ang, dim):
        c = jnp.cos(ang)
        s = jnp.sin(ang)
        c = jnp.tile(jnp.concatenate([c, c], axis=-1), (1, LANE // dim))
        s = jnp.tile(jnp.concatenate([-s, s], axis=-1), (1, LANE // dim))
        return c, s

    ca, sa = expand(axial(HEAD_DIM), HEAD_DIM)
    cr, sr = expand(line(HEAD_DIM), HEAD_DIM)
    cd, sd = expand(axial(DIFF_DIM), DIFF_DIM)
    return ca, sa, cr, sr, cd, sd


_ATTN_HEAD_ORDER = (0, 2, 1, 3)


def _attn_perm():
    return jnp.concatenate([jnp.arange(HEAD_DIM) + HEAD_DIM * h for h in _ATTN_HEAD_ORDER])


def _block_diag(w):
    nb, blk, _ = w.shape
    out = jnp.zeros((nb * blk, nb * blk), w.dtype)
    for i in range(nb):
        out = out.at[i * blk:(i + 1) * blk, i * blk:(i + 1) * blk].set(w[i])
    return out


def kernel(x, c, ctx, c_ctx, w_ada, b_ada, g_norm1, g_norm2, w_in, w_out, attn_sink, ret_decay_logit,
           ret_gn, diff_lambda, diff_gn, conv_w, conv_b, lru_wa, lru_ba, lru_wx, lru_bx, lru_lambda,
           ffn_w1, ffn_w3, ffn_w2, moe_router, moe_w1, moe_w3, moe_w2, final_norm):
    b, n, d = x.shape
    m = ctx.shape[1]
    tables = _rope_tables(n)
    perm = _attn_perm()

    cc = jnp.concatenate([c, c_ctx[None, :], jnp.zeros((8 - b - 1, d), F32)], axis=0)
    mods = _mod_vectors(cc, w_ada, b_ada)

    xc = ctx
    for layer in range(DEPTH):
        need_ctx = layer < DEPTH - 1
        lam_init = 0.8 - 0.6 * math.exp(-0.3 * layer)
        mod = mods[layer]
        sh1, sc1, gt1, sh2, sc2, gt2 = [mod[:b, None, i * d:(i + 1) * d] for i in range(6)]
        sh1c, sc1c, gt1c, sh2c, sc2c, gt2c = [
            jnp.broadcast_to(mod[b:b + 1, None, i * d:(i + 1) * d], (b, 1, d)) for i in range(6)]

        wi = w_in[layer]
        wi = jnp.concatenate([wi[:, :256][:, perm], wi[:, 256:]], axis=1).astype(BF16)
        wo = w_out[layer]
        wo = jnp.concatenate([wo[:256][perm], wo[256:]], axis=0).astype(BF16)
        sink_perm = attn_sink[layer][jnp.array(_ATTN_HEAD_ORDER)]

        (qa, ka, va, rq, rk, rv, rg, dqt, dk, dvt, lx, ly) = _in_projection(
            x, g_norm1[layer], sh1, sc1, wi, tables)
        (qac, kac, vac, rqc, rkc, rvc, rgc, dqtc, dkc, dvtc, lxc, lyc) = _in_projection(
            xc, g_norm1[layer], sh1c, sc1c, wi, None)

        oa = _swa_attention(sink_perm, qa, ka, va, kac, vac)
        dl = ret_decay_logit[layer]
        dl_lane = jnp.repeat(dl, HEAD_DIM, axis=1).reshape(2, 1, GROUP)
        dl_head = jnp.broadcast_to(dl[:, :, None, None], (2, N_HEADS, 8, RET_CHUNK))
        ryc, s_ctx = _retention(dl_lane, dl_head, rqc, rkc, rvc, jnp.zeros((b, 2, GROUP, GROUP), F32))
        ry, _ = _retention(dl_lane, dl_head, rq, rk, rv, s_ctx)
        gn_col = diff_gn[layer].reshape(N_HEADS, HEAD_DIM, 1)
        k_all = jnp.concatenate([dk, dkc], axis=1)
        vt_all = jnp.concatenate([dvt, dvtc], axis=1)
        dot = _diff_attention(diff_lambda[layer], gn_col, dqt, k_all, vt_all, lam_init)
        lru = []
        lru_c = []
        for direction in range(2):
            prm = (conv_w[layer], conv_b[layer],
                   _block_diag(lru_wa[layer, direction]).astype(BF16), lru_ba[layer, direction],
                   _block_diag(lru_wx[layer, direction]).astype(BF16), lru_bx[layer, direction],
                   lru_lambda[layer, direction])
            rev = direction == 1
            hc = _lru_scan(lxc, *prm, jnp.zeros((b, 1, GROUP), F32), rev)
            h0 = hc[:, 0:1, :] if rev else hc[:, m - 1:m, :]
            lru.append(_lru_scan(lx, *prm, h0, rev))
            lru_c.append(hc)

        x = _out_projection(x, gt1, oa, ry, rg, ret_gn[layer], dot, lru[0], lru[1], ly, wo)
        j = layer // 2
        if layer % 2 == 0:
            fw = (ffn_w1[j][None].astype(BF16), ffn_w3[j][None].astype(BF16), ffn_w2[j][None].astype(BF16))
            router = None
        else:
            fw = (moe_w1[j].astype(BF16), moe_w3[j].astype(BF16), moe_w2[j].astype(BF16))
            router = jnp.pad(moe_router[j], ((0, 0), (0, LANE - N_EXPERTS)))
        x = _ffn(x, g_norm2[layer], sh2, sc2, gt2, *fw, router=router)

        if need_ctx:
            oac = _ctx_attention(sink_perm, qac, kac, vac)
            dotc = _diff_attention(diff_lambda[layer], gn_col, dqtc, dkc, dvtc, lam_init)
            xc = _out_projection(xc, gt1c, oac, ryc, rgc, ret_gn[layer], dotc, lru_c[0], lru_c[1], lyc, wo)
            xc = _ffn(xc, g_norm2[layer], sh2c, sc2c, gt2c, *fw, router=router)
    return _final_norm(x, final_norm)
```

```python
import functools
import math

import jax
import jax.numpy as jnp
from jax import lax
from jax.experimental import pallas as pl
from jax.experimental.pallas import tpu as pltpu

F32 = jnp.float32
BF16 = jnp.bfloat16

D_MODEL = 1024
DEPTH = 2
GRID_W = 64
HEAD_DIM = 64
GROUP = D_MODEL // 4
N_HEADS = GROUP // HEAD_DIM
DIFF_DIM = HEAD_DIM // 2
WINDOW = 128
NORM_EPS = 1e-6
ROPE_BASE = 10000.0
LRU_BLOCKS = 4
LRU_C = 8.0
N_EXPERTS = 8
LANE = 128
LOG2E = 1.4426950408889634
DIFF_TQ = 256
DIFF_TK = 256
DIFF_VROWS = HEAD_DIM + 16

OFF_AQ, OFF_AK, OFF_AV = 0, 256, 384
OFF_RQ, OFF_RK, OFF_RV, OFF_RG = 512, 768, 1024, 1280
OFF_DQ, OFF_DK, OFF_DV = 1536, 1792, 2048
OFF_LX, OFF_LY = 2304, 2560
D_IN = 2816

VMEM_LIMIT = 56 * 1024 * 1024


def _params(*sem):
    return pltpu.CompilerParams(dimension_semantics=sem, vmem_limit_bytes=VMEM_LIMIT)


def _silu(x):
    return x * jax.nn.sigmoid(x)


def _softplus(x):
    return jnp.maximum(x, 0.0) + jnp.log(1.0 + jnp.exp(-jnp.abs(x)))


def _dot(a, b):
    return jnp.dot(a, b, preferred_element_type=F32)


def _dot_nt(a, b):
    return lax.dot_general(a, b, (((1,), (1,)), ((), ())), preferred_element_type=F32)


def _mod_kernel(c_ref, w_ref, b_ref, o_ref):
    s = _silu(c_ref[...])
    o_ref[0] = _dot(s.astype(BF16), w_ref[0].astype(BF16)) + b_ref[0]


def _mod_vectors(cc, w_ada, b_ada):
    depth, d, n6 = w_ada.shape
    tn = 1536
    return pl.pallas_call(
        _mod_kernel,
        out_shape=jax.ShapeDtypeStruct((depth, 8, n6), F32),
        grid=(depth, n6 // tn),
        in_specs=[pl.BlockSpec((8, d), lambda l, j: (0, 0)),
                  pl.BlockSpec((1, d, tn), lambda l, j: (l, 0, j)),
                  pl.BlockSpec((1, 1, tn), lambda l, j: (l, 0, j))],
        out_specs=pl.BlockSpec((1, 8, tn), lambda l, j: (l, 0, j)),
        compiler_params=_params("arbitrary", "arbitrary"),
        name="mod_vectors",
    )(cc, w_ada, b_ada.reshape(depth, 1, n6))


def _rope_tile(x, cos, sin_signed, d):
    lane = lax.broadcasted_iota(jnp.int32, x.shape, 1)
    first = (lane & (d - 1)) < (d // 2)
    partner = jnp.where(first, pltpu.roll(x, LANE - d // 2, axis=1), pltpu.roll(x, d // 2, axis=1))
    return x * cos + partner * sin_signed


def _inproj_kernel(*refs, rope, tm):
    if rope:
        (x_ref, g_ref, sh_ref, sc_ref, w_ref, ca_ref, sa_ref, cr_ref, sr_ref, cd_ref, sd_ref,
         qa_ref, ka_ref, va_ref, rq_ref, rk_ref, rv_ref, rg_ref,
         dqt_ref, dk_ref, dvt_ref, lx_ref, ly_ref) = refs
    else:
        (x_ref, g_ref, sh_ref, sc_ref, w_ref,
         qa_ref, ka_ref, va_ref, rq_ref, rk_ref, rv_ref, rg_ref,
         dqt_ref, dk_ref, dvt_ref, lx_ref, ly_ref) = refs
    x = x_ref[0]
    ms = jnp.mean(x * x, axis=-1, keepdims=True)
    y = x * lax.rsqrt(ms + NORM_EPS) * g_ref[...]
    h = y * (1.0 + sc_ref[0]) + sh_ref[0]
    hb = h.astype(BF16)

    def proj(off, width):
        return _dot(hb, w_ref[:, off:off + width])

    def roped(p, c_ref, s_ref, d, scale):
        tiles = []
        for t in range(p.shape[1] // LANE):
            pt = p[:, t * LANE:(t + 1) * LANE]
            if rope:
                pt = _rope_tile(pt, c_ref[...], s_ref[...], d)
            tiles.append(pt if scale == 1.0 else pt * scale)
        return tiles[0] if len(tiles) == 1 else jnp.concatenate(tiles, axis=1)

    ca = sa = cr = sr = cd = sd = None
    if rope:
        ca, sa, cr, sr, cd, sd = ca_ref, sa_ref, cr_ref, sr_ref, cd_ref, sd_ref
    qa_ref[0] = roped(proj(OFF_AQ, 256), ca, sa, HEAD_DIM, HEAD_DIM ** -0.5).astype(BF16)
    ka_ref[0] = roped(proj(OFF_AK, 128), ca, sa, HEAD_DIM, 1.0).astype(BF16)
    va_ref[0] = proj(OFF_AV, 128).astype(BF16)
    rq_ref[0] = roped(proj(OFF_RQ, 256), cr, sr, HEAD_DIM, 1.0).astype(BF16)
    rk_ref[0] = roped(proj(OFF_RK, 256), cr, sr, HEAD_DIM, HEAD_DIM ** -0.5).astype(BF16)
    rv_ref[0] = proj(OFF_RV, 256).astype(BF16)
    rg_ref[0] = proj(OFF_RG, 256)
    dq = roped(proj(OFF_DQ, 256), cd, sd, DIFF_DIM, DIFF_DIM ** -0.5 * LOG2E)
    dqt_ref[0] = dq.T.astype(BF16)
    dk_ref[0] = roped(proj(OFF_DK, 256), cd, sd, DIFF_DIM, 1.0).astype(BF16)
    dvt = proj(OFF_DV, 256).T.astype(BF16)
    ones_rows = (lax.broadcasted_iota(jnp.int32, (DIFF_VROWS - HEAD_DIM, DIFF_TK), 0) == 0).astype(BF16)
    for j in range(tm // DIFF_TK):
        for hd in range(N_HEADS):
            dvt_ref[0, j, hd, 0:HEAD_DIM, :] = dvt[hd * HEAD_DIM:(hd + 1) * HEAD_DIM, j * DIFF_TK:(j + 1) * DIFF_TK]
            dvt_ref[0, j, hd, HEAD_DIM:DIFF_VROWS, :] = ones_rows
    lx_ref[0] = proj(OFF_LX, 256)
    ly_ref[0] = proj(OFF_LY, 256)


def _in_projection(x, g, sh, sc, w_in, tables):
    b, l, d = x.shape
    rope = tables is not None
    tm = 512 if l % 512 == 0 else 256
    nt = l // tm
    row = lambda shape: pl.BlockSpec(shape, lambda i, j: (i, j, 0))
    vec = pl.BlockSpec((1, 1, d), lambda i, j: (i, 0, 0))
    in_specs = [row((1, tm, d)), pl.BlockSpec((1, d), lambda i, j: (0, 0)), vec, vec,
                pl.BlockSpec((d, D_IN), lambda i, j: (0, 0))]
    args = [x, g.reshape(1, d), sh, sc, w_in]
    if rope:
        in_specs += [pl.BlockSpec((tm, LANE), lambda i, j: (j, 0))] * 6
        args += list(tables)
    sds = jax.ShapeDtypeStruct
    out_shape = [sds((b, l, 256), BF16), sds((b, l, 128), BF16), sds((b, l, 128), BF16),
                 sds((b, l, 256), BF16), sds((b, l, 256), BF16), sds((b, l, 256), BF16),
                 sds((b, l, 256), F32),
                 sds((b, 256, l), BF16), sds((b, l, 256), BF16),
                 sds((b, l // DIFF_TK, N_HEADS, DIFF_VROWS, DIFF_TK), BF16),
                 sds((b, l, 256), F32), sds((b, l, 256), F32)]
    out_specs = [row((1, tm, 256)), row((1, tm, 128)), row((1, tm, 128)),
                 row((1, tm, 256)), row((1, tm, 256)), row((1, tm, 256)), row((1, tm, 256)),
                 pl.BlockSpec((1, 256, tm), lambda i, j: (i, 0, j)), row((1, tm, 256)),
                 pl.BlockSpec((1, tm // DIFF_TK, N_HEADS, DIFF_VROWS, DIFF_TK), lambda i, j: (i, j, 0, 0, 0)),
                 row((1, tm, 256)), row((1, tm, 256))]
    return pl.pallas_call(
        functools.partial(_inproj_kernel, rope=rope, tm=tm),
        out_shape=out_shape, grid=(b, nt), in_specs=in_specs, out_specs=out_specs,
        compiler_params=_params("arbitrary", "arbitrary"),
        name="in_projection_rope" if rope else "in_projection_ctx",
    )(*args)


def _swa_kernel(*refs, window, tq, n):
    if window:
        sink_ref, q_ref, k_ref, v_ref, kc_ref, vc_ref, o_ref = refs
    else:
        sink_ref, q_ref, kc_ref, vc_ref, o_ref = refs
    kc = kc_ref[0]
    vc = vc_ref[0]
    m = kc.shape[0]
    if window:
        i = pl.program_id(1)
        w = tq + 2 * WINDOW
        t0 = i * tq
        start = pl.multiple_of(jnp.clip(t0 - WINDOW, 0, n - w), WINDOW)
        kcat = jnp.concatenate([k_ref[0, pl.ds(start, w), :], kc], axis=0)
        vcat = jnp.concatenate([v_ref[0, pl.ds(start, w), :], vc], axis=0)
        qpos = t0 + lax.broadcasted_iota(jnp.int32, (tq, w + m), 0)
        col = lax.broadcasted_iota(jnp.int32, (tq, w + m), 1)
        valid = (jnp.abs(qpos - (start + col)) <= WINDOW) | (col >= w)
    else:
        kcat, vcat, valid = kc, vc, None
    q = q_ref[0]
    lane = lax.broadcasted_iota(jnp.int32, (1, LANE), 1)
    out_tiles = []
    for t in range(2):
        qt = q[:, t * LANE:(t + 1) * LANE]
        out_t = None
        for half in range(2):
            in_half = (lane >= half * HEAD_DIM) & (lane < (half + 1) * HEAD_DIM)
            sink = sink_ref[2 * t + half]
            qm = jnp.where(in_half, qt, jnp.zeros_like(qt))
            s = _dot_nt(qm, kcat)
            if valid is not None:
                s = jnp.where(valid, s, -jnp.inf)
            mx = jnp.maximum(jnp.max(s, axis=1, keepdims=True), sink)
            e = jnp.exp(s - mx)
            den = jnp.sum(e, axis=1, keepdims=True) + jnp.exp(sink - mx)
            o = _dot(e.astype(BF16), vcat) / den
            out_t = o if out_t is None else jnp.where(in_half, o, out_t)
        out_tiles.append(out_t)
    o_ref[0] = jnp.concatenate(out_tiles, axis=1)


def _swa_attention(sink_perm, q, k, v, kc, vc):
    b, n, _ = q.shape
    m = kc.shape[1]
    tq = 256
    smem = pl.BlockSpec(memory_space=pltpu.SMEM)
    ctx = pl.BlockSpec((1, m, 128), lambda i, j: (i, 0, 0))
    in_specs = [smem, pl.BlockSpec((1, tq, 256), lambda i, j: (i, j, 0)),
                pl.BlockSpec((1, n, 128), lambda i, j: (i, 0, 0)),
                pl.BlockSpec((1, n, 128), lambda i, j: (i, 0, 0)), ctx, ctx]
    return pl.pallas_call(
        functools.partial(_swa_kernel, window=True, tq=tq, n=n),
        out_shape=jax.ShapeDtypeStruct((b, n, 256), F32),
        grid=(b, n // tq), in_specs=in_specs,
        out_specs=pl.BlockSpec((1, tq, 256), lambda i, j: (i, j, 0)),
        compiler_params=_params("arbitrary", "arbitrary"),
        name="swa_attention",
    )(sink_perm, q, k, v, kc, vc)


def _ctx_attention(sink_perm, qc, kc, vc):
    b, m, _ = qc.shape
    smem = pl.BlockSpec(memory_space=pltpu.SMEM)
    ctx = pl.BlockSpec((1, m, 128), lambda i: (i, 0, 0))
    return pl.pallas_call(
        functools.partial(_swa_kernel, window=False, tq=m, n=m),
        out_shape=jax.ShapeDtypeStruct((b, m, 256), F32),
        grid=(b,), in_specs=[smem, pl.BlockSpec((1, m, 256), lambda i: (i, 0, 0)), ctx, ctx],
        out_specs=pl.BlockSpec((1, m, 256), lambda i: (i, 0, 0)),
        compiler_params=_params("arbitrary"),
        name="ctx_attention",
    )(sink_perm, qc, kc, vc)


RET_CHUNK = 128


def _ret_kernel(dl_lane_ref, dl_head_ref, q_ref, k_ref, v_ref, s0_ref, y_ref, sfin_ref, s_scr, *, nc):
    c = RET_CHUNK
    d = pl.program_id(1)
    ci = pl.program_id(2)
    fwd = d == 0

    @pl.when(ci == 0)
    def _():
        s_scr[...] = s0_ref[0, 0]

    q = q_ref[0]
    k = k_ref[0]
    v = v_ref[0]
    lgl = -_softplus(-dl_lane_ref[0])
    row = lax.broadcasted_iota(jnp.int32, (c, 1), 0)
    xe = jnp.where(fwd, row + 1, c - row).astype(F32)
    ze = jnp.where(fwd, c - 1 - row, row).astype(F32)
    xi = jnp.exp(lgl * xe)
    zeta = jnp.exp(lgl * ze)
    cross = _dot(q, s_scr[...].astype(BF16)) * xi
    ni = lax.broadcasted_iota(jnp.int32, (c, c), 0)
    mi = lax.broadcasted_iota(jnp.int32, (c, c), 1)
    dd = jnp.where(fwd, ni - mi, mi - ni)
    ddf = jnp.maximum(dd, 0).astype(F32)
    lane_head = lax.broadcasted_iota(jnp.int32, (1, GROUP), 1) // HEAD_DIM
    inner = jnp.zeros((c, GROUP), F32)
    for h in range(N_HEADS):
        lgh = -_softplus(-dl_head_ref[0, h][0:1, :])
        dm = jnp.where(dd >= 0, jnp.exp(lgh * ddf), 0.0)
        qh = jnp.where(lane_head == h, q, jnp.zeros_like(q))
        s = _dot_nt(qh, k) * dm
        ih = _dot(s.astype(BF16), v)
        inner = jnp.where(lane_head == h, ih, inner)
    y_ref[0, 0] = inner + cross
    kz = (k.astype(F32) * zeta).T.astype(BF16)
    u = _dot(kz, v)
    dec = jnp.exp(lgl * float(c))
    rh = lax.broadcasted_iota(jnp.int32, (GROUP, GROUP), 0) // HEAD_DIM
    ch = lax.broadcasted_iota(jnp.int32, (GROUP, GROUP), 1) // HEAD_DIM
    s_scr[...] = jnp.where(rh == ch, dec * s_scr[...] + u, 0.0)

    @pl.when(ci == nc - 1)
    def _():
        sfin_ref[0, 0] = s_scr[...]


def _retention(dl_lane, dl_head, q, k, v, s0):
    b, l, _ = q.shape
    c = RET_CHUNK
    nc = l // c
    cidx = lambda i, d, j: (i, jnp.where(d == 0, j, nc - 1 - j), 0)
    tok = pl.BlockSpec((1, c, 256), cidx)
    st = pl.BlockSpec((1, 1, 256, 256), lambda i, d, j: (i, d, 0, 0))
    return pl.pallas_call(
        functools.partial(_ret_kernel, nc=nc),
        out_shape=[jax.ShapeDtypeStruct((b, 2, l, 256), F32),
                   jax.ShapeDtypeStruct((b, 2, 256, 256), F32)],
        grid=(b, 2, nc),
        in_specs=[pl.BlockSpec((1, 1, 256), lambda i, d, j: (d, 0, 0)),
                  pl.BlockSpec((1, N_HEADS, 8, c), lambda i, d, j: (d, 0, 0, 0)),
                  tok, tok, tok, st],
        out_specs=[pl.BlockSpec((1, 1, c, 256), lambda i, d, j: (i, d, jnp.where(d == 0, j, nc - 1 - j), 0)),
                   st],
        scratch_shapes=[pltpu.VMEM((GROUP, GROUP), F32)],
        compiler_params=_params("arbitrary", "arbitrary", "arbitrary"),
        name="retention",
    )(dl_lane, dl_head, q, k, v, s0)


def _inner_tiles(tiles):
    for g in (8, 7, 6, 5, 4, 3, 2):
        if tiles % g == 0:
            return g
    return 1


def _diff_kernel(lam_ref, gn_ref, qt_ref, k_ref, vt_ref, o_ref, s_scr, *, tiles, lam_init):
    h = pl.program_id(1)
    inner = _inner_tiles(tiles)
    outer = tiles // inner
    tq = qt_ref.shape[2]
    lv = lam_ref[...]
    lam = (jnp.exp(jnp.sum(lv[0:1] * lv[1:2], axis=1, keepdims=True))
           - jnp.exp(jnp.sum(lv[2:3] * lv[3:4], axis=1, keepdims=True)) + lam_init)
    qt = qt_ref[0]
    rowi = lax.broadcasted_iota(jnp.int32, (LANE, 1), 0)
    hoff = (h % 2) * HEAD_DIM
    qpads = []
    for mp in range(2):
        lo = hoff + mp * DIFF_DIM
        qpads.append(jnp.where((rowi >= lo) & (rowi < lo + DIFF_DIM), qt, jnp.zeros_like(qt)))

    def step(q_mp, q_chunk, e_mp, e_chunk, m_e):
        mx = pv = None
        for t in range(inner):
            sl = slice(t * DIFF_TK, (t + 1) * DIFF_TK)
            if q_mp is not None:
                base = pl.multiple_of((q_chunk * inner + t) * DIFF_TK, DIFF_TK)
                s = _dot(k_ref[0, pl.ds(base, DIFF_TK), :], qpads[q_mp])
                s_scr[q_mp, sl, :] = s
                cm = jnp.max(s, axis=0, keepdims=True)
                mx = cm if mx is None else jnp.maximum(mx, cm)
            if e_mp is not None:
                p = jnp.exp2(s_scr[e_mp, sl, :] - m_e).astype(BF16)
                d = _dot(vt_ref[0, e_chunk * inner + t, 0], p)
                pv = d if pv is None else pv + d
        return mx, pv

    neg_inf = jnp.full((1, tq), -jnp.inf, F32)
    zeros = jnp.zeros((DIFF_VROWS, tq), F32)
    m0_new, _ = step(0, 0, None, None, None)
    alpha0 = jnp.zeros((1, tq), F32)

    def body(c, carry):
        m0_new, alpha0, acc0, m1_run, acc1 = carry
        mx1, pv0 = step(1, c, 0, c, m0_new)
        acc0 = alpha0 * acc0 + pv0
        m1_new = jnp.maximum(m1_run, mx1)
        alpha1 = jnp.exp2(m1_run - m1_new)
        mx0, pv1 = step(0, jnp.minimum(c + 1, outer - 1), 1, c, m1_new)
        acc1 = alpha1 * acc1 + pv1
        m0_next = jnp.maximum(m0_new, mx0)
        return m0_next, jnp.exp2(m0_new - m0_next), acc0, m1_new, acc1

    _, _, acc0, _, acc1 = lax.fori_loop(0, outer, body, (m0_new, alpha0, zeros, neg_inf, zeros),
                                        unroll=min(2, outer))
    maps = [acc[0:HEAD_DIM] / acc[HEAD_DIM:HEAD_DIM + 1] for acc in (acc0, acc1)]
    o = maps[0] - lam * maps[1]
    ms = jnp.mean(o * o, axis=0, keepdims=True)
    o_ref[0] = o * lax.rsqrt(ms + NORM_EPS) * (gn_ref[0] * (1.0 - lam_init))


def _diff_attention(lam_vecs, gn_col, qt, k_all, vt_all, lam_init):
    b, _, lq = qt.shape
    lk = k_all.shape[1]
    tiles = lk // DIFF_TK
    tq = DIFF_TQ
    return pl.pallas_call(
        functools.partial(_diff_kernel, tiles=tiles, lam_init=lam_init),
        out_shape=jax.ShapeDtypeStruct((b, 256, lq), F32),
        grid=(b, N_HEADS, lq // tq),
        in_specs=[pl.BlockSpec((4, DIFF_DIM), lambda i, h, j: (0, 0)),
                  pl.BlockSpec((1, HEAD_DIM, 1), lambda i, h, j: (h, 0, 0)),
                  pl.BlockSpec((1, LANE, tq), lambda i, h, j: (i, h // 2, j)),
                  pl.BlockSpec((1, lk, LANE), lambda i, h, j: (i, 0, h // 2)),
                  pl.BlockSpec((1, tiles, 1, DIFF_VROWS, DIFF_TK), lambda i, h, j: (i, 0, h, 0, 0))],
        out_specs=pl.BlockSpec((1, HEAD_DIM, tq), lambda i, h, j: (i, h, j)),
        scratch_shapes=[pltpu.VMEM((2, _inner_tiles(tiles) * DIFF_TK, tq), F32)],
        compiler_params=_params("arbitrary", "arbitrary", "arbitrary"),
        name="diff_attention",
    )(lam_vecs, gn_col, qt, k_all, vt_all)


def _lru_kernel(x_ref, xp_ref, xn_ref, cw_ref, cb_ref, wa_ref, ba_ref, wx_ref, bx_ref, lam_ref,
                h0_ref, h_ref, carry, *, reverse, tm, nt):
    j = pl.program_id(1)
    tix = (nt - 1 - j) if reverse else j

    @pl.when(j == 0)
    def _():
        carry[...] = jnp.broadcast_to(h0_ref[0], carry.shape)

    x = x_ref[0]
    row = lax.broadcasted_iota(jnp.int32, (tm, 1), 0)
    prev = jnp.where(tix > 0, xp_ref[0][7:8, :], 0.0)
    nxt0 = jnp.where(tix < nt - 1, xn_ref[0][0:1, :], 0.0)
    nxt1 = jnp.where(tix < nt - 1, xn_ref[0][1:2, :], 0.0)
    xm1 = jnp.where(row == 0, prev, pltpu.roll(x, 1, axis=0))
    xp1 = jnp.where(row == tm - 1, nxt0, pltpu.roll(x, tm - 1, axis=0))
    xp2 = jnp.where(row == tm - 1, nxt1, jnp.where(row == tm - 2, nxt0, pltpu.roll(x, tm - 2, axis=0)))
    cw = cw_ref[...]
    u = cw[0:1] * xm1 + cw[1:2] * x + cw[2:3] * xp1 + cw[3:4] * xp2 + cb_ref[...]
    ub = u.astype(BF16)
    r = jax.nn.sigmoid(_dot(ub, wa_ref[0]) + ba_ref[0])
    gi = jax.nn.sigmoid(_dot(ub, wx_ref[0]) + bx_ref[0])
    log_a = -LRU_C * r * _softplus(-lam_ref[0])
    a = jnp.exp(log_a)
    th = jnp.tanh(log_a)
    bv = jnp.sqrt(-2.0 * th / (1.0 - th)) * (gi * u)
    k = 1
    while k < tm:
        if reverse:
            keep = row < tm - k
            shift = tm - k
        else:
            keep = row >= k
            shift = k
        a_sh = jnp.where(keep, pltpu.roll(a, shift, axis=0), 1.0)
        b_sh = jnp.where(keep, pltpu.roll(bv, shift, axis=0), 0.0)
        bv = a * b_sh + bv
        a = a * a_sh
        k *= 2
    hcur = a * carry[0:1, :] + bv
    h_ref[0] = hcur
    last = hcur[0:1, :] if reverse else hcur[tm - 1:tm, :]
    carry[...] = jnp.broadcast_to(last, carry.shape)


def _lru_scan(lx, conv_w, conv_b, wa_bd, ba, wx_bd, bx, lam, h0, reverse):
    b, l, w = lx.shape
    tm = 512 if l % 512 == 0 else 256
    nt = l // tm
    tb = tm // 8
    tidx = (lambda j: nt - 1 - j) if reverse else (lambda j: j)
    vec = pl.BlockSpec((1, w), lambda i, j: (0, 0))
    mat = pl.BlockSpec((1, w, w), lambda i, j: (0, 0, 0))
    return pl.pallas_call(
        functools.partial(_lru_kernel, reverse=reverse, tm=tm, nt=nt),
        out_shape=jax.ShapeDtypeStruct((b, l, w), F32),
        grid=(b, nt),
        in_specs=[pl.BlockSpec((1, tm, w), lambda i, j: (i, tidx(j), 0)),
                  pl.BlockSpec((1, 8, w), lambda i, j: (i, jnp.maximum(tidx(j) * tb - 1, 0), 0)),
                  pl.BlockSpec((1, 8, w), lambda i, j: (i, jnp.minimum((tidx(j) + 1) * tb, l // 8 - 1), 0)),
                  pl.BlockSpec((4, w), lambda i, j: (0, 0)), vec,
                  mat, vec, mat, vec, vec,
                  pl.BlockSpec((1, 1, w), lambda i, j: (i, 0, 0))],
        out_specs=pl.BlockSpec((1, tm, w), lambda i, j: (i, tidx(j), 0)),
        scratch_shapes=[pltpu.VMEM((8, w), F32)],
        compiler_params=_params("arbitrary", "arbitrary"),
        name="lru_scan_rev" if reverse else "lru_scan_fwd",
    )(lx, lx, lx, conv_w, conv_b.reshape(1, w), wa_bd.reshape(1, w, w), ba.reshape(1, w),
      wx_bd.reshape(1, w, w), bx.reshape(1, w), lam.reshape(1, w), h0)


def _outproj_kernel(x_ref, gt_ref, oa_ref, ry_ref, rg_ref, rgn_ref, dot_ref, hf_ref, hb_ref, ly_ref,
                    bd_ref, w_ref, o_ref):
    oa = oa_ref[0]
    y = ry_ref[0, 0] + ry_ref[0, 1]
    y2 = y * y
    hi = y2.astype(BF16)
    lo = (y2 - hi.astype(F32)).astype(BF16)
    ms = _dot(hi, bd_ref[...]) + _dot(lo, bd_ref[...])
    ob = y * lax.rsqrt(ms + NORM_EPS) * rgn_ref[...] * _silu(rg_ref[0])
    od = dot_ref[0].T
    ol = (hf_ref[0] + hb_ref[0]) * jax.nn.gelu(ly_ref[0])
    acc = _dot(oa.astype(BF16), w_ref[0:256, :])
    acc = acc + _dot(ob.astype(BF16), w_ref[256:512, :])
    acc = acc + _dot(od.astype(BF16), w_ref[512:768, :])
    acc = acc + _dot(ol.astype(BF16), w_ref[768:1024, :])
    o_ref[0] = x_ref[0] + gt_ref[0] * acc


def _out_projection(x, gt, oa, ry, rg, ret_gn, dot, hf, hb, ly, w_out):
    b, l, d = x.shape
    tm = 512 if l % 512 == 0 else 256
    row = lambda wdt: pl.BlockSpec((1, tm, wdt), lambda i, j: (i, j, 0))
    hd = jnp.arange(GROUP) // HEAD_DIM
    bd = jnp.where(hd[:, None] == hd[None, :], 1.0 / HEAD_DIM, 0.0).astype(BF16)
    return pl.pallas_call(
        _outproj_kernel,
        out_shape=jax.ShapeDtypeStruct((b, l, d), F32),
        grid=(b, l // tm),
        in_specs=[row(d), pl.BlockSpec((1, 1, d), lambda i, j: (i, 0, 0)), row(256),
                  pl.BlockSpec((1, 2, tm, 256), lambda i, j: (i, 0, j, 0)), row(256),
                  pl.BlockSpec((1, 256), lambda i, j: (0, 0)),
                  pl.BlockSpec((1, 256, tm), lambda i, j: (i, 0, j)),
                  row(256), row(256), row(256),
                  pl.BlockSpec((256, 256), lambda i, j: (0, 0)),
                  pl.BlockSpec((d, d), lambda i, j: (0, 0))],
        out_specs=row(d),
        compiler_params=_params("arbitrary", "arbitrary"),
        name="out_projection",
    )(x, gt, oa, ry, rg, ret_gn.reshape(1, 256), dot, hf, hb, ly, bd, w_out)


def _norm_modulate(x_ref, g_ref, sh_ref, sc_ref):
    x = x_ref[0]
    ms = jnp.mean(x * x, axis=-1, keepdims=True)
    y = x * lax.rsqrt(ms + NORM_EPS) * g_ref[...]
    return y * (1.0 + sc_ref[0]) + sh_ref[0]


def _ffn_kernel(x_ref, g_ref, sh_ref, sc_ref, gt_ref, w1_ref, w3_ref, w2_ref, o_ref, hb_scr, acc_scr,
                *, n_f):
    f = pl.program_id(2)

    @pl.when(f == 0)
    def _():
        hb_scr[...] = _norm_modulate(x_ref, g_ref, sh_ref, sc_ref).astype(BF16)
        acc_scr[...] = jnp.zeros_like(acc_scr)

    hb = hb_scr[...]
    a = _silu(_dot(hb, w1_ref[...])) * _dot(hb, w3_ref[...])
    acc_scr[...] += _dot(a.astype(BF16), w2_ref[...])

    @pl.when(f == n_f - 1)
    def _():
        o_ref[0] = x_ref[0] + gt_ref[0] * acc_scr[...]


def _ffn(x, g, sh, sc, gt, w1, w3, w2):
    b, l, d = x.shape
    ff = w1.shape[1]
    tm = 1024 if l % 1024 == 0 else 256
    n_f = 2
    fc = ff // n_f
    vec = pl.BlockSpec((1, 1, d), lambda i, j, f: (i, 0, 0))
    return pl.pallas_call(
        functools.partial(_ffn_kernel, n_f=n_f),
        out_shape=jax.ShapeDtypeStruct((b, l, d), F32),
        grid=(b, l // tm, n_f),
        in_specs=[pl.BlockSpec((1, tm, d), lambda i, j, f: (i, j, 0)),
                  pl.BlockSpec((1, d), lambda i, j, f: (0, 0)), vec, vec, vec,
                  pl.BlockSpec((d, fc), lambda i, j, f: (0, f)),
                  pl.BlockSpec((d, fc), lambda i, j, f: (0, f)),
                  pl.BlockSpec((fc, d), lambda i, j, f: (f, 0))],
        out_specs=pl.BlockSpec((1, tm, d), lambda i, j, f: (i, j, 0)),
        scratch_shapes=[pltpu.VMEM((tm, d), BF16), pltpu.VMEM((tm, d), F32)],
        compiler_params=_params("arbitrary", "arbitrary", "arbitrary"),
        name="dense_ffn",
    )(x, g.reshape(1, d), sh, sc, gt, w1, w3, w2)


MOE_RT = 128


def _moe_kernel(x_ref, g_ref, sh_ref, sc_ref, gt_ref, rt_ref, w1_ref, w3_ref, w2_ref, o_ref,
                hb_scr, acc_scr, gate_scr, pos_scr, gatet_scr, post_scr, x_scr, y_scr, gs_scr, cnt_smem,
                *, n_e, n_f, tm):
    e = pl.program_id(2)
    f = pl.program_id(3)
    rt2 = 2 * MOE_RT

    @pl.when((e == 0) & (f == 0))
    def _():
        h = _norm_modulate(x_ref, g_ref, sh_ref, sc_ref)
        hb = h.astype(BF16)
        hb_scr[...] = hb
        acc_scr[...] = jnp.zeros_like(acc_scr)
        rt = rt_ref[...]
        rhi = rt.astype(BF16)
        rlo = (rt - rhi.astype(F32)).astype(BF16)
        hlo = (h - hb.astype(F32)).astype(BF16)
        logits = _dot(hb, rhi) + _dot(hb, rlo) + _dot(hlo, rhi)
        lane = lax.broadcasted_iota(jnp.int32, logits.shape, 1)
        logits = jnp.where(lane < N_EXPERTS, logits, -jnp.inf)
        m1 = jnp.max(logits, axis=1, keepdims=True)
        i1 = jnp.min(jnp.where(logits == m1, lane, LANE), axis=1, keepdims=True)
        rest = jnp.where(lane == i1, -jnp.inf, logits)
        m2 = jnp.max(rest, axis=1, keepdims=True)
        i2 = jnp.min(jnp.where(rest == m2, lane, LANE), axis=1, keepdims=True)
        e2 = jnp.exp(m2 - m1)
        den = 1.0 + e2
        gate = jnp.where(lane == i1, 1.0 / den, 0.0) + jnp.where(lane == i2, e2 / den, 0.0)
        gate_scr[...] = gate
        sel = jnp.where(gate > 0.0, 1.0, 0.0)
        tok_r = lax.broadcasted_iota(jnp.int32, (tm, tm), 0)
        tok_c = lax.broadcasted_iota(jnp.int32, (tm, tm), 1)
        earlier = jnp.where(tok_c < tok_r, 1.0, 0.0).astype(BF16)
        pos = _dot(earlier, sel.astype(BF16))
        pos_scr[...] = pos
        post_scr[...] = pos.T
        gatet_scr[...] = gate.T
        for ee in range(n_e):
            cnt_smem[ee] = jnp.sum(jnp.where(lane == ee, sel, 0.0)).astype(jnp.int32)

    n_tiles = (cnt_smem[e] + (MOE_RT - 1)) // MOE_RT
    n_pairs = (n_tiles + 1) // 2

    @pl.when(f == 0)
    def _():
        sub = lax.broadcasted_iota(jnp.int32, (8, 1), 0)
        pos_e = jnp.sum(jnp.where(sub == e, post_scr[0:8, :], 0.0), axis=0, keepdims=True)
        gate_e = jnp.sum(jnp.where(sub == e, gatet_scr[0:8, :], 0.0), axis=0, keepdims=True)

        def gather(j, carry):
            r0 = pl.multiple_of(j * MOE_RT, MOE_RT)
            slot = (r0 + lax.broadcasted_iota(jnp.int32, (MOE_RT, 1), 0)).astype(F32)
            hit = (pos_e == slot) & (gate_e > 0.0)
            onehot = jnp.where(hit, 1.0, 0.0).astype(BF16)
            x_scr[pl.ds(r0, MOE_RT), :] = _dot(onehot, hb_scr[...]).astype(BF16)
            gs = jnp.sum(jnp.where(hit, gate_e, 0.0), axis=1, keepdims=True)
            gs_scr[pl.ds(r0, MOE_RT), :] = jnp.broadcast_to(gs, (MOE_RT, LANE))
            y_scr[pl.ds(r0, MOE_RT), :] = jnp.zeros((MOE_RT, y_scr.shape[1]), F32)
            return carry

        lax.fori_loop(0, 2 * n_pairs, gather, 0)

    def expert(j, carry):
        r0 = pl.multiple_of(j * MOE_RT, MOE_RT)
        xj = x_scr[pl.ds(r0, MOE_RT), :]
        a = _silu(_dot(xj, w1_ref[0])) * _dot(xj, w3_ref[0])
        y_scr[pl.ds(r0, MOE_RT), :] += _dot(a.astype(BF16), w2_ref[0])
        return carry

    lax.fori_loop(0, n_tiles, expert, 0)

    @pl.when(f == n_f - 1)
    def _():
        lane = lax.broadcasted_iota(jnp.int32, (1, LANE), 1)
        pos_col = jnp.sum(jnp.where(lane == e, pos_scr[...], 0.0), axis=1, keepdims=True)
        gate_col = jnp.sum(jnp.where(lane == e, gate_scr[...], 0.0), axis=1, keepdims=True)

        def scatter(jj, carry):
            r0 = pl.multiple_of(jj * rt2, rt2)
            slot = (r0 + lax.broadcasted_iota(jnp.int32, (1, rt2), 1)).astype(F32)
            onehot = jnp.where((pos_col == slot) & (gate_col > 0.0), 1.0, 0.0).astype(BF16)
            ys = (y_scr[pl.ds(r0, rt2), :] * gs_scr[pl.ds(r0, rt2), 0:1]).astype(BF16)
            acc_scr[...] += _dot(onehot, ys)
            return carry

        lax.fori_loop(0, n_pairs, scatter, 0)

    @pl.when((e == n_e - 1) & (f == n_f - 1))
    def _():
        o_ref[0] = x_ref[0] + gt_ref[0] * acc_scr[...]


def _moe_ffn(x, g, sh, sc, gt, w1, w3, w2, router):
    b, l, d = x.shape
    n_e, _, ff = w1.shape
    tm = 1024 if l % 1024 == 0 else 256
    n_f = 4
    fc = ff // n_f
    vec = pl.BlockSpec((1, 1, d), lambda i, j, e, f: (i, 0, 0))
    return pl.pallas_call(
        functools.partial(_moe_kernel, n_e=n_e, n_f=n_f, tm=tm),
        out_shape=jax.ShapeDtypeStruct((b, l, d), F32),
        grid=(b, l // tm, n_e, n_f),
        in_specs=[pl.BlockSpec((1, tm, d), lambda i, j, e, f: (i, j, 0)),
                  pl.BlockSpec((1, d), lambda i, j, e, f: (0, 0)), vec, vec, vec,
                  pl.BlockSpec((d, LANE), lambda i, j, e, f: (0, 0)),
                  pl.BlockSpec((1, d, fc), lambda i, j, e, f: (e, 0, f)),
                  pl.BlockSpec((1, d, fc), lambda i, j, e, f: (e, 0, f)),
                  pl.BlockSpec((1, fc, d), lambda i, j, e, f: (e, f, 0))],
        out_specs=pl.BlockSpec((1, tm, d), lambda i, j, e, f: (i, j, 0)),
        scratch_shapes=[pltpu.VMEM((tm, d), BF16),
                        pltpu.VMEM((tm, d), F32),
                        pltpu.VMEM((tm, LANE), F32),
                        pltpu.VMEM((tm, LANE), F32),
                        pltpu.VMEM((LANE, tm), F32),
                        pltpu.VMEM((LANE, tm), F32),
                        pltpu.VMEM((tm, d), BF16),
                        pltpu.VMEM((tm, d), F32),
                        pltpu.VMEM((tm, LANE), F32),
                        pltpu.SMEM((n_e,), jnp.int32)],
        compiler_params=_params("arbitrary", "arbitrary", "arbitrary", "arbitrary"),
        name="moe_ffn",
    )(x, g.reshape(1, d), sh, sc, gt, router, w1, w3, w2)


def _final_norm_kernel(x_ref, g_ref, o_ref):
    x = x_ref[0]
    ms = jnp.mean(x * x, axis=-1, keepdims=True)
    o_ref[0] = x * lax.rsqrt(ms + NORM_EPS) * g_ref[...]


def _final_norm(x, g):
    b, l, d = x.shape
    tm = 1024 if l % 1024 == 0 else 256
    return pl.pallas_call(
        _final_norm_kernel,
        out_shape=jax.ShapeDtypeStruct((b, l, d), F32),
        grid=(b, l // tm),
        in_specs=[pl.BlockSpec((1, tm, d), lambda i, j: (i, j, 0)),
                  pl.BlockSpec((1, d), lambda i, j: (0, 0))],
        out_specs=pl.BlockSpec((1, tm, d), lambda i, j: (i, j, 0)),
        compiler_params=_params("arbitrary", "arbitrary"),
        name="final_norm",
    )(x, g.reshape(1, d))


def _rope_tables(n):
    rows = n // GRID_W
    row = jnp.repeat(jnp.arange(rows, dtype=F32), GRID_W)
    col = jnp.tile(jnp.arange(GRID_W, dtype=F32), rows)

    def axial(dim):
        nf = dim // 4
        inv = ROPE_BASE ** (-jnp.arange(nf, dtype=F32) / nf)
        return jnp.concatenate([row[:, None] * inv, col[:, None] * inv], axis=-1)

    def line(dim):
        nf = dim // 2
        inv = ROPE_BASE ** (-jnp.arange(nf, dtype=F32) / nf)
        return jnp.arange(n, dtype=F32)[:, None] * inv

    def expand(ang, dim):
        c = jnp.cos(ang)
        s = jnp.sin(ang)
        c = jnp.tile(jnp.concatenate([c, c], axis=-1), (1, LANE // dim))
        s = jnp.tile(jnp.concatenate([-s, s], axis=-1), (1, LANE // dim))
        return c, s

    ca, sa = expand(axial(HEAD_DIM), HEAD_DIM)
    cr, sr = expand(line(HEAD_DIM), HEAD_DIM)
    cd, sd = expand(axial(DIFF_DIM), DIFF_DIM)
    return ca, sa, cr, sr, cd, sd


_ATTN_HEAD_ORDER = (0, 2, 1, 3)


def _attn_perm():
    return jnp.concatenate([jnp.arange(HEAD_DIM) + HEAD_DIM * h for h in _ATTN_HEAD_ORDER])


def _block_diag(w):
    nb, blk, _ = w.shape
    out = jnp.zeros((nb * blk, nb * blk), w.dtype)
    for i in range(nb):
        out = out.at[i * blk:(i + 1) * blk, i * blk:(i + 1) * blk].set(w[i])
    return out


def kernel(x, c, ctx, c_ctx, w_ada, b_ada, g_norm1, g_norm2, w_in, w_out, attn_sink, ret_decay_logit,
           ret_gn, diff_lambda, diff_gn, conv_w, conv_b, lru_wa, lru_ba, lru_wx, lru_bx, lru_lambda,
           ffn_w1, ffn_w3, ffn_w2, moe_router, moe_w1, moe_w3, moe_w2, final_norm):
    b, n, d = x.shape
    m = ctx.shape[1]
    tables = _rope_tables(n)
    perm = _attn_perm()

    cc = jnp.concatenate([c, c_ctx[None, :], jnp.zeros((8 - b - 1, d), F32)], axis=0)
    mods = _mod_vectors(cc, w_ada, b_ada)

    xc = ctx
    for layer in range(DEPTH):
        need_ctx = layer < DEPTH - 1
        lam_init = 0.8 - 0.6 * math.exp(-0.3 * layer)
        mod = mods[layer]
        sh1, sc1, gt1, sh2, sc2, gt2 = [mod[:b, None, i * d:(i + 1) * d] for i in range(6)]
        sh1c, sc1c, gt1c, sh2c, sc2c, gt2c = [
            jnp.broadcast_to(mod[b:b + 1, None, i * d:(i + 1) * d], (b, 1, d)) for i in range(6)]

        wi = w_in[layer]
        wi = jnp.concatenate([wi[:, :256][:, perm], wi[:, 256:]], axis=1).astype(BF16)
        wo = w_out[layer]
        wo = jnp.concatenate([wo[:256][perm], wo[256:]], axis=0).astype(BF16)
        sink_perm = attn_sink[layer][jnp.array(_ATTN_HEAD_ORDER)]

        (qa, ka, va, rq, rk, rv, rg, dqt, dk, dvt, lx, ly) = _in_projection(
            x, g_norm1[layer], sh1, sc1, wi, tables)
        (qac, kac, vac, rqc, rkc, rvc, rgc, dqtc, dkc, dvtc, lxc, lyc) = _in_projection(
            xc, g_norm1[layer], sh1c, sc1c, wi, None)

        oa = _swa_attention(sink_perm, qa, ka, va, kac, vac)
        dl = ret_decay_logit[layer]
        dl_lane = jnp.repeat(dl, HEAD_DIM, axis=1).reshape(2, 1, GROUP)
        dl_head = jnp.broadcast_to(dl[:, :, None, None], (2, N_HEADS, 8, RET_CHUNK))
        ryc, s_ctx = _retention(dl_lane, dl_head, rqc, rkc, rvc, jnp.zeros((b, 2, GROUP, GROUP), F32))
        ry, _ = _retention(dl_lane, dl_head, rq, rk, rv, s_ctx)
        gn_col = diff_gn[layer].reshape(N_HEADS, HEAD_DIM, 1)
        k_all = jnp.concatenate([dk, dkc], axis=1)
        vt_all = jnp.concatenate([dvt, dvtc], axis=1)
        dot = _diff_attention(diff_lambda[layer], gn_col, dqt, k_all, vt_all, lam_init)
        lru = []
        lru_c = []
        for direction in range(2):
            prm = (conv_w[layer], conv_b[layer],
                   _block_diag(lru_wa[layer, direction]).astype(BF16), lru_ba[layer, direction],
                   _block_diag(lru_wx[layer, direction]).astype(BF16), lru_bx[layer, direction],
                   lru_lambda[layer, direction])
            rev = direction == 1
            hc = _lru_scan(lxc, *prm, jnp.zeros((b, 1, GROUP), F32), rev)
            h0 = hc[:, 0:1, :] if rev else hc[:, m - 1:m, :]
            lru.append(_lru_scan(lx, *prm, h0, rev))
            lru_c.append(hc)

        x = _out_projection(x, gt1, oa, ry, rg, ret_gn[layer], dot, lru[0], lru[1], ly, wo)
        j = layer // 2
        if layer % 2 == 0:
            ffn = functools.partial(_ffn, w1=ffn_w1[j].astype(BF16), w3=ffn_w3[j].astype(BF16),
                                    w2=ffn_w2[j].astype(BF16))
        else:
            ffn = functools.partial(_moe_ffn, w1=moe_w1[j].astype(BF16), w3=moe_w3[j].astype(BF16),
                                    w2=moe_w2[j].astype(BF16),
                                    router=jnp.pad(moe_router[j], ((0, 0), (0, LANE - N_EXPERTS))))
        x = ffn(x, g_norm2[layer], sh2, sc2, gt2)

        if need_ctx:
            oac = _ctx_attention(sink_perm, qac, kac, vac)
            dotc = _diff_attention(diff_lambda[layer], gn_col, dqtc, dkc, dvtc, lam_init)
            xc = _out_projection(xc, gt1c, oac, ryc, rgc, ret_gn[layer], dotc, lru_c[0], lru_c[1], lyc, wo)
            xc = ffn(xc, g_norm2[layer], sh2c, sc2c, gt2c)
    return _final_norm(x, final_norm)
```

```python
import functools
import math

import jax
import jax.numpy as jnp
from jax import lax
from jax.experimental import pallas as pl
from jax.experimental.pallas import tpu as pltpu

F32 = jnp.float32
BF16 = jnp.bfloat16

D_MODEL = 1024
DEPTH = 2
GRID_W = 64
HEAD_DIM = 64
GROUP = D_MODEL // 4
N_HEADS = GROUP // HEAD_DIM
DIFF_DIM = HEAD_DIM // 2
WINDOW = 128
NORM_EPS = 1e-6
ROPE_BASE = 10000.0
LRU_BLOCKS = 4
LRU_C = 8.0
N_EXPERTS = 8
LANE = 128
LOG2E = 1.4426950408889634
DIFF_TQ = 256
DIFF_TK = 256
DIFF_VROWS = HEAD_DIM + 16

OFF_AQ, OFF_AK, OFF_AV = 0, 256, 384
OFF_RQ, OFF_RK, OFF_RV, OFF_RG = 512, 768, 1024, 1280
OFF_DQ, OFF_DK, OFF_DV = 1536, 1792, 2048
OFF_LX, OFF_LY = 2304, 2560
D_IN = 2816

VMEM_LIMIT = 56 * 1024 * 1024


def _params(*sem):
    return pltpu.CompilerParams(dimension_semantics=sem, vmem_limit_bytes=VMEM_LIMIT)


def _silu(x):
    return x * jax.nn.sigmoid(x)


def _softplus(x):
    return jnp.maximum(x, 0.0) + jnp.log(1.0 + jnp.exp(-jnp.abs(x)))


def _dot(a, b):
    return jnp.dot(a, b, preferred_element_type=F32)


def _dot_nt(a, b):
    return lax.dot_general(a, b, (((1,), (1,)), ((), ())), preferred_element_type=F32)


def _mod_kernel(c_ref, w_ref, b_ref, o_ref):
    s = _silu(c_ref[...])
    o_ref[0] = _dot(s.astype(BF16), w_ref[0].astype(BF16)) + b_ref[0]


def _mod_vectors(cc, w_ada, b_ada):
    depth, d, n6 = w_ada.shape
    tn = 1536
    return pl.pallas_call(
        _mod_kernel,
        out_shape=jax.ShapeDtypeStruct((depth, 8, n6), F32),
        grid=(depth, n6 // tn),
        in_specs=[pl.BlockSpec((8, d), lambda l, j: (0, 0)),
                  pl.BlockSpec((1, d, tn), lambda l, j: (l, 0, j)),
                  pl.BlockSpec((1, 1, tn), lambda l, j: (l, 0, j))],
        out_specs=pl.BlockSpec((1, 8, tn), lambda l, j: (l, 0, j)),
        compiler_params=_params("arbitrary", "arbitrary"),
        name="mod_vectors",
    )(cc, w_ada, b_ada.reshape(depth, 1, n6))


def _rope_tile(x, cos, sin_signed, d):
    lane = lax.broadcasted_iota(jnp.int32, x.shape, 1)
    first = (lane & (d - 1)) < (d // 2)
    partner = jnp.where(first, pltpu.roll(x, LANE - d // 2, axis=1), pltpu.roll(x, d // 2, axis=1))
    return x * cos + partner * sin_signed


def _inproj_kernel(*refs, rope, tm):
    if rope:
        (x_ref, g_ref, sh_ref, sc_ref, w_ref, ca_ref, sa_ref, cr_ref, sr_ref, cd_ref, sd_ref,
         qa_ref, ka_ref, va_ref, rq_ref, rk_ref, rv_ref, rg_ref,
         dqt_ref, dk_ref, dvt_ref, lx_ref, ly_ref) = refs
    else:
        (x_ref, g_ref, sh_ref, sc_ref, w_ref,
         qa_ref, ka_ref, va_ref, rq_ref, rk_ref, rv_ref, rg_ref,
         dqt_ref, dk_ref, dvt_ref, lx_ref, ly_ref) = refs
    x = x_ref[0]
    ms = jnp.mean(x * x, axis=-1, keepdims=True)
    y = x * lax.rsqrt(ms + NORM_EPS) * g_ref[...]
    h = y * (1.0 + sc_ref[0]) + sh_ref[0]
    hb = h.astype(BF16)

    def proj(off, width):
        return _dot(hb, w_ref[:, off:off + width])

    def roped(p, c_ref, s_ref, d, scale):
        tiles = []
        for t in range(p.shape[1] // LANE):
            pt = p[:, t * LANE:(t + 1) * LANE]
            if rope:
                pt = _rope_tile(pt, c_ref[...], s_ref[...], d)
            tiles.append(pt if scale == 1.0 else pt * scale)
        return tiles[0] if len(tiles) == 1 else jnp.concatenate(tiles, axis=1)

    ca = sa = cr = sr = cd = sd = None
    if rope:
        ca, sa, cr, sr, cd, sd = ca_ref, sa_ref, cr_ref, sr_ref, cd_ref, sd_ref
    qa_ref[0] = roped(proj(OFF_AQ, 256), ca, sa, HEAD_DIM, HEAD_DIM ** -0.5).astype(BF16)
    ka_ref[0] = roped(proj(OFF_AK, 128), ca, sa, HEAD_DIM, 1.0).astype(BF16)
    va_ref[0] = proj(OFF_AV, 128).astype(BF16)
    rq_ref[0] = roped(proj(OFF_RQ, 256), cr, sr, HEAD_DIM, 1.0).astype(BF16)
    rk_ref[0] = roped(proj(OFF_RK, 256), cr, sr, HEAD_DIM, HEAD_DIM ** -0.5).astype(BF16)
    rv_ref[0] = proj(OFF_RV, 256).astype(BF16)
    rg_ref[0] = proj(OFF_RG, 256)
    dq = roped(proj(OFF_DQ, 256), cd, sd, DIFF_DIM, DIFF_DIM ** -0.5 * LOG2E)
    dqt_ref[0] = dq.T.astype(BF16)
    dk_ref[0] = roped(proj(OFF_DK, 256), cd, sd, DIFF_DIM, 1.0).astype(BF16)
    dvt = proj(OFF_DV, 256).T.astype(BF16)
    ones_rows = (lax.broadcasted_iota(jnp.int32, (DIFF_VROWS - HEAD_DIM, DIFF_TK), 0) == 0).astype(BF16)
    for j in range(tm // DIFF_TK):
        for hd in range(N_HEADS):
            dvt_ref[0, j, hd, 0:HEAD_DIM, :] = dvt[hd * HEAD_DIM:(hd + 1) * HEAD_DIM, j * DIFF_TK:(j + 1) * DIFF_TK]
            dvt_ref[0, j, hd, HEAD_DIM:DIFF_VROWS, :] = ones_rows
    lx_ref[0] = proj(OFF_LX, 256)
    ly_ref[0] = proj(OFF_LY, 256)


def _in_projection(x, g, sh, sc, w_in, tables):
    b, l, d = x.shape
    rope = tables is not None
    tm = 512 if l % 512 == 0 else 256
    nt = l // tm
    row = lambda shape: pl.BlockSpec(shape, lambda i, j: (i, j, 0))
    vec = pl.BlockSpec((1, 1, d), lambda i, j: (i, 0, 0))
    in_specs = [row((1, tm, d)), pl.BlockSpec((1, d), lambda i, j: (0, 0)), vec, vec,
                pl.BlockSpec((d, D_IN), lambda i, j: (0, 0))]
    args = [x, g.reshape(1, d), sh, sc, w_in]
    if rope:
        in_specs += [pl.BlockSpec((tm, LANE), lambda i, j: (j, 0))] * 6
        args += list(tables)
    sds = jax.ShapeDtypeStruct
    out_shape = [sds((b, l, 256), BF16), sds((b, l, 128), BF16), sds((b, l, 128), BF16),
                 sds((b, l, 256), BF16), sds((b, l, 256), BF16), sds((b, l, 256), BF16),
                 sds((b, l, 256), F32),
                 sds((b, 256, l), BF16), sds((b, l, 256), BF16),
                 sds((b, l // DIFF_TK, N_HEADS, DIFF_VROWS, DIFF_TK), BF16),
                 sds((b, l, 256), F32), sds((b, l, 256), F32)]
    out_specs = [row((1, tm, 256)), row((1, tm, 128)), row((1, tm, 128)),
                 row((1, tm, 256)), row((1, tm, 256)), row((1, tm, 256)), row((1, tm, 256)),
                 pl.BlockSpec((1, 256, tm), lambda i, j: (i, 0, j)), row((1, tm, 256)),
                 pl.BlockSpec((1, tm // DIFF_TK, N_HEADS, DIFF_VROWS, DIFF_TK), lambda i, j: (i, j, 0, 0, 0)),
                 row((1, tm, 256)), row((1, tm, 256))]
    return pl.pallas_call(
        functools.partial(_inproj_kernel, rope=rope, tm=tm),
        out_shape=out_shape, grid=(b, nt), in_specs=in_specs, out_specs=out_specs,
        compiler_params=_params("arbitrary", "arbitrary"),
        name="in_projection_rope" if rope else "in_projection_ctx",
    )(*args)


def _swa_kernel(*refs, window, tq, n):
    if window:
        sink_ref, q_ref, k_ref, v_ref, kc_ref, vc_ref, o_ref = refs
    else:
        sink_ref, q_ref, kc_ref, vc_ref, o_ref = refs
    kc = kc_ref[0]
    vc = vc_ref[0]
    m = kc.shape[0]
    if window:
        i = pl.program_id(1)
        w = tq + 2 * WINDOW
        t0 = i * tq
        start = pl.multiple_of(jnp.clip(t0 - WINDOW, 0, n - w), WINDOW)
        kcat = jnp.concatenate([k_ref[0, pl.ds(start, w), :], kc], axis=0)
        vcat = jnp.concatenate([v_ref[0, pl.ds(start, w), :], vc], axis=0)
        qpos = t0 + lax.broadcasted_iota(jnp.int32, (tq, w + m), 0)
        col = lax.broadcasted_iota(jnp.int32, (tq, w + m), 1)
        valid = (jnp.abs(qpos - (start + col)) <= WINDOW) | (col >= w)
    else:
        kcat, vcat, valid = kc, vc, None
    q = q_ref[0]
    lane = lax.broadcasted_iota(jnp.int32, (1, LANE), 1)
    out_tiles = []
    for t in range(2):
        qt = q[:, t * LANE:(t + 1) * LANE]
        out_t = None
        for half in range(2):
            in_half = (lane >= half * HEAD_DIM) & (lane < (half + 1) * HEAD_DIM)
            sink = sink_ref[2 * t + half]
            qm = jnp.where(in_half, qt, jnp.zeros_like(qt))
            s = _dot_nt(qm, kcat)
            if valid is not None:
                s = jnp.where(valid, s, -jnp.inf)
            mx = jnp.maximum(jnp.max(s, axis=1, keepdims=True), sink)
            e = jnp.exp(s - mx)
            den = jnp.sum(e, axis=1, keepdims=True) + jnp.exp(sink - mx)
            o = _dot(e.astype(BF16), vcat) / den
            out_t = o if out_t is None else jnp.where(in_half, o, out_t)
        out_tiles.append(out_t)
    o_ref[0] = jnp.concatenate(out_tiles, axis=1)


def _swa_attention(sink_perm, q, k, v, kc, vc):
    b, n, _ = q.shape
    m = kc.shape[1]
    tq = 256
    smem = pl.BlockSpec(memory_space=pltpu.SMEM)
    ctx = pl.BlockSpec((1, m, 128), lambda i, j: (i, 0, 0))
    in_specs = [smem, pl.BlockSpec((1, tq, 256), lambda i, j: (i, j, 0)),
                pl.BlockSpec((1, n, 128), lambda i, j: (i, 0, 0)),
                pl.BlockSpec((1, n, 128), lambda i, j: (i, 0, 0)), ctx, ctx]
    return pl.pallas_call(
        functools.partial(_swa_kernel, window=True, tq=tq, n=n),
        out_shape=jax.ShapeDtypeStruct((b, n, 256), F32),
        grid=(b, n // tq), in_specs=in_specs,
        out_specs=pl.BlockSpec((1, tq, 256), lambda i, j: (i, j, 0)),
        compiler_params=_params("arbitrary", "arbitrary"),
        name="swa_attention",
    )(sink_perm, q, k, v, kc, vc)


def _ctx_attention(sink_perm, qc, kc, vc):
    b, m, _ = qc.shape
    smem = pl.BlockSpec(memory_space=pltpu.SMEM)
    ctx = pl.BlockSpec((1, m, 128), lambda i: (i, 0, 0))
    return pl.pallas_call(
        functools.partial(_swa_kernel, window=False, tq=m, n=m),
        out_shape=jax.ShapeDtypeStruct((b, m, 256), F32),
        grid=(b,), in_specs=[smem, pl.BlockSpec((1, m, 256), lambda i: (i, 0, 0)), ctx, ctx],
        out_specs=pl.BlockSpec((1, m, 256), lambda i: (i, 0, 0)),
        compiler_params=_params("arbitrary"),
        name="ctx_attention",
    )(sink_perm, qc, kc, vc)


RET_CHUNK = 128


def _ret_kernel(dl_lane_ref, dl_head_ref, q_ref, k_ref, v_ref, s0_ref, y_ref, sfin_ref, s_scr, *, nc):
    c = RET_CHUNK
    d = pl.program_id(1)
    ci = pl.program_id(2)
    fwd = d == 0

    @pl.when(ci == 0)
    def _():
        s_scr[...] = s0_ref[0, 0]

    q = q_ref[0]
    k = k_ref[0]
    v = v_ref[0]
    lgl = -_softplus(-dl_lane_ref[0])
    row = lax.broadcasted_iota(jnp.int32, (c, 1), 0)
    xe = jnp.where(fwd, row + 1, c - row).astype(F32)
    ze = jnp.where(fwd, c - 1 - row, row).astype(F32)
    xi = jnp.exp(lgl * xe)
    zeta = jnp.exp(lgl * ze)
    cross = _dot(q, s_scr[...].astype(BF16)) * xi
    ni = lax.broadcasted_iota(jnp.int32, (c, c), 0)
    mi = lax.broadcasted_iota(jnp.int32, (c, c), 1)
    dd = jnp.where(fwd, ni - mi, mi - ni)
    ddf = jnp.maximum(dd, 0).astype(F32)
    lane_head = lax.broadcasted_iota(jnp.int32, (1, GROUP), 1) // HEAD_DIM
    inner = jnp.zeros((c, GROUP), F32)
    for h in range(N_HEADS):
        lgh = -_softplus(-dl_head_ref[0, h][0:1, :])
        dm = jnp.where(dd >= 0, jnp.exp(lgh * ddf), 0.0)
        qh = jnp.where(lane_head == h, q, jnp.zeros_like(q))
        s = _dot_nt(qh, k) * dm
        ih = _dot(s.astype(BF16), v)
        inner = jnp.where(lane_head == h, ih, inner)
    y_ref[0, 0] = inner + cross
    kz = (k.astype(F32) * zeta).T.astype(BF16)
    u = _dot(kz, v)
    dec = jnp.exp(lgl * float(c))
    rh = lax.broadcasted_iota(jnp.int32, (GROUP, GROUP), 0) // HEAD_DIM
    ch = lax.broadcasted_iota(jnp.int32, (GROUP, GROUP), 1) // HEAD_DIM
    s_scr[...] = jnp.where(rh == ch, dec * s_scr[...] + u, 0.0)

    @pl.when(ci == nc - 1)
    def _():
        sfin_ref[0, 0] = s_scr[...]


def _retention(dl_lane, dl_head, q, k, v, s0):
    b, l, _ = q.shape
    c = RET_CHUNK
    nc = l // c
    cidx = lambda i, d, j: (i, jnp.where(d == 0, j, nc - 1 - j), 0)
    tok = pl.BlockSpec((1, c, 256), cidx)
    st = pl.BlockSpec((1, 1, 256, 256), lambda i, d, j: (i, d, 0, 0))
    return pl.pallas_call(
        functools.partial(_ret_kernel, nc=nc),
        out_shape=[jax.ShapeDtypeStruct((b, 2, l, 256), F32),
                   jax.ShapeDtypeStruct((b, 2, 256, 256), F32)],
        grid=(b, 2, nc),
        in_specs=[pl.BlockSpec((1, 1, 256), lambda i, d, j: (d, 0, 0)),
                  pl.BlockSpec((1, N_HEADS, 8, c), lambda i, d, j: (d, 0, 0, 0)),
                  tok, tok, tok, st],
        out_specs=[pl.BlockSpec((1, 1, c, 256), lambda i, d, j: (i, d, jnp.where(d == 0, j, nc - 1 - j), 0)),
                   st],
        scratch_shapes=[pltpu.VMEM((GROUP, GROUP), F32)],
        compiler_params=_params("arbitrary", "arbitrary", "arbitrary"),
        name="retention",
    )(dl_lane, dl_head, q, k, v, s0)


def _inner_tiles(tiles):
    for g in (8, 7, 6, 5, 4, 3, 2):
        if tiles % g == 0:
            return g
    return 1


def _diff_kernel(lam_ref, gn_ref, qt_ref, k_ref, vt_ref, o_ref, s_scr, *, tiles, lam_init):
    h = pl.program_id(1)
    inner = _inner_tiles(tiles)
    outer = tiles // inner
    tq = qt_ref.shape[2]
    lv = lam_ref[...]
    lam = (jnp.exp(jnp.sum(lv[0:1] * lv[1:2], axis=1, keepdims=True))
           - jnp.exp(jnp.sum(lv[2:3] * lv[3:4], axis=1, keepdims=True)) + lam_init)
    qt = qt_ref[0]
    rowi = lax.broadcasted_iota(jnp.int32, (LANE, 1), 0)
    hoff = (h % 2) * HEAD_DIM
    qpads = []
    for mp in range(2):
        lo = hoff + mp * DIFF_DIM
        qpads.append(jnp.where((rowi >= lo) & (rowi < lo + DIFF_DIM), qt, jnp.zeros_like(qt)))

    def step(mp, q_chunk, e_chunk, slot, m_e):
        mx = pv = None
        for t in range(inner):
            sl = slice(t * DIFF_TK, (t + 1) * DIFF_TK)
            if q_chunk is not None:
                base = pl.multiple_of((q_chunk * inner + t) * DIFF_TK, DIFF_TK)
                s = _dot(k_ref[0, pl.ds(base, DIFF_TK), :], qpads[mp])
                s_scr[mp, 1 - slot, sl, :] = s
                cm = jnp.max(s, axis=0, keepdims=True)
                mx = cm if mx is None else jnp.maximum(mx, cm)
            if e_chunk is not None:
                p = jnp.exp2(s_scr[mp, slot, sl, :] - m_e).astype(BF16)
                d = _dot(vt_ref[0, e_chunk * inner + t, 0], p)
                pv = d if pv is None else pv + d
        return mx, pv

    def advance(mp, q_chunk, e_chunk, slot, state):
        m_e, alpha, acc = state
        mx, pv = step(mp, q_chunk, e_chunk, slot, m_e)
        acc = alpha * acc + pv
        if q_chunk is None:
            return m_e, alpha, acc
        m_next = jnp.maximum(m_e, mx)
        return m_next, jnp.exp2(m_e - m_next), acc

    zeros = jnp.zeros((DIFF_VROWS, tq), F32)
    states = []
    for mp in range(2):
        m_first, _ = step(mp, 0, None, 1, None)
        states.append((m_first, jnp.zeros((1, tq), F32), zeros))
    states = tuple(states)

    def body(i, carry):
        carry = tuple(advance(mp, 2 * i + 1, 2 * i, 0, carry[mp]) for mp in range(2))
        return tuple(advance(mp, 2 * i + 2, 2 * i + 1, 1, carry[mp]) for mp in range(2))

    pairs = (outer - 1) // 2
    states = lax.fori_loop(0, pairs, body, states)
    last = outer - 1
    if last % 2 == 1:
        states = tuple(advance(mp, last, last - 1, 0, states[mp]) for mp in range(2))
    states = [advance(mp, None, last, last % 2, states[mp]) for mp in range(2)]
    maps = [acc[0:HEAD_DIM] / acc[HEAD_DIM:HEAD_DIM + 1] for _, _, acc in states]
    o = maps[0] - lam * maps[1]
    ms = jnp.mean(o * o, axis=0, keepdims=True)
    o_ref[0] = o * lax.rsqrt(ms + NORM_EPS) * (gn_ref[0] * (1.0 - lam_init))


def _diff_attention(lam_vecs, gn_col, qt, k_all, vt_all, lam_init):
    b, _, lq = qt.shape
    lk = k_all.shape[1]
    tiles = lk // DIFF_TK
    tq = DIFF_TQ
    return pl.pallas_call(
        functools.partial(_diff_kernel, tiles=tiles, lam_init=lam_init),
        out_shape=jax.ShapeDtypeStruct((b, 256, lq), F32),
        grid=(b, N_HEADS, lq // tq),
        in_specs=[pl.BlockSpec((4, DIFF_DIM), lambda i, h, j: (0, 0)),
                  pl.BlockSpec((1, HEAD_DIM, 1), lambda i, h, j: (h, 0, 0)),
                  pl.BlockSpec((1, LANE, tq), lambda i, h, j: (i, h // 2, j)),
                  pl.BlockSpec((1, lk, LANE), lambda i, h, j: (i, 0, h // 2)),
                  pl.BlockSpec((1, tiles, 1, DIFF_VROWS, DIFF_TK), lambda i, h, j: (i, 0, h, 0, 0))],
        out_specs=pl.BlockSpec((1, HEAD_DIM, tq), lambda i, h, j: (i, h, j)),
        scratch_shapes=[pltpu.VMEM((2, 2, _inner_tiles(tiles) * DIFF_TK, tq), F32)],
        compiler_params=_params("arbitrary", "arbitrary", "arbitrary"),
        name="diff_attention",
    )(lam_vecs, gn_col, qt, k_all, vt_all)


def _lru_kernel(x_ref, xp_ref, xn_ref, cw_ref, cb_ref, wa_ref, ba_ref, wx_ref, bx_ref, lam_ref,
                h0_ref, h_ref, carry, *, reverse, tm, nt):
    j = pl.program_id(1)
    tix = (nt - 1 - j) if reverse else j

    @pl.when(j == 0)
    def _():
        carry[...] = jnp.broadcast_to(h0_ref[0], carry.shape)

    x = x_ref[0]
    row = lax.broadcasted_iota(jnp.int32, (tm, 1), 0)
    prev = jnp.where(tix > 0, xp_ref[0][7:8, :], 0.0)
    nxt0 = jnp.where(tix < nt - 1, xn_ref[0][0:1, :], 0.0)
    nxt1 = jnp.where(tix < nt - 1, xn_ref[0][1:2, :], 0.0)
    xm1 = jnp.where(row == 0, prev, pltpu.roll(x, 1, axis=0))
    xp1 = jnp.where(row == tm - 1, nxt0, pltpu.roll(x, tm - 1, axis=0))
    xp2 = jnp.where(row == tm - 1, nxt1, jnp.where(row == tm - 2, nxt0, pltpu.roll(x, tm - 2, axis=0)))
    cw = cw_ref[...]
    u = cw[0:1] * xm1 + cw[1:2] * x + cw[2:3] * xp1 + cw[3:4] * xp2 + cb_ref[...]
    ub = u.astype(BF16)
    r = jax.nn.sigmoid(_dot(ub, wa_ref[0]) + ba_ref[0])
    gi = jax.nn.sigmoid(_dot(ub, wx_ref[0]) + bx_ref[0])
    log_a = -LRU_C * r * _softplus(-lam_ref[0])
    a = jnp.exp(log_a)
    th = jnp.tanh(log_a)
    bv = jnp.sqrt(-2.0 * th / (1.0 - th)) * (gi * u)
    k = 1
    while k < tm:
        if reverse:
            keep = row < tm - k
            shift = tm - k
        else:
            keep = row >= k
            shift = k
        a_sh = jnp.where(keep, pltpu.roll(a, shift, axis=0), 1.0)
        b_sh = jnp.where(keep, pltpu.roll(bv, shift, axis=0), 0.0)
        bv = a * b_sh + bv
        a = a * a_sh
        k *= 2
    hcur = a * carry[0:1, :] + bv
    h_ref[0] = hcur
    last = hcur[0:1, :] if reverse else hcur[tm - 1:tm, :]
    carry[...] = jnp.broadcast_to(last, carry.shape)


def _lru_scan(lx, conv_w, conv_b, wa_bd, ba, wx_bd, bx, lam, h0, reverse):
    b, l, w = lx.shape
    tm = 512 if l % 512 == 0 else 256
    nt = l // tm
    tb = tm // 8
    tidx = (lambda j: nt - 1 - j) if reverse else (lambda j: j)
    vec = pl.BlockSpec((1, w), lambda i, j: (0, 0))
    mat = pl.BlockSpec((1, w, w), lambda i, j: (0, 0, 0))
    return pl.pallas_call(
        functools.partial(_lru_kernel, reverse=reverse, tm=tm, nt=nt),
        out_shape=jax.ShapeDtypeStruct((b, l, w), F32),
        grid=(b, nt),
        in_specs=[pl.BlockSpec((1, tm, w), lambda i, j: (i, tidx(j), 0)),
                  pl.BlockSpec((1, 8, w), lambda i, j: (i, jnp.maximum(tidx(j) * tb - 1, 0), 0)),
                  pl.BlockSpec((1, 8, w), lambda i, j: (i, jnp.minimum((tidx(j) + 1) * tb, l // 8 - 1), 0)),
                  pl.BlockSpec((4, w), lambda i, j: (0, 0)), vec,
                  mat, vec, mat, vec, vec,
                  pl.BlockSpec((1, 1, w), lambda i, j: (i, 0, 0))],
        out_specs=pl.BlockSpec((1, tm, w), lambda i, j: (i, tidx(j), 0)),
        scratch_shapes=[pltpu.VMEM((8, w), F32)],
        compiler_params=_params("arbitrary", "arbitrary"),
        name="lru_scan_rev" if reverse else "lru_scan_fwd",
    )(lx, lx, lx, conv_w, conv_b.reshape(1, w), wa_bd.reshape(1, w, w), ba.reshape(1, w),
      wx_bd.reshape(1, w, w), bx.reshape(1, w), lam.reshape(1, w), h0)


def _outproj_kernel(x_ref, gt_ref, oa_ref, ry_ref, rg_ref, rgn_ref, dot_ref, hf_ref, hb_ref, ly_ref,
                    bd_ref, w_ref, o_ref):
    oa = oa_ref[0]
    y = ry_ref[0, 0] + ry_ref[0, 1]
    y2 = y * y
    hi = y2.astype(BF16)
    lo = (y2 - hi.astype(F32)).astype(BF16)
    ms = _dot(hi, bd_ref[...]) + _dot(lo, bd_ref[...])
    ob = y * lax.rsqrt(ms + NORM_EPS) * rgn_ref[...] * _silu(rg_ref[0])
    od = dot_ref[0].T
    ol = (hf_ref[0] + hb_ref[0]) * jax.nn.gelu(ly_ref[0])
    acc = _dot(oa.astype(BF16), w_ref[0:256, :])
    acc = acc + _dot(ob.astype(BF16), w_ref[256:512, :])
    acc = acc + _dot(od.astype(BF16), w_ref[512:768, :])
    acc = acc + _dot(ol.astype(BF16), w_ref[768:1024, :])
    o_ref[0] = x_ref[0] + gt_ref[0] * acc


def _out_projection(x, gt, oa, ry, rg, ret_gn, dot, hf, hb, ly, w_out):
    b, l, d = x.shape
    tm = 512 if l % 512 == 0 else 256
    row = lambda wdt: pl.BlockSpec((1, tm, wdt), lambda i, j: (i, j, 0))
    hd = jnp.arange(GROUP) // HEAD_DIM
    bd = jnp.where(hd[:, None] == hd[None, :], 1.0 / HEAD_DIM, 0.0).astype(BF16)
    return pl.pallas_call(
        _outproj_kernel,
        out_shape=jax.ShapeDtypeStruct((b, l, d), F32),
        grid=(b, l // tm),
        in_specs=[row(d), pl.BlockSpec((1, 1, d), lambda i, j: (i, 0, 0)), row(256),
                  pl.BlockSpec((1, 2, tm, 256), lambda i, j: (i, 0, j, 0)), row(256),
                  pl.BlockSpec((1, 256), lambda i, j: (0, 0)),
                  pl.BlockSpec((1, 256, tm), lambda i, j: (i, 0, j)),
                  row(256), row(256), row(256),
                  pl.BlockSpec((256, 256), lambda i, j: (0, 0)),
                  pl.BlockSpec((d, d), lambda i, j: (0, 0))],
        out_specs=row(d),
        compiler_params=_params("arbitrary", "arbitrary"),
        name="out_projection",
    )(x, gt, oa, ry, rg, ret_gn.reshape(1, 256), dot, hf, hb, ly, bd, w_out)


def _norm_modulate(x_ref, g_ref, sh_ref, sc_ref):
    x = x_ref[0]
    ms = jnp.mean(x * x, axis=-1, keepdims=True)
    y = x * lax.rsqrt(ms + NORM_EPS) * g_ref[...]
    return y * (1.0 + sc_ref[0]) + sh_ref[0]


def _ffn_kernel(x_ref, g_ref, sh_ref, sc_ref, gt_ref, w1_ref, w3_ref, w2_ref, o_ref, hb_scr, acc_scr,
                *, n_f):
    f = pl.program_id(2)

    @pl.when(f == 0)
    def _():
        hb_scr[...] = _norm_modulate(x_ref, g_ref, sh_ref, sc_ref).astype(BF16)
        acc_scr[...] = jnp.zeros_like(acc_scr)

    hb = hb_scr[...]
    a = _silu(_dot(hb, w1_ref[...])) * _dot(hb, w3_ref[...])
    acc_scr[...] += _dot(a.astype(BF16), w2_ref[...])

    @pl.when(f == n_f - 1)
    def _():
        o_ref[0] = x_ref[0] + gt_ref[0] * acc_scr[...]


def _ffn(x, g, sh, sc, gt, w1, w3, w2):
    b, l, d = x.shape
    ff = w1.shape[1]
    tm = 1024 if l % 1024 == 0 else 256
    n_f = 2
    fc = ff // n_f
    vec = pl.BlockSpec((1, 1, d), lambda i, j, f: (i, 0, 0))
    return pl.pallas_call(
        functools.partial(_ffn_kernel, n_f=n_f),
        out_shape=jax.ShapeDtypeStruct((b, l, d), F32),
        grid=(b, l // tm, n_f),
        in_specs=[pl.BlockSpec((1, tm, d), lambda i, j, f: (i, j, 0)),
                  pl.BlockSpec((1, d), lambda i, j, f: (0, 0)), vec, vec, vec,
                  pl.BlockSpec((d, fc), lambda i, j, f: (0, f)),
                  pl.BlockSpec((d, fc), lambda i, j, f: (0, f)),
                  pl.BlockSpec((fc, d), lambda i, j, f: (f, 0))],
        out_specs=pl.BlockSpec((1, tm, d), lambda i, j, f: (i, j, 0)),
        scratch_shapes=[pltpu.VMEM((tm, d), BF16), pltpu.VMEM((tm, d), F32)],
        compiler_params=_params("arbitrary", "arbitrary", "arbitrary"),
        name="dense_ffn",
    )(x, g.reshape(1, d), sh, sc, gt, w1, w3, w2)


MOE_RT = 128


def _moe_kernel(x_ref, g_ref, sh_ref, sc_ref, gt_ref, rt_ref, w1_ref, w3_ref, w2_ref, o_ref,
                hb_scr, acc_scr, gate_scr, pos_scr, gatet_scr, post_scr, x_scr, y_scr, gs_scr, cnt_smem,
                *, n_e, n_f, tm):
    e = pl.program_id(2)
    f = pl.program_id(3)
    rt2 = 2 * MOE_RT

    @pl.when((e == 0) & (f == 0))
    def _():
        h = _norm_modulate(x_ref, g_ref, sh_ref, sc_ref)
        hb = h.astype(BF16)
        hb_scr[...] = hb
        acc_scr[...] = jnp.zeros_like(acc_scr)
        rt = rt_ref[...]
        rhi = rt.astype(BF16)
        rlo = (rt - rhi.astype(F32)).astype(BF16)
        hlo = (h - hb.astype(F32)).astype(BF16)
        logits = _dot(hb, rhi) + _dot(hb, rlo) + _dot(hlo, rhi)
        lane = lax.broadcasted_iota(jnp.int32, logits.shape, 1)
        logits = jnp.where(lane < N_EXPERTS, logits, -jnp.inf)
        m1 = jnp.max(logits, axis=1, keepdims=True)
        i1 = jnp.min(jnp.where(logits == m1, lane, LANE), axis=1, keepdims=True)
        rest = jnp.where(lane == i1, -jnp.inf, logits)
        m2 = jnp.max(rest, axis=1, keepdims=True)
        i2 = jnp.min(jnp.where(rest == m2, lane, LANE), axis=1, keepdims=True)
        e2 = jnp.exp(m2 - m1)
        den = 1.0 + e2
        gate = jnp.where(lane == i1, 1.0 / den, 0.0) + jnp.where(lane == i2, e2 / den, 0.0)
        gate_scr[...] = gate
        sel = jnp.where(gate > 0.0, 1.0, 0.0)
        tok_r = lax.broadcasted_iota(jnp.int32, (tm, tm), 0)
        tok_c = lax.broadcasted_iota(jnp.int32, (tm, tm), 1)
        earlier = jnp.where(tok_c < tok_r, 1.0, 0.0).astype(BF16)
        pos = _dot(earlier, sel.astype(BF16))
        pos_scr[...] = pos
        post_scr[...] = pos.T
        gatet_scr[...] = gate.T
        for ee in range(n_e):
            cnt_smem[ee] = jnp.sum(jnp.where(lane == ee, sel, 0.0)).astype(jnp.int32)

    n_tiles = (cnt_smem[e] + (MOE_RT - 1)) // MOE_RT
    n_pairs = (n_tiles + 1) // 2

    @pl.when(f == 0)
    def _():
        sub = lax.broadcasted_iota(jnp.int32, (8, 1), 0)
        pos_e = jnp.sum(jnp.where(sub == e, post_scr[0:8, :], 0.0), axis=0, keepdims=True)
        gate_e = jnp.sum(jnp.where(sub == e, gatet_scr[0:8, :], 0.0), axis=0, keepdims=True)

        def gather(j, carry):
            r0 = pl.multiple_of(j * MOE_RT, MOE_RT)
            slot = (r0 + lax.broadcasted_iota(jnp.int32, (MOE_RT, 1), 0)).astype(F32)
            hit = (pos_e == slot) & (gate_e > 0.0)
            onehot = jnp.where(hit, 1.0, 0.0).astype(BF16)
            x_scr[pl.ds(r0, MOE_RT), :] = _dot(onehot, hb_scr[...]).astype(BF16)
            gs = jnp.sum(jnp.where(hit, gate_e, 0.0), axis=1, keepdims=True)
            gs_scr[pl.ds(r0, MOE_RT), :] = jnp.broadcast_to(gs, (MOE_RT, LANE))
            y_scr[pl.ds(r0, MOE_RT), :] = jnp.zeros((MOE_RT, y_scr.shape[1]), F32)
            return carry

        lax.fori_loop(0, 2 * n_pairs, gather, 0)

    def expert(j, carry):
        r0 = pl.multiple_of(j * MOE_RT, MOE_RT)
        xj = x_scr[pl.ds(r0, MOE_RT), :]
        a = _silu(_dot(xj, w1_ref[0])) * _dot(xj, w3_ref[0])
        y_scr[pl.ds(r0, MOE_RT), :] += _dot(a.astype(BF16), w2_ref[0])
        return carry

    lax.fori_loop(0, n_tiles, expert, 0)

    @pl.when(f == n_f - 1)
    def _():
        lane = lax.broadcasted_iota(jnp.int32, (1, LANE), 1)
        pos_col = jnp.sum(jnp.where(lane == e, pos_scr[...], 0.0), axis=1, keepdims=True)
        gate_col = jnp.sum(jnp.where(lane == e, gate_scr[...], 0.0), axis=1, keepdims=True)

        def scatter(jj, carry):
            r0 = pl.multiple_of(jj * rt2, rt2)
            slot = (r0 + lax.broadcasted_iota(jnp.int32, (1, rt2), 1)).astype(F32)
            onehot = jnp.where((pos_col == slot) & (gate_col > 0.0), 1.0, 0.0).astype(BF16)
            ys = (y_scr[pl.ds(r0, rt2), :] * gs_scr[pl.ds(r0, rt2), 0:1]).astype(BF16)
            acc_scr[...] += _dot(onehot, ys)
            return carry

        lax.fori_loop(0, n_pairs, scatter, 0)

    @pl.when((e == n_e - 1) & (f == n_f - 1))
    def _():
        o_ref[0] = x_ref[0] + gt_ref[0] * acc_scr[...]


def _moe_ffn(x, g, sh, sc, gt, w1, w3, w2, router):
    b, l, d = x.shape
    n_e, _, ff = w1.shape
    tm = 1024 if l % 1024 == 0 else 256
    n_f = 4
    fc = ff // n_f
    vec = pl.BlockSpec((1, 1, d), lambda i, j, e, f: (i, 0, 0))
    return pl.pallas_call(
        functools.partial(_moe_kernel, n_e=n_e, n_f=n_f, tm=tm),
        out_shape=jax.ShapeDtypeStruct((b, l, d), F32),
        grid=(b, l // tm, n_e, n_f),
        in_specs=[pl.BlockSpec((1, tm, d), lambda i, j, e, f: (i, j, 0)),
                  pl.BlockSpec((1, d), lambda i, j, e, f: (0, 0)), vec, vec, vec,
                  pl.BlockSpec((d, LANE), lambda i, j, e, f: (0, 0)),
                  pl.BlockSpec((1, d, fc), lambda i, j, e, f: (e, 0, f)),
                  pl.BlockSpec((1, d, fc), lambda i, j, e, f: (e, 0, f)),
                  pl.BlockSpec((1, fc, d), lambda i, j, e, f: (e, f, 0))],
        out_specs=pl.BlockSpec((1, tm, d), lambda i, j, e, f: (i, j, 0)),
        scratch_shapes=[pltpu.VMEM((tm, d), BF16),
                        pltpu.VMEM((tm, d), F32),
                        pltpu.VMEM((tm, LANE), F32),
                        pltpu.VMEM((tm, LANE), F32),
                        pltpu.VMEM((LANE, tm), F32),
                        pltpu.VMEM((LANE, tm), F32),
                        pltpu.VMEM((tm, d), BF16),
                        pltpu.VMEM((tm, d), F32),
                        pltpu.VMEM((tm, LANE), F32),
                        pltpu.SMEM((n_e,), jnp.int32)],
        compiler_params=_params("arbitrary", "arbitrary", "arbitrary", "arbitrary"),
        name="moe_ffn",
    )(x, g.reshape(1, d), sh, sc, gt, router, w1, w3, w2)


def _final_norm_kernel(x_ref, g_ref, o_ref):
    x = x_ref[0]
    ms = jnp.mean(x * x, axis=-1, keepdims=True)
    o_ref[0] = x * lax.rsqrt(ms + NORM_EPS) * g_ref[...]


def _final_norm(x, g):
    b, l, d = x.shape
    tm = 1024 if l % 1024 == 0 else 256
    return pl.pallas_call(
        _final_norm_kernel,
        out_shape=jax.ShapeDtypeStruct((b, l, d), F32),
        grid=(b, l // tm),
        in_specs=[pl.BlockSpec((1, tm, d), lambda i, j: (i, j, 0)),
                  pl.BlockSpec((1, d), lambda i, j: (0, 0))],
        out_specs=pl.BlockSpec((1, tm, d), lambda i, j: (i, j, 0)),
        compiler_params=_params("arbitrary", "arbitrary"),
        name="final_norm",
    )(x, g.reshape(1, d))


def _rope_tables(n):
    rows = n // GRID_W
    row = jnp.repeat(jnp.arange(rows, dtype=F32), GRID_W)
    col = jnp.tile(jnp.arange(GRID_W, dtype=F32), rows)

    def axial(dim):
        nf = dim // 4
        inv = ROPE_BASE ** (-jnp.arange(nf, dtype=F32) / nf)
        return jnp.concatenate([row[:, None] * inv, col[:, None] * inv], axis=-1)

    def line(dim):
        nf = dim // 2
        inv = ROPE_BASE ** (-jnp.arange(nf, dtype=F32) / nf)
        return jnp.arange(n, dtype=F32)[:, None] * inv

    def expand(ang, dim):
        c = jnp.cos(ang)
        s = jnp.sin(ang)
        c = jnp.tile(jnp.concatenate([c, c], axis=-1), (1, LANE // dim))
        s = jnp.tile(jnp.concatenate([-s, s], axis=-1), (1, LANE // dim))
        return c, s

    ca, sa = expand(axial(HEAD_DIM), HEAD_DIM)
    cr, sr = expand(line(HEAD_DIM), HEAD_DIM)
    cd, sd = expand(axial(DIFF_DIM), DIFF_DIM)
    return ca, sa, cr, sr, cd, sd


_ATTN_HEAD_ORDER = (0, 2, 1, 3)


def _attn_perm():
    return jnp.concatenate([jnp.arange(HEAD_DIM) + HEAD_DIM * h for h in _ATTN_HEAD_ORDER])


def _block_diag(w):
    nb, blk, _ = w.shape
    out = jnp.zeros((nb * blk, nb * blk), w.dtype)
    for i in range(nb):
        out = out.at[i * blk:(i + 1) * blk, i * blk:(i + 1) * blk].set(w[i])
    return out


def kernel(x, c, ctx, c_ctx, w_ada, b_ada, g_norm1, g_norm2, w_in, w_out, attn_sink, ret_decay_logit,
           ret_gn, diff_lambda, diff_gn, conv_w, conv_b, lru_wa, lru_ba, lru_wx, lru_bx, lru_lambda,
           ffn_w1, ffn_w3, ffn_w2, moe_router, moe_w1, moe_w3, moe_w2, final_norm):
    b, n, d = x.shape
    m = ctx.shape[1]
    tables = _rope_tables(n)
    perm = _attn_perm()

    cc = jnp.concatenate([c, c_ctx[None, :], jnp.zeros((8 - b - 1, d), F32)], axis=0)
    mods = _mod_vectors(cc, w_ada, b_ada)

    xc = ctx
    for layer in range(DEPTH):
        need_ctx = layer < DEPTH - 1
        lam_init = 0.8 - 0.6 * math.exp(-0.3 * layer)
        mod = mods[layer]
        sh1, sc1, gt1, sh2, sc2, gt2 = [mod[:b, None, i * d:(i + 1) * d] for i in range(6)]
        sh1c, sc1c, gt1c, sh2c, sc2c, gt2c = [
            jnp.broadcast_to(mod[b:b + 1, None, i * d:(i + 1) * d], (b, 1, d)) for i in range(6)]

        wi = w_in[layer]
        wi = jnp.concatenate([wi[:, :256][:, perm], wi[:, 256:]], axis=1).astype(BF16)
        wo = w_out[layer]
        wo = jnp.concatenate([wo[:256][perm], wo[256:]], axis=0).astype(BF16)
        sink_perm = attn_sink[layer][jnp.array(_ATTN_HEAD_ORDER)]

        (qa, ka, va, rq, rk, rv, rg, dqt, dk, dvt, lx, ly) = _in_projection(
            x, g_norm1[layer], sh1, sc1, wi, tables)
        (qac, kac, vac, rqc, rkc, rvc, rgc, dqtc, dkc, dvtc, lxc, lyc) = _in_projection(
            xc, g_norm1[layer], sh1c, sc1c, wi, None)

        oa = _swa_attention(sink_perm, qa, ka, va, kac, vac)
        dl = ret_decay_logit[layer]
        dl_lane = jnp.repeat(dl, HEAD_DIM, axis=1).reshape(2, 1, GROUP)
        dl_head = jnp.broadcast_to(dl[:, :, None, None], (2, N_HEADS, 8, RET_CHUNK))
        ryc, s_ctx = _retention(dl_lane, dl_head, rqc, rkc, rvc, jnp.zeros((b, 2, GROUP, GROUP), F32))
        ry, _ = _retention(dl_lane, dl_head, rq, rk, rv, s_ctx)
        gn_col = diff_gn[layer].reshape(N_HEADS, HEAD_DIM, 1)
        k_all = jnp.concatenate([dk, dkc], axis=1)
        vt_all = jnp.concatenate([dvt, dvtc], axis=1)
        dot = _diff_attention(diff_lambda[layer], gn_col, dqt, k_all, vt_all, lam_init)
        lru = []
        lru_c = []
        for direction in range(2):
            prm = (conv_w[layer], conv_b[layer],
                   _block_diag(lru_wa[layer, direction]).astype(BF16), lru_ba[layer, direction],
                   _block_diag(lru_wx[layer, direction]).astype(BF16), lru_bx[layer, direction],
                   lru_lambda[layer, direction])
            rev = direction == 1
            hc = _lru_scan(lxc, *prm, jnp.zeros((b, 1, GROUP), F32), rev)
            h0 = hc[:, 0:1, :] if rev else hc[:, m - 1:m, :]
            lru.append(_lru_scan(lx, *prm, h0, rev))
            lru_c.append(hc)

        x = _out_projection(x, gt1, oa, ry, rg, ret_gn[layer], dot, lru[0], lru[1], ly, wo)
        j = layer // 2
        if layer % 2 == 0:
            ffn = functools.partial(_ffn, w1=ffn_w1[j].astype(BF16), w3=ffn_w3[j].astype(BF16),
                                    w2=ffn_w2[j].astype(BF16))
        else:
            ffn = functools.partial(_moe_ffn, w1=moe_w1[j].astype(BF16), w3=moe_w3[j].astype(BF16),
                                    w2=moe_w2[j].astype(BF16),
                                    router=jnp.pad(moe_router[j], ((0, 0), (0, LANE - N_EXPERTS))))
        x = ffn(x, g_norm2[layer], sh2, sc2, gt2)

        if need_ctx:
            oac = _ctx_attention(sink_perm, qac, kac, vac)
            dotc = _diff_attention(diff_lambda[layer], gn_col, dqtc, dkc, dvtc, lam_init)
            xc = _out_projection(xc, gt1c, oac, ryc, rgc, ret_gn[layer], dotc, lru_c[0], lru_c[1], lyc, wo)
            xc = ffn(xc, g_norm2[layer], sh2c, sc2c, gt2c)
    return _final_norm(x, final_norm)
```

```python
import functools
import math

import jax
import jax.numpy as jnp
from jax import lax
from jax.experimental import pallas as pl
from jax.experimental.pallas import tpu as pltpu

F32 = jnp.float32
BF16 = jnp.bfloat16

D_MODEL = 1024
DEPTH = 2
GRID_W = 64
HEAD_DIM = 64
GROUP = D_MODEL // 4
N_HEADS = GROUP // HEAD_DIM
DIFF_DIM = HEAD_DIM // 2
WINDOW = 128
NORM_EPS = 1e-6
ROPE_BASE = 10000.0
LRU_BLOCKS = 4
LRU_C = 8.0
N_EXPERTS = 8
LANE = 128
LOG2E = 1.4426950408889634
DIFF_TQ = 256
DIFF_TK = 256
DIFF_VROWS = HEAD_DIM + 16

OFF_AQ, OFF_AK, OFF_AV = 0, 256, 384
OFF_RQ, OFF_RK, OFF_RV, OFF_RG = 512, 768, 1024, 1280
OFF_DQ, OFF_DK, OFF_DV = 1536, 1792, 2048
OFF_LX, OFF_LY = 2304, 2560
D_IN = 2816

VMEM_LIMIT = 56 * 1024 * 1024


def _params(*sem):
    return pltpu.CompilerParams(dimension_semantics=sem, vmem_limit_bytes=VMEM_LIMIT)


def _silu(x):
    return x * jax.nn.sigmoid(x)


def _softplus(x):
    return jnp.maximum(x, 0.0) + jnp.log(1.0 + jnp.exp(-jnp.abs(x)))


def _dot(a, b):
    return jnp.dot(a, b, preferred_element_type=F32)


def _dot_nt(a, b):
    return lax.dot_general(a, b, (((1,), (1,)), ((), ())), preferred_element_type=F32)


def _mod_kernel(c_ref, w_ref, b_ref, o_ref):
    s = _silu(c_ref[...])
    o_ref[0] = _dot(s.astype(BF16), w_ref[0].astype(BF16)) + b_ref[0]


def _mod_vectors(cc, w_ada, b_ada):
    depth, d, n6 = w_ada.shape
    tn = 1536
    return pl.pallas_call(
        _mod_kernel,
        out_shape=jax.ShapeDtypeStruct((depth, 8, n6), F32),
        grid=(depth, n6 // tn),
        in_specs=[pl.BlockSpec((8, d), lambda l, j: (0, 0)),
                  pl.BlockSpec((1, d, tn), lambda l, j: (l, 0, j)),
                  pl.BlockSpec((1, 1, tn), lambda l, j: (l, 0, j))],
        out_specs=pl.BlockSpec((1, 8, tn), lambda l, j: (l, 0, j)),
        compiler_params=_params("arbitrary", "arbitrary"),
        name="mod_vectors",
    )(cc, w_ada, b_ada.reshape(depth, 1, n6))


def _rope_tile(x, cos, sin_signed, d):
    lane = lax.broadcasted_iota(jnp.int32, x.shape, 1)
    first = (lane & (d - 1)) < (d // 2)
    partner = jnp.where(first, pltpu.roll(x, LANE - d // 2, axis=1), pltpu.roll(x, d // 2, axis=1))
    return x * cos + partner * sin_signed


def _inproj_kernel(*refs, rope, tm):
    if rope:
        (x_ref, g_ref, sh_ref, sc_ref, w_ref, ca_ref, sa_ref, cr_ref, sr_ref, cd_ref, sd_ref,
         qa_ref, ka_ref, va_ref, rq_ref, rk_ref, rv_ref, rg_ref,
         dqt_ref, dk_ref, dvt_ref, lx_ref, ly_ref) = refs
    else:
        (x_ref, g_ref, sh_ref, sc_ref, w_ref,
         qa_ref, ka_ref, va_ref, rq_ref, rk_ref, rv_ref, rg_ref,
         dqt_ref, dk_ref, dvt_ref, lx_ref, ly_ref) = refs
    x = x_ref[0]
    ms = jnp.mean(x * x, axis=-1, keepdims=True)
    y = x * lax.rsqrt(ms + NORM_EPS) * g_ref[...]
    h = y * (1.0 + sc_ref[0]) + sh_ref[0]
    hb = h.astype(BF16)

    def proj(off, width):
        return _dot(hb, w_ref[:, off:off + width])

    def roped(p, c_ref, s_ref, d, scale):
        tiles = []
        for t in range(p.shape[1] // LANE):
            pt = p[:, t * LANE:(t + 1) * LANE]
            if rope:
                pt = _rope_tile(pt, c_ref[...], s_ref[...], d)
            tiles.append(pt if scale == 1.0 else pt * scale)
        return tiles[0] if len(tiles) == 1 else jnp.concatenate(tiles, axis=1)

    ca = sa = cr = sr = cd = sd = None
    if rope:
        ca, sa, cr, sr, cd, sd = ca_ref, sa_ref, cr_ref, sr_ref, cd_ref, sd_ref
    qa_ref[0] = roped(proj(OFF_AQ, 256), ca, sa, HEAD_DIM, HEAD_DIM ** -0.5).astype(BF16)
    ka_ref[0] = roped(proj(OFF_AK, 128), ca, sa, HEAD_DIM, 1.0).astype(BF16)
    va_ref[0] = proj(OFF_AV, 128).astype(BF16)
    rq_ref[0] = roped(proj(OFF_RQ, 256), cr, sr, HEAD_DIM, 1.0).astype(BF16)
    rk_ref[0] = roped(proj(OFF_RK, 256), cr, sr, HEAD_DIM, HEAD_DIM ** -0.5).astype(BF16)
    rv_ref[0] = proj(OFF_RV, 256).astype(BF16)
    rg_ref[0] = proj(OFF_RG, 256)
    dq = roped(proj(OFF_DQ, 256), cd, sd, DIFF_DIM, DIFF_DIM ** -0.5 * LOG2E)
    dqt_ref[0] = dq.T.astype(BF16)
    dk_ref[0] = roped(proj(OFF_DK, 256), cd, sd, DIFF_DIM, 1.0).astype(BF16)
    dvt = proj(OFF_DV, 256).T.astype(BF16)
    ones_rows = (lax.broadcasted_iota(jnp.int32, (DIFF_VROWS - HEAD_DIM, DIFF_TK), 0) == 0).astype(BF16)
    for j in range(tm // DIFF_TK):
        for hd in range(N_HEADS):
            dvt_ref[0, j, hd, 0:HEAD_DIM, :] = dvt[hd * HEAD_DIM:(hd + 1) * HEAD_DIM, j * DIFF_TK:(j + 1) * DIFF_TK]
            dvt_ref[0, j, hd, HEAD_DIM:DIFF_VROWS, :] = ones_rows
    lx_ref[0] = proj(OFF_LX, 256)
    ly_ref[0] = proj(OFF_LY, 256)


def _in_projection(x, g, sh, sc, w_in, tables):
    b, l, d = x.shape
    rope = tables is not None
    tm = 512 if l % 512 == 0 else 256
    nt = l // tm
    row = lambda shape: pl.BlockSpec(shape, lambda i, j: (i, j, 0))
    vec = pl.BlockSpec((1, 1, d), lambda i, j: (i, 0, 0))
    in_specs = [row((1, tm, d)), pl.BlockSpec((1, d), lambda i, j: (0, 0)), vec, vec,
                pl.BlockSpec((d, D_IN), lambda i, j: (0, 0))]
    args = [x, g.reshape(1, d), sh, sc, w_in]
    if rope:
        in_specs += [pl.BlockSpec((tm, LANE), lambda i, j: (j, 0))] * 6
        args += list(tables)
    sds = jax.ShapeDtypeStruct
    out_shape = [sds((b, l, 256), BF16), sds((b, l, 128), BF16), sds((b, l, 128), BF16),
                 sds((b, l, 256), BF16), sds((b, l, 256), BF16), sds((b, l, 256), BF16),
                 sds((b, l, 256), F32),
                 sds((b, 256, l), BF16), sds((b, l, 256), BF16),
                 sds((b, l // DIFF_TK, N_HEADS, DIFF_VROWS, DIFF_TK), BF16),
                 sds((b, l, 256), F32), sds((b, l, 256), F32)]
    out_specs = [row((1, tm, 256)), row((1, tm, 128)), row((1, tm, 128)),
                 row((1, tm, 256)), row((1, tm, 256)), row((1, tm, 256)), row((1, tm, 256)),
                 pl.BlockSpec((1, 256, tm), lambda i, j: (i, 0, j)), row((1, tm, 256)),
                 pl.BlockSpec((1, tm // DIFF_TK, N_HEADS, DIFF_VROWS, DIFF_TK), lambda i, j: (i, j, 0, 0, 0)),
                 row((1, tm, 256)), row((1, tm, 256))]
    return pl.pallas_call(
        functools.partial(_inproj_kernel, rope=rope, tm=tm),
        out_shape=out_shape, grid=(b, nt), in_specs=in_specs, out_specs=out_specs,
        compiler_params=_params("arbitrary", "arbitrary"),
        name="in_projection_rope" if rope else "in_projection_ctx",
    )(*args)


def _swa_kernel(*refs, window, tq, n):
    if window:
        sink_ref, q_ref, k_ref, v_ref, kc_ref, vc_ref, o_ref = refs
    else:
        sink_ref, q_ref, kc_ref, vc_ref, o_ref = refs
    kc = kc_ref[0]
    vc = vc_ref[0]
    m = kc.shape[0]
    if window:
        i = pl.program_id(1)
        w = tq + 2 * WINDOW
        t0 = i * tq
        start = pl.multiple_of(jnp.clip(t0 - WINDOW, 0, n - w), WINDOW)
        kcat = jnp.concatenate([k_ref[0, pl.ds(start, w), :], kc], axis=0)
        vcat = jnp.concatenate([v_ref[0, pl.ds(start, w), :], vc], axis=0)
        qpos = t0 + lax.broadcasted_iota(jnp.int32, (tq, w + m), 0)
        col = lax.broadcasted_iota(jnp.int32, (tq, w + m), 1)
        valid = (jnp.abs(qpos - (start + col)) <= WINDOW) | (col >= w)
    else:
        kcat, vcat, valid = kc, vc, None
    q = q_ref[0]
    lane = lax.broadcasted_iota(jnp.int32, (1, LANE), 1)
    out_tiles = []
    for t in range(2):
        qt = q[:, t * LANE:(t + 1) * LANE]
        out_t = None
        for half in range(2):
            in_half = (lane >= half * HEAD_DIM) & (lane < (half + 1) * HEAD_DIM)
            sink = sink_ref[2 * t + half]
            qm = jnp.where(in_half, qt, jnp.zeros_like(qt))
            s = _dot_nt(qm, kcat)
            if valid is not None:
                s = jnp.where(valid, s, -jnp.inf)
            mx = jnp.maximum(jnp.max(s, axis=1, keepdims=True), sink)
            e = jnp.exp(s - mx)
            den = jnp.sum(e, axis=1, keepdims=True) + jnp.exp(sink - mx)
            o = _dot(e.astype(BF16), vcat) / den
            out_t = o if out_t is None else jnp.where(in_half, o, out_t)
        out_tiles.append(out_t)
    o_ref[0] = jnp.concatenate(out_tiles, axis=1)


def _swa_attention(sink_perm, q, k, v, kc, vc):
    b, n, _ = q.shape
    m = kc.shape[1]
    tq = 256
    smem = pl.BlockSpec(memory_space=pltpu.SMEM)
    ctx = pl.BlockSpec((1, m, 128), lambda i, j: (i, 0, 0))
    in_specs = [smem, pl.BlockSpec((1, tq, 256), lambda i, j: (i, j, 0)),
                pl.BlockSpec((1, n, 128), lambda i, j: (i, 0, 0)),
                pl.BlockSpec((1, n, 128), lambda i, j: (i, 0, 0)), ctx, ctx]
    return pl.pallas_call(
        functools.partial(_swa_kernel, window=True, tq=tq, n=n),
        out_shape=jax.ShapeDtypeStruct((b, n, 256), F32),
        grid=(b, n // tq), in_specs=in_specs,
        out_specs=pl.BlockSpec((1, tq, 256), lambda i, j: (i, j, 0)),
        compiler_params=_params("arbitrary", "arbitrary"),
        name="swa_attention",
    )(sink_perm, q, k, v, kc, vc)


def _ctx_attention(sink_perm, qc, kc, vc):
    b, m, _ = qc.shape
    smem = pl.BlockSpec(memory_space=pltpu.SMEM)
    ctx = pl.BlockSpec((1, m, 128), lambda i: (i, 0, 0))
    return pl.pallas_call(
        functools.partial(_swa_kernel, window=False, tq=m, n=m),
        out_shape=jax.ShapeDtypeStruct((b, m, 256), F32),
        grid=(b,), in_specs=[smem, pl.BlockSpec((1, m, 256), lambda i: (i, 0, 0)), ctx, ctx],
        out_specs=pl.BlockSpec((1, m, 256), lambda i: (i, 0, 0)),
        compiler_params=_params("arbitrary"),
        name="ctx_attention",
    )(sink_perm, qc, kc, vc)


RET_CHUNK = 128


def _ret_kernel(dl_lane_ref, dl_head_ref, q_ref, k_ref, v_ref, s0_ref, y_ref, sfin_ref, s_scr, *, nc):
    c = RET_CHUNK
    d = pl.program_id(1)
    ci = pl.program_id(2)
    fwd = d == 0

    @pl.when(ci == 0)
    def _():
        s_scr[...] = s0_ref[0, 0]

    q = q_ref[0]
    k = k_ref[0]
    v = v_ref[0]
    lgl = -_softplus(-dl_lane_ref[0])
    row = lax.broadcasted_iota(jnp.int32, (c, 1), 0)
    xe = jnp.where(fwd, row + 1, c - row).astype(F32)
    ze = jnp.where(fwd, c - 1 - row, row).astype(F32)
    xi = jnp.exp(lgl * xe)
    zeta = jnp.exp(lgl * ze)
    cross = _dot(q, s_scr[...].astype(BF16)) * xi
    ni = lax.broadcasted_iota(jnp.int32, (c, c), 0)
    mi = lax.broadcasted_iota(jnp.int32, (c, c), 1)
    dd = jnp.where(fwd, ni - mi, mi - ni)
    ddf = jnp.maximum(dd, 0).astype(F32)
    lane_head = lax.broadcasted_iota(jnp.int32, (1, GROUP), 1) // HEAD_DIM
    inner = jnp.zeros((c, GROUP), F32)
    for h in range(N_HEADS):
        lgh = -_softplus(-dl_head_ref[0, h][0:1, :])
        dm = jnp.where(dd >= 0, jnp.exp(lgh * ddf), 0.0)
        qh = jnp.where(lane_head == h, q, jnp.zeros_like(q))
        s = _dot_nt(qh, k) * dm
        ih = _dot(s.astype(BF16), v)
        inner = jnp.where(lane_head == h, ih, inner)
    y_ref[0, 0] = inner + cross
    kz = (k.astype(F32) * zeta).T.astype(BF16)
    u = _dot(kz, v)
    dec = jnp.exp(lgl * float(c))
    rh = lax.broadcasted_iota(jnp.int32, (GROUP, GROUP), 0) // HEAD_DIM
    ch = lax.broadcasted_iota(jnp.int32, (GROUP, GROUP), 1) // HEAD_DIM
    s_scr[...] = jnp.where(rh == ch, dec * s_scr[...] + u, 0.0)

    @pl.when(ci == nc - 1)
    def _():
        sfin_ref[0, 0] = s_scr[...]


def _retention(dl_lane, dl_head, q, k, v, s0):
    b, l, _ = q.shape
    c = RET_CHUNK
    nc = l // c
    cidx = lambda i, d, j: (i, jnp.where(d == 0, j, nc - 1 - j), 0)
    tok = pl.BlockSpec((1, c, 256), cidx)
    st = pl.BlockSpec((1, 1, 256, 256), lambda i, d, j: (i, d, 0, 0))
    return pl.pallas_call(
        functools.partial(_ret_kernel, nc=nc),
        out_shape=[jax.ShapeDtypeStruct((b, 2, l, 256), F32),
                   jax.ShapeDtypeStruct((b, 2, 256, 256), F32)],
        grid=(b, 2, nc),
        in_specs=[pl.BlockSpec((1, 1, 256), lambda i, d, j: (d, 0, 0)),
                  pl.BlockSpec((1, N_HEADS, 8, c), lambda i, d, j: (d, 0, 0, 0)),
                  tok, tok, tok, st],
        out_specs=[pl.BlockSpec((1, 1, c, 256), lambda i, d, j: (i, d, jnp.where(d == 0, j, nc - 1 - j), 0)),
                   st],
        scratch_shapes=[pltpu.VMEM((GROUP, GROUP), F32)],
        compiler_params=_params("arbitrary", "arbitrary", "arbitrary"),
        name="retention",
    )(dl_lane, dl_head, q, k, v, s0)


def _inner_tiles(tiles):
    for g in (8, 7, 6, 5, 4, 3, 2):
        if tiles % g == 0:
            return g
    return 1


def _diff_kernel(qt_ref, k_ref, vt_ref, o_ref, s_scr, *, tiles, nq):
    h = pl.program_id(1)
    inner = _inner_tiles(tiles)
    outer = tiles // inner
    total = nq * outer
    tq = DIFF_TQ
    rowi = lax.broadcasted_iota(jnp.int32, (LANE, 1), 0)
    hoff = (h % 2) * HEAD_DIM
    rowmask = [(rowi >= hoff + mp * DIFF_DIM) & (rowi < hoff + (mp + 1) * DIFF_DIM) for mp in range(2)]

    def step(mp, g_q, g_e, slot, m_e):
        mx = pv = None
        if g_q is not None:
            q0 = pl.multiple_of((g_q // outer) * tq, tq)
            qt = qt_ref[0, :, pl.ds(q0, tq)]
            qpad = jnp.where(rowmask[mp], qt, jnp.zeros_like(qt))
            kc = (g_q % outer) * inner
        if g_e is not None:
            vc = (g_e % outer) * inner
        for t in range(inner):
            sl = slice(t * DIFF_TK, (t + 1) * DIFF_TK)
            if g_q is not None:
                base = pl.multiple_of((kc + t) * DIFF_TK, DIFF_TK)
                s = _dot(k_ref[0, pl.ds(base, DIFF_TK), :], qpad)
                s_scr[mp, 1 - slot, sl, :] = s
                cm = jnp.max(s, axis=0, keepdims=True)
                mx = cm if mx is None else jnp.maximum(mx, cm)
            if g_e is not None:
                p = jnp.exp2(s_scr[mp, slot, sl, :] - m_e).astype(BF16)
                d = _dot(vt_ref[0, vc + t, 0], p)
                pv = d if pv is None else pv + d
        return mx, pv

    def advance(mp, g_q, g_e, slot, state):
        m_e, alpha, acc = state
        mx, pv = step(mp, g_q, g_e, slot, m_e)
        acc = alpha * acc + pv
        o0 = pl.multiple_of((g_e // outer) * tq, tq)
        o_ref[0, 0, mp, :, pl.ds(o0, tq)] = acc
        if g_q is None:
            return m_e, alpha, acc
        m_base = jnp.where(g_q % outer == 0, -jnp.inf, m_e)
        m_next = jnp.maximum(m_base, mx)
        return m_next, jnp.exp2(m_base - m_next), acc

    zeros = jnp.zeros((DIFF_VROWS, tq), F32)
    states = []
    for mp in range(2):
        m_first, _ = step(mp, 0, None, 1, None)
        states.append((m_first, jnp.zeros((1, tq), F32), zeros))
    states = tuple(states)

    def body(i, carry):
        carry = tuple(advance(mp, 2 * i + 1, 2 * i, 0, carry[mp]) for mp in range(2))
        return tuple(advance(mp, 2 * i + 2, 2 * i + 1, 1, carry[mp]) for mp in range(2))

    last = total - 1
    states = lax.fori_loop(0, last // 2, body, states, unroll=min(4, max(last // 2, 1)))
    if last % 2 == 1:
        states = tuple(advance(mp, last, last - 1, 0, states[mp]) for mp in range(2))
    for mp in range(2):
        advance(mp, None, last, last % 2, states[mp])


def _diff_attention(qt, k_all, vt_all):
    b, _, lq = qt.shape
    lk = k_all.shape[1]
    tiles = lk // DIFF_TK
    nq = lq // DIFF_TQ
    return pl.pallas_call(
        functools.partial(_diff_kernel, tiles=tiles, nq=nq),
        out_shape=jax.ShapeDtypeStruct((b, N_HEADS, 2, DIFF_VROWS, lq), F32),
        grid=(b, N_HEADS),
        in_specs=[pl.BlockSpec((1, LANE, lq), lambda i, h: (i, h // 2, 0)),
                  pl.BlockSpec((1, lk, LANE), lambda i, h: (i, 0, h // 2)),
                  pl.BlockSpec((1, tiles, 1, DIFF_VROWS, DIFF_TK), lambda i, h: (i, 0, h, 0, 0))],
        out_specs=pl.BlockSpec((1, 1, 2, DIFF_VROWS, lq), lambda i, h: (i, h, 0, 0, 0)),
        scratch_shapes=[pltpu.VMEM((2, 2, _inner_tiles(tiles) * DIFF_TK, DIFF_TQ), F32)],
        compiler_params=_params("arbitrary", "arbitrary"),
        name="diff_attention",
    )(qt, k_all, vt_all)


def _lru_kernel(x_ref, xp_ref, xn_ref, cw_ref, cb_ref, wa_ref, ba_ref, wx_ref, bx_ref, lam_ref,
                h0_ref, h_ref, carry, *, reverse, tm, nt):
    j = pl.program_id(1)
    tix = (nt - 1 - j) if reverse else j

    @pl.when(j == 0)
    def _():
        carry[...] = jnp.broadcast_to(h0_ref[0], carry.shape)

    x = x_ref[0]
    row = lax.broadcasted_iota(jnp.int32, (tm, 1), 0)
    prev = jnp.where(tix > 0, xp_ref[0][7:8, :], 0.0)
    nxt0 = jnp.where(tix < nt - 1, xn_ref[0][0:1, :], 0.0)
    nxt1 = jnp.where(tix < nt - 1, xn_ref[0][1:2, :], 0.0)
    xm1 = jnp.where(row == 0, prev, pltpu.roll(x, 1, axis=0))
    xp1 = jnp.where(row == tm - 1, nxt0, pltpu.roll(x, tm - 1, axis=0))
    xp2 = jnp.where(row == tm - 1, nxt1, jnp.where(row == tm - 2, nxt0, pltpu.roll(x, tm - 2, axis=0)))
    cw = cw_ref[...]
    u = cw[0:1] * xm1 + cw[1:2] * x + cw[2:3] * xp1 + cw[3:4] * xp2 + cb_ref[...]
    ub = u.astype(BF16)
    r = jax.nn.sigmoid(_dot(ub, wa_ref[0]) + ba_ref[0])
    gi = jax.nn.sigmoid(_dot(ub, wx_ref[0]) + bx_ref[0])
    log_a = -LRU_C * r * _softplus(-lam_ref[0])
    a = jnp.exp(log_a)
    th = jnp.tanh(log_a)
    bv = jnp.sqrt(-2.0 * th / (1.0 - th)) * (gi * u)
    k = 1
    while k < tm:
        if reverse:
            keep = row < tm - k
            shift = tm - k
        else:
            keep = row >= k
            shift = k
        a_sh = jnp.where(keep, pltpu.roll(a, shift, axis=0), 1.0)
        b_sh = jnp.where(keep, pltpu.roll(bv, shift, axis=0), 0.0)
        bv = a * b_sh + bv
        a = a * a_sh
        k *= 2
    hcur = a * carry[0:1, :] + bv
    h_ref[0] = hcur
    last = hcur[0:1, :] if reverse else hcur[tm - 1:tm, :]
    carry[...] = jnp.broadcast_to(last, carry.shape)


def _lru_scan(lx, conv_w, conv_b, wa_bd, ba, wx_bd, bx, lam, h0, reverse):
    b, l, w = lx.shape
    tm = 512 if l % 512 == 0 else 256
    nt = l // tm
    tb = tm // 8
    tidx = (lambda j: nt - 1 - j) if reverse else (lambda j: j)
    vec = pl.BlockSpec((1, w), lambda i, j: (0, 0))
    mat = pl.BlockSpec((1, w, w), lambda i, j: (0, 0, 0))
    return pl.pallas_call(
        functools.partial(_lru_kernel, reverse=reverse, tm=tm, nt=nt),
        out_shape=jax.ShapeDtypeStruct((b, l, w), F32),
        grid=(b, nt),
        in_specs=[pl.BlockSpec((1, tm, w), lambda i, j: (i, tidx(j), 0)),
                  pl.BlockSpec((1, 8, w), lambda i, j: (i, jnp.maximum(tidx(j) * tb - 1, 0), 0)),
                  pl.BlockSpec((1, 8, w), lambda i, j: (i, jnp.minimum((tidx(j) + 1) * tb, l // 8 - 1), 0)),
                  pl.BlockSpec((4, w), lambda i, j: (0, 0)), vec,
                  mat, vec, mat, vec, vec,
                  pl.BlockSpec((1, 1, w), lambda i, j: (i, 0, 0))],
        out_specs=pl.BlockSpec((1, tm, w), lambda i, j: (i, tidx(j), 0)),
        scratch_shapes=[pltpu.VMEM((8, w), F32)],
        compiler_params=_params("arbitrary", "arbitrary"),
        name="lru_scan_rev" if reverse else "lru_scan_fwd",
    )(lx, lx, lx, conv_w, conv_b.reshape(1, w), wa_bd.reshape(1, w, w), ba.reshape(1, w),
      wx_bd.reshape(1, w, w), bx.reshape(1, w), lam.reshape(1, w), h0)


def _outproj_kernel(x_ref, gt_ref, oa_ref, ry_ref, rg_ref, rgn_ref, lam_ref, dgn_ref, draw_ref, hf_ref, hb_ref,
                    ly_ref, bd_ref, w_ref, o_ref, *, lam_init):
    oa = oa_ref[0]
    y = ry_ref[0, 0] + ry_ref[0, 1]
    y2 = y * y
    hi = y2.astype(BF16)
    lo = (y2 - hi.astype(F32)).astype(BF16)
    ms = _dot(hi, bd_ref[...]) + _dot(lo, bd_ref[...])
    ob = y * lax.rsqrt(ms + NORM_EPS) * rgn_ref[...] * _silu(rg_ref[0])
    lv = lam_ref[...]
    lam = (jnp.exp(jnp.sum(lv[0:1] * lv[1:2], axis=1, keepdims=True))
           - jnp.exp(jnp.sum(lv[2:3] * lv[3:4], axis=1, keepdims=True)) + lam_init)
    heads = []
    for hd in range(N_HEADS):
        a0 = draw_ref[0, hd, 0]
        a1 = draw_ref[0, hd, 1]
        oh = (a0[0:HEAD_DIM] / a0[HEAD_DIM:HEAD_DIM + 1]
              - lam * (a1[0:HEAD_DIM] / a1[HEAD_DIM:HEAD_DIM + 1]))
        msd = jnp.mean(oh * oh, axis=0, keepdims=True)
        heads.append(oh * lax.rsqrt(msd + NORM_EPS) * (dgn_ref[hd] * (1.0 - lam_init)))
    od = jnp.concatenate(heads, axis=0).T
    ol = (hf_ref[0] + hb_ref[0]) * jax.nn.gelu(ly_ref[0])
    acc = _dot(oa.astype(BF16), w_ref[0:256, :])
    acc = acc + _dot(ob.astype(BF16), w_ref[256:512, :])
    acc = acc + _dot(od.astype(BF16), w_ref[512:768, :])
    acc = acc + _dot(ol.astype(BF16), w_ref[768:1024, :])
    o_ref[0] = x_ref[0] + gt_ref[0] * acc


def _out_projection(x, gt, oa, ry, rg, ret_gn, lam_vecs, diff_gn, lam_init, draw, hf, hb, ly, w_out):
    b, l, d = x.shape
    tm = 512 if l % 512 == 0 else 256
    row = lambda wdt: pl.BlockSpec((1, tm, wdt), lambda i, j: (i, j, 0))
    hd = jnp.arange(GROUP) // HEAD_DIM
    bd = jnp.where(hd[:, None] == hd[None, :], 1.0 / HEAD_DIM, 0.0).astype(BF16)
    return pl.pallas_call(
        functools.partial(_outproj_kernel, lam_init=lam_init),
        out_shape=jax.ShapeDtypeStruct((b, l, d), F32),
        grid=(b, l // tm),
        in_specs=[row(d), pl.BlockSpec((1, 1, d), lambda i, j: (i, 0, 0)), row(256),
                  pl.BlockSpec((1, 2, tm, 256), lambda i, j: (i, 0, j, 0)), row(256),
                  pl.BlockSpec((1, 256), lambda i, j: (0, 0)),
                  pl.BlockSpec((4, DIFF_DIM), lambda i, j: (0, 0)),
                  pl.BlockSpec((N_HEADS, HEAD_DIM, 1), lambda i, j: (0, 0, 0)),
                  pl.BlockSpec((1, N_HEADS, 2, DIFF_VROWS, tm), lambda i, j: (i, 0, 0, 0, j)),
                  row(256), row(256), row(256),
                  pl.BlockSpec((256, 256), lambda i, j: (0, 0)),
                  pl.BlockSpec((d, d), lambda i, j: (0, 0))],
        out_specs=row(d),
        compiler_params=_params("arbitrary", "arbitrary"),
        name="out_projection",
    )(x, gt, oa, ry, rg, ret_gn.reshape(1, 256), lam_vecs, diff_gn.reshape(N_HEADS, HEAD_DIM, 1), draw,
      hf, hb, ly, bd, w_out)


def _norm_modulate(x_ref, g_ref, sh_ref, sc_ref):
    x = x_ref[0]
    ms = jnp.mean(x * x, axis=-1, keepdims=True)
    y = x * lax.rsqrt(ms + NORM_EPS) * g_ref[...]
    return y * (1.0 + sc_ref[0]) + sh_ref[0]


def _ffn_kernel(x_ref, g_ref, sh_ref, sc_ref, gt_ref, w1_ref, w3_ref, w2_ref, o_ref, hb_scr, acc_scr,
                *, n_f):
    f = pl.program_id(2)

    @pl.when(f == 0)
    def _():
        hb_scr[...] = _norm_modulate(x_ref, g_ref, sh_ref, sc_ref).astype(BF16)
        acc_scr[...] = jnp.zeros_like(acc_scr)

    hb = hb_scr[...]
    a = _silu(_dot(hb, w1_ref[...])) * _dot(hb, w3_ref[...])
    acc_scr[...] += _dot(a.astype(BF16), w2_ref[...])

    @pl.when(f == n_f - 1)
    def _():
        o_ref[0] = x_ref[0] + gt_ref[0] * acc_scr[...]


def _ffn(x, g, sh, sc, gt, w1, w3, w2):
    b, l, d = x.shape
    ff = w1.shape[1]
    tm = 1024 if l % 1024 == 0 else 256
    n_f = 2
    fc = ff // n_f
    vec = pl.BlockSpec((1, 1, d), lambda i, j, f: (i, 0, 0))
    return pl.pallas_call(
        functools.partial(_ffn_kernel, n_f=n_f),
        out_shape=jax.ShapeDtypeStruct((b, l, d), F32),
        grid=(b, l // tm, n_f),
        in_specs=[pl.BlockSpec((1, tm, d), lambda i, j, f: (i, j, 0)),
                  pl.BlockSpec((1, d), lambda i, j, f: (0, 0)), vec, vec, vec,
                  pl.BlockSpec((d, fc), lambda i, j, f: (0, f)),
                  pl.BlockSpec((d, fc), lambda i, j, f: (0, f)),
                  pl.BlockSpec((fc, d), lambda i, j, f: (f, 0))],
        out_specs=pl.BlockSpec((1, tm, d), lambda i, j, f: (i, j, 0)),
        scratch_shapes=[pltpu.VMEM((tm, d), BF16), pltpu.VMEM((tm, d), F32)],
        compiler_params=_params("arbitrary", "arbitrary", "arbitrary"),
        name="dense_ffn",
    )(x, g.reshape(1, d), sh, sc, gt, w1, w3, w2)


MOE_RT = 128


def _moe_kernel(x_ref, g_ref, sh_ref, sc_ref, gt_ref, fg_ref, rt_ref, w1_ref, w3_ref, w2_ref, o_ref,
                hb_scr, acc_scr, gate_scr, pos_scr, gatet_scr, post_scr, x_scr, y_scr, gs_scr, cnt_smem,
                *, n_e, n_f, tm, final_norm):
    e = pl.program_id(2)
    f = pl.program_id(3)
    rt2 = 2 * MOE_RT

    @pl.when((e == 0) & (f == 0))
    def _():
        h = _norm_modulate(x_ref, g_ref, sh_ref, sc_ref)
        hb = h.astype(BF16)
        hb_scr[...] = hb
        acc_scr[...] = jnp.zeros_like(acc_scr)
        rt = rt_ref[...]
        rhi = rt.astype(BF16)
        rlo = (rt - rhi.astype(F32)).astype(BF16)
        hlo = (h - hb.astype(F32)).astype(BF16)
        logits = _dot(hb, rhi) + _dot(hb, rlo) + _dot(hlo, rhi)
        lane = lax.broadcasted_iota(jnp.int32, logits.shape, 1)
        logits = jnp.where(lane < N_EXPERTS, logits, -jnp.inf)
        m1 = jnp.max(logits, axis=1, keepdims=True)
        i1 = jnp.min(jnp.where(logits == m1, lane, LANE), axis=1, keepdims=True)
        rest = jnp.where(lane == i1, -jnp.inf, logits)
        m2 = jnp.max(rest, axis=1, keepdims=True)
        i2 = jnp.min(jnp.where(rest == m2, lane, LANE), axis=1, keepdims=True)
        e2 = jnp.exp(m2 - m1)
        den = 1.0 + e2
        gate = jnp.where(lane == i1, 1.0 / den, 0.0) + jnp.where(lane == i2, e2 / den, 0.0)
        gate_scr[...] = gate
        sel = jnp.where(gate > 0.0, 1.0, 0.0)
        tok_r = lax.broadcasted_iota(jnp.int32, (tm, tm), 0)
        tok_c = lax.broadcasted_iota(jnp.int32, (tm, tm), 1)
        earlier = jnp.where(tok_c < tok_r, 1.0, 0.0).astype(BF16)
        pos = _dot(earlier, sel.astype(BF16))
        pos_scr[...] = pos
        post_scr[...] = pos.T
        gatet_scr[...] = gate.T
        for ee in range(n_e):
            cnt_smem[ee] = jnp.sum(jnp.where(lane == ee, sel, 0.0)).astype(jnp.int32)

    n_tiles = (cnt_smem[e] + (MOE_RT - 1)) // MOE_RT
    n_pairs = (n_tiles + 1) // 2

    @pl.when(f == 0)
    def _():
        sub = lax.broadcasted_iota(jnp.int32, (8, 1), 0)
        pos_e = jnp.sum(jnp.where(sub == e, post_scr[0:8, :], 0.0), axis=0, keepdims=True)
        gate_e = jnp.sum(jnp.where(sub == e, gatet_scr[0:8, :], 0.0), axis=0, keepdims=True)

        def gather(j, carry):
            r0 = pl.multiple_of(j * MOE_RT, MOE_RT)
            slot = (r0 + lax.broadcasted_iota(jnp.int32, (MOE_RT, 1), 0)).astype(F32)
            hit = (pos_e == slot) & (gate_e > 0.0)
            onehot = jnp.where(hit, 1.0, 0.0).astype(BF16)
            x_scr[pl.ds(r0, MOE_RT), :] = _dot(onehot, hb_scr[...]).astype(BF16)
            gs = jnp.sum(jnp.where(hit, gate_e, 0.0), axis=1, keepdims=True)
            gs_scr[pl.ds(r0, MOE_RT), :] = jnp.broadcast_to(gs, (MOE_RT, LANE))
            y_scr[pl.ds(r0, MOE_RT), :] = jnp.zeros((MOE_RT, y_scr.shape[1]), F32)
            return carry

        lax.fori_loop(0, 2 * n_pairs, gather, 0)

    def expert(j, carry):
        r0 = pl.multiple_of(j * MOE_RT, MOE_RT)
        xj = x_scr[pl.ds(r0, MOE_RT), :]
        a = _silu(_dot(xj, w1_ref[0])) * _dot(xj, w3_ref[0])
        y_scr[pl.ds(r0, MOE_RT), :] += _dot(a.astype(BF16), w2_ref[0])
        return carry

    lax.fori_loop(0, n_tiles, expert, 0)

    @pl.when(f == n_f - 1)
    def _():
        lane = lax.broadcasted_iota(jnp.int32, (1, LANE), 1)
        pos_col = jnp.sum(jnp.where(lane == e, pos_scr[...], 0.0), axis=1, keepdims=True)
        gate_col = jnp.sum(jnp.where(lane == e, gate_scr[...], 0.0), axis=1, keepdims=True)

        def scatter(jj, carry):
            r0 = pl.multiple_of(jj * rt2, rt2)
            slot = (r0 + lax.broadcasted_iota(jnp.int32, (1, rt2), 1)).astype(F32)
            onehot = jnp.where((pos_col == slot) & (gate_col > 0.0), 1.0, 0.0).astype(BF16)
            ys = (y_scr[pl.ds(r0, rt2), :] * gs_scr[pl.ds(r0, rt2), 0:1]).astype(BF16)
            acc_scr[...] += _dot(onehot, ys)
            return carry

        lax.fori_loop(0, n_pairs, scatter, 0)

    @pl.when((e == n_e - 1) & (f == n_f - 1))
    def _():
        out = x_ref[0] + gt_ref[0] * acc_scr[...]
        if final_norm:
            ms = jnp.mean(out * out, axis=-1, keepdims=True)
            out = out * lax.rsqrt(ms + NORM_EPS) * fg_ref[...]
        o_ref[0] = out


def _moe_ffn(x, g, sh, sc, gt, final_g, w1, w3, w2, router, final_norm):
    b, l, d = x.shape
    n_e, _, ff = w1.shape
    tm = 1024 if l % 1024 == 0 else 256
    n_f = 4
    fc = ff // n_f
    vec = pl.BlockSpec((1, 1, d), lambda i, j, e, f: (i, 0, 0))
    return pl.pallas_call(
        functools.partial(_moe_kernel, n_e=n_e, n_f=n_f, tm=tm, final_norm=final_norm),
        out_shape=jax.ShapeDtypeStruct((b, l, d), F32),
        grid=(b, l // tm, n_e, n_f),
        in_specs=[pl.BlockSpec((1, tm, d), lambda i, j, e, f: (i, j, 0)),
                  pl.BlockSpec((1, d), lambda i, j, e, f: (0, 0)), vec, vec, vec,
                  pl.BlockSpec((1, d), lambda i, j, e, f: (0, 0)),
                  pl.BlockSpec((d, LANE), lambda i, j, e, f: (0, 0)),
                  pl.BlockSpec((1, d, fc), lambda i, j, e, f: (e, 0, f)),
                  pl.BlockSpec((1, d, fc), lambda i, j, e, f: (e, 0, f)),
                  pl.BlockSpec((1, fc, d), lambda i, j, e, f: (e, f, 0))],
        out_specs=pl.BlockSpec((1, tm, d), lambda i, j, e, f: (i, j, 0)),
        scratch_shapes=[pltpu.VMEM((tm, d), BF16),
                        pltpu.VMEM((tm, d), F32),
                        pltpu.VMEM((tm, LANE), F32),
                        pltpu.VMEM((tm, LANE), F32),
                        pltpu.VMEM((LANE, tm), F32),
                        pltpu.VMEM((LANE, tm), F32),
                        pltpu.VMEM((tm, d), BF16),
                        pltpu.VMEM((tm, d), F32),
                        pltpu.VMEM((tm, LANE), F32),
                        pltpu.SMEM((n_e,), jnp.int32)],
        compiler_params=_params("arbitrary", "arbitrary", "arbitrary", "arbitrary"),
        name="moe_ffn",
    )(x, g.reshape(1, d), sh, sc, gt, final_g.reshape(1, d), router, w1, w3, w2)


def _rope_tables(n):
    rows = n // GRID_W
    row = jnp.repeat(jnp.arange(rows, dtype=F32), GRID_W)
    col = jnp.tile(jnp.arange(GRID_W, dtype=F32), rows)

    def axial(dim):
        nf = dim // 4
        inv = ROPE_BASE ** (-jnp.arange(nf, dtype=F32) / nf)
        return jnp.concatenate([row[:, None] * inv, col[:, None] * inv], axis=-1)

    def line(dim):
        nf = dim // 2
        inv = ROPE_BASE ** (-jnp.arange(nf, dtype=F32) / nf)
        return jnp.arange(n, dtype=F32)[:, None] * inv

    def expand(ang, dim):
        c = jnp.cos(ang)
        s = jnp.sin(ang)
        c = jnp.tile(jnp.concatenate([c, c], axis=-1), (1, LANE // dim))
        s = jnp.tile(jnp.concatenate([-s, s], axis=-1), (1, LANE // dim))
        return c, s

    ca, sa = expand(axial(HEAD_DIM), HEAD_DIM)
    cr, sr = expand(line(HEAD_DIM), HEAD_DIM)
    cd, sd = expand(axial(DIFF_DIM), DIFF_DIM)
    return ca, sa, cr, sr, cd, sd


_ATTN_HEAD_ORDER = (0, 2, 1, 3)


def _attn_perm():
    return jnp.concatenate([jnp.arange(HEAD_DIM) + HEAD_DIM * h for h in _ATTN_HEAD_ORDER])


def _block_diag(w):
    nb, blk, _ = w.shape
    out = jnp.zeros((nb * blk, nb * blk), w.dtype)
    for i in range(nb):
        out = out.at[i * blk:(i + 1) * blk, i * blk:(i + 1) * blk].set(w[i])
    return out


def kernel(x, c, ctx, c_ctx, w_ada, b_ada, g_norm1, g_norm2, w_in, w_out, attn_sink, ret_decay_logit,
           ret_gn, diff_lambda, diff_gn, conv_w, conv_b, lru_wa, lru_ba, lru_wx, lru_bx, lru_lambda,
           ffn_w1, ffn_w3, ffn_w2, moe_router, moe_w1, moe_w3, moe_w2, final_norm):
    b, n, d = x.shape
    m = ctx.shape[1]
    tables = _rope_tables(n)
    perm = _attn_perm()

    cc = jnp.concatenate([c, c_ctx[None, :], jnp.zeros((8 - b - 1, d), F32)], axis=0)
    mods = _mod_vectors(cc, w_ada, b_ada)

    xc = ctx
    for layer in range(DEPTH):
        need_ctx = layer < DEPTH - 1
        lam_init = 0.8 - 0.6 * math.exp(-0.3 * layer)
        mod = mods[layer]
        sh1, sc1, gt1, sh2, sc2, gt2 = [mod[:b, None, i * d:(i + 1) * d] for i in range(6)]
        sh1c, sc1c, gt1c, sh2c, sc2c, gt2c = [
            jnp.broadcast_to(mod[b:b + 1, None, i * d:(i + 1) * d], (b, 1, d)) for i in range(6)]

        wi = w_in[layer]
        wi = jnp.concatenate([wi[:, :256][:, perm], wi[:, 256:]], axis=1).astype(BF16)
        wo = w_out[layer]
        wo = jnp.concatenate([wo[:256][perm], wo[256:]], axis=0).astype(BF16)
        sink_perm = attn_sink[layer][jnp.array(_ATTN_HEAD_ORDER)]

        (qa, ka, va, rq, rk, rv, rg, dqt, dk, dvt, lx, ly) = _in_projection(
            x, g_norm1[layer], sh1, sc1, wi, tables)
        (qac, kac, vac, rqc, rkc, rvc, rgc, dqtc, dkc, dvtc, lxc, lyc) = _in_projection(
            xc, g_norm1[layer], sh1c, sc1c, wi, None)

        oa = _swa_attention(sink_perm, qa, ka, va, kac, vac)
        dl = ret_decay_logit[layer]
        dl_lane = jnp.repeat(dl, HEAD_DIM, axis=1).reshape(2, 1, GROUP)
        dl_head = jnp.broadcast_to(dl[:, :, None, None], (2, N_HEADS, 8, RET_CHUNK))
        ryc, s_ctx = _retention(dl_lane, dl_head, rqc, rkc, rvc, jnp.zeros((b, 2, GROUP, GROUP), F32))
        ry, _ = _retention(dl_lane, dl_head, rq, rk, rv, s_ctx)
        k_all = jnp.concatenate([dk, dkc], axis=1)
        vt_all = jnp.concatenate([dvt, dvtc], axis=1)
        draw = _diff_attention(dqt, k_all, vt_all)
        assemble = functools.partial(_out_projection, ret_gn=ret_gn[layer], lam_vecs=diff_lambda[layer],
                                     diff_gn=diff_gn[layer], lam_init=lam_init, w_out=wo)
        lru = []
        lru_c = []
        for direction in range(2):
            prm = (conv_w[layer], conv_b[layer],
                   _block_diag(lru_wa[layer, direction]).astype(BF16), lru_ba[layer, direction],
                   _block_diag(lru_wx[layer, direction]).astype(BF16), lru_bx[layer, direction],
                   lru_lambda[layer, direction])
            rev = direction == 1
            hc = _lru_scan(lxc, *prm, jnp.zeros((b, 1, GROUP), F32), rev)
            h0 = hc[:, 0:1, :] if rev else hc[:, m - 1:m, :]
            lru.append(_lru_scan(lx, *prm, h0, rev))
            lru_c.append(hc)

        x = assemble(x, gt1, oa, ry, rg, draw=draw, hf=lru[0], hb=lru[1], ly=ly)
        j = layer // 2
        if layer % 2 == 0:
            ffn = functools.partial(_ffn, w1=ffn_w1[j].astype(BF16), w3=ffn_w3[j].astype(BF16),
                                    w2=ffn_w2[j].astype(BF16))
            x = ffn(x, g_norm2[layer], sh2, sc2, gt2)
        else:
            moe = functools.partial(_moe_ffn, final_g=final_norm, w1=moe_w1[j].astype(BF16),
                                    w3=moe_w3[j].astype(BF16), w2=moe_w2[j].astype(BF16),
                                    router=jnp.pad(moe_router[j], ((0, 0), (0, LANE - N_EXPERTS))))
            x = moe(x, g_norm2[layer], sh2, sc2, gt2, final_norm=layer == DEPTH - 1)
            ffn = functools.partial(moe, final_norm=False)

        if need_ctx:
            oac = _ctx_attention(sink_perm, qac, kac, vac)
            drawc = _diff_attention(dqtc, dkc, dvtc)
            xc = assemble(xc, gt1c, oac, ryc, rgc, draw=drawc, hf=lru_c[0], hb=lru_c[1], ly=lyc)
            xc = ffn(xc, g_norm2[layer], sh2c, sc2c, gt2c)
    assert DEPTH % 2 == 0, "the closing RMSNorm is fused into the last (routed) layer"
    return x
```

```python
import functools
import math

import jax
import jax.numpy as jnp
from jax import lax
from jax.experimental import pallas as pl
from jax.experimental.pallas import tpu as pltpu

F32 = jnp.float32
BF16 = jnp.bfloat16

D_MODEL = 1024
DEPTH = 2
GRID_W = 64
HEAD_DIM = 64
GROUP = D_MODEL // 4
N_HEADS = GROUP // HEAD_DIM
DIFF_DIM = HEAD_DIM // 2
WINDOW = 128
NORM_EPS = 1e-6
ROPE_BASE = 10000.0
LRU_BLOCKS = 4
LRU_C = 8.0
N_EXPERTS = 8
LANE = 128
LOG2E = 1.4426950408889634
DIFF_TQ = 256
DIFF_TK = 256
DIFF_VROWS = HEAD_DIM + 16

OFF_AQ, OFF_AK, OFF_AV = 0, 256, 384
OFF_RQ, OFF_RK, OFF_RV, OFF_RG = 512, 768, 1024, 1280
OFF_DQ, OFF_DK, OFF_DV = 1536, 1792, 2048
OFF_LX, OFF_LY = 2304, 2560
D_IN = 2816

VMEM_LIMIT = 56 * 1024 * 1024


def _params(*sem):
    return pltpu.CompilerParams(dimension_semantics=sem, vmem_limit_bytes=VMEM_LIMIT)


def _silu(x):
    return x * jax.nn.sigmoid(x)


def _softplus(x):
    return jnp.maximum(x, 0.0) + jnp.log(1.0 + jnp.exp(-jnp.abs(x)))


def _dot(a, b):
    return jnp.dot(a, b, preferred_element_type=F32)


def _dot_nt(a, b):
    return lax.dot_general(a, b, (((1,), (1,)), ((), ())), preferred_element_type=F32)


def _mod_kernel(c_ref, w_ref, b_ref, o_ref):
    s = _silu(c_ref[...])
    o_ref[0] = _dot(s.astype(BF16), w_ref[0].astype(BF16)) + b_ref[0]


def _mod_vectors(cc, w_ada, b_ada):
    depth, d, n6 = w_ada.shape
    tn = 1536
    return pl.pallas_call(
        _mod_kernel,
        out_shape=jax.ShapeDtypeStruct((depth, 8, n6), F32),
        grid=(depth, n6 // tn),
        in_specs=[pl.BlockSpec((8, d), lambda l, j: (0, 0)),
                  pl.BlockSpec((1, d, tn), lambda l, j: (l, 0, j)),
                  pl.BlockSpec((1, 1, tn), lambda l, j: (l, 0, j))],
        out_specs=pl.BlockSpec((1, 8, tn), lambda l, j: (l, 0, j)),
        compiler_params=_params("arbitrary", "arbitrary"),
        name="mod_vectors",
    )(cc, w_ada, b_ada.reshape(depth, 1, n6))


def _rope_tile(x, cos, sin_signed, d):
    lane = lax.broadcasted_iota(jnp.int32, x.shape, 1)
    first = (lane & (d - 1)) < (d // 2)
    partner = jnp.where(first, pltpu.roll(x, LANE - d // 2, axis=1), pltpu.roll(x, d // 2, axis=1))
    return x * cos + partner * sin_signed


def _inproj_kernel(*refs, rope, tm):
    if rope:
        (x_ref, g_ref, sh_ref, sc_ref, w_ref, ca_ref, sa_ref, cr_ref, sr_ref, cd_ref, sd_ref,
         qa_ref, ka_ref, va_ref, rq_ref, rk_ref, rv_ref, rg_ref,
         dqt_ref, dk_ref, dvt_ref, lx_ref, ly_ref) = refs
    else:
        (x_ref, g_ref, sh_ref, sc_ref, w_ref,
         qa_ref, ka_ref, va_ref, rq_ref, rk_ref, rv_ref, rg_ref,
         dqt_ref, dk_ref, dvt_ref, lx_ref, ly_ref) = refs
    x = x_ref[0]
    ms = jnp.mean(x * x, axis=-1, keepdims=True)
    y = x * lax.rsqrt(ms + NORM_EPS) * g_ref[...]
    h = y * (1.0 + sc_ref[0]) + sh_ref[0]
    hb = h.astype(BF16)

    def proj(off, width):
        return _dot(hb, w_ref[:, off:off + width])

    def roped(p, c_ref, s_ref, d, scale):
        tiles = []
        for t in range(p.shape[1] // LANE):
            pt = p[:, t * LANE:(t + 1) * LANE]
            if rope:
                pt = _rope_tile(pt, c_ref[...], s_ref[...], d)
            tiles.append(pt if scale == 1.0 else pt * scale)
        return tiles[0] if len(tiles) == 1 else jnp.concatenate(tiles, axis=1)

    ca = sa = cr = sr = cd = sd = None
    if rope:
        ca, sa, cr, sr, cd, sd = ca_ref, sa_ref, cr_ref, sr_ref, cd_ref, sd_ref
    qa_ref[0] = roped(proj(OFF_AQ, 256), ca, sa, HEAD_DIM, HEAD_DIM ** -0.5).astype(BF16)
    ka_ref[0] = roped(proj(OFF_AK, 128), ca, sa, HEAD_DIM, 1.0).astype(BF16)
    va_ref[0] = proj(OFF_AV, 128).astype(BF16)
    rq_ref[0] = roped(proj(OFF_RQ, 256), cr, sr, HEAD_DIM, 1.0).astype(BF16)
    rk_ref[0] = roped(proj(OFF_RK, 256), cr, sr, HEAD_DIM, HEAD_DIM ** -0.5).astype(BF16)
    rv_ref[0] = proj(OFF_RV, 256).astype(BF16)
    rg_ref[0] = proj(OFF_RG, 256)
    dq = roped(proj(OFF_DQ, 256), cd, sd, DIFF_DIM, DIFF_DIM ** -0.5 * LOG2E)
    dqt_ref[0] = dq.T.astype(BF16)
    dk_ref[0] = roped(proj(OFF_DK, 256), cd, sd, DIFF_DIM, 1.0).astype(BF16)
    dvt = proj(OFF_DV, 256).T.astype(BF16)
    ones_rows = (lax.broadcasted_iota(jnp.int32, (DIFF_VROWS - HEAD_DIM, DIFF_TK), 0) == 0).astype(BF16)
    for j in range(tm // DIFF_TK):
        for hd in range(N_HEADS):
            dvt_ref[0, j, hd, 0:HEAD_DIM, :] = dvt[hd * HEAD_DIM:(hd + 1) * HEAD_DIM, j * DIFF_TK:(j + 1) * DIFF_TK]
            dvt_ref[0, j, hd, HEAD_DIM:DIFF_VROWS, :] = ones_rows
    lx_ref[0] = proj(OFF_LX, 256)
    ly_ref[0] = proj(OFF_LY, 256)


def _in_projection(x, g, sh, sc, w_in, tables):
    b, l, d = x.shape
    rope = tables is not None
    tm = 512 if l % 512 == 0 else 256
    nt = l // tm
    row = lambda shape: pl.BlockSpec(shape, lambda i, j: (i, j, 0))
    vec = pl.BlockSpec((1, 1, d), lambda i, j: (i, 0, 0))
    in_specs = [row((1, tm, d)), pl.BlockSpec((1, d), lambda i, j: (0, 0)), vec, vec,
                pl.BlockSpec((d, D_IN), lambda i, j: (0, 0))]
    args = [x, g.reshape(1, d), sh, sc, w_in]
    if rope:
        in_specs += [pl.BlockSpec((tm, LANE), lambda i, j: (j, 0))] * 6
        args += list(tables)
    sds = jax.ShapeDtypeStruct
    out_shape = [sds((b, l, 256), BF16), sds((b, l, 128), BF16), sds((b, l, 128), BF16),
                 sds((b, l, 256), BF16), sds((b, l, 256), BF16), sds((b, l, 256), BF16),
                 sds((b, l, 256), F32),
                 sds((b, 256, l), BF16), sds((b, l, 256), BF16),
                 sds((b, l // DIFF_TK, N_HEADS, DIFF_VROWS, DIFF_TK), BF16),
                 sds((b, l, 256), F32), sds((b, l, 256), F32)]
    out_specs = [row((1, tm, 256)), row((1, tm, 128)), row((1, tm, 128)),
                 row((1, tm, 256)), row((1, tm, 256)), row((1, tm, 256)), row((1, tm, 256)),
                 pl.BlockSpec((1, 256, tm), lambda i, j: (i, 0, j)), row((1, tm, 256)),
                 pl.BlockSpec((1, tm // DIFF_TK, N_HEADS, DIFF_VROWS, DIFF_TK), lambda i, j: (i, j, 0, 0, 0)),
                 row((1, tm, 256)), row((1, tm, 256))]
    return pl.pallas_call(
        functools.partial(_inproj_kernel, rope=rope, tm=tm),
        out_shape=out_shape, grid=(b, nt), in_specs=in_specs, out_specs=out_specs,
        compiler_params=_params("arbitrary", "arbitrary"),
        name="in_projection_rope" if rope else "in_projection_ctx",
    )(*args)


def _swa_kernel(*refs, window, tq, n):
    if window:
        sink_ref, q_ref, k_ref, v_ref, kc_ref, vc_ref, o_ref = refs
    else:
        sink_ref, q_ref, kc_ref, vc_ref, o_ref = refs
    kc = kc_ref[0]
    vc = vc_ref[0]
    m = kc.shape[0]
    if window:
        i = pl.program_id(1)
        w = tq + 2 * WINDOW
        t0 = i * tq
        start = pl.multiple_of(jnp.clip(t0 - WINDOW, 0, n - w), WINDOW)
        kcat = jnp.concatenate([k_ref[0, pl.ds(start, w), :], kc], axis=0)
        vcat = jnp.concatenate([v_ref[0, pl.ds(start, w), :], vc], axis=0)
        qpos = t0 + lax.broadcasted_iota(jnp.int32, (tq, w + m), 0)
        col = lax.broadcasted_iota(jnp.int32, (tq, w + m), 1)
        valid = (jnp.abs(qpos - (start + col)) <= WINDOW) | (col >= w)
    else:
        kcat, vcat, valid = kc, vc, None
    q = q_ref[0]
    lane = lax.broadcasted_iota(jnp.int32, (1, LANE), 1)
    out_tiles = []
    for t in range(2):
        qt = q[:, t * LANE:(t + 1) * LANE]
        out_t = None
        for half in range(2):
            in_half = (lane >= half * HEAD_DIM) & (lane < (half + 1) * HEAD_DIM)
            sink = sink_ref[2 * t + half]
            qm = jnp.where(in_half, qt, jnp.zeros_like(qt))
            s = _dot_nt(qm, kcat)
            if valid is not None:
                s = jnp.where(valid, s, -jnp.inf)
            mx = jnp.maximum(jnp.max(s, axis=1, keepdims=True), sink)
            e = jnp.exp(s - mx)
            den = jnp.sum(e, axis=1, keepdims=True) + jnp.exp(sink - mx)
            o = _dot(e.astype(BF16), vcat) / den
            out_t = o if out_t is None else jnp.where(in_half, o, out_t)
        out_tiles.append(out_t)
    o_ref[0] = jnp.concatenate(out_tiles, axis=1)


def _swa_attention(sink_perm, q, k, v, kc, vc):
    b, n, _ = q.shape
    m = kc.shape[1]
    tq = 256
    smem = pl.BlockSpec(memory_space=pltpu.SMEM)
    ctx = pl.BlockSpec((1, m, 128), lambda i, j: (i, 0, 0))
    in_specs = [smem, pl.BlockSpec((1, tq, 256), lambda i, j: (i, j, 0)),
                pl.BlockSpec((1, n, 128), lambda i, j: (i, 0, 0)),
                pl.BlockSpec((1, n, 128), lambda i, j: (i, 0, 0)), ctx, ctx]
    return pl.pallas_call(
        functools.partial(_swa_kernel, window=True, tq=tq, n=n),
        out_shape=jax.ShapeDtypeStruct((b, n, 256), F32),
        grid=(b, n // tq), in_specs=in_specs,
        out_specs=pl.BlockSpec((1, tq, 256), lambda i, j: (i, j, 0)),
        compiler_params=_params("arbitrary", "arbitrary"),
        name="swa_attention",
    )(sink_perm, q, k, v, kc, vc)


def _ctx_attention(sink_perm, qc, kc, vc):
    b, m, _ = qc.shape
    smem = pl.BlockSpec(memory_space=pltpu.SMEM)
    ctx = pl.BlockSpec((1, m, 128), lambda i: (i, 0, 0))
    return pl.pallas_call(
        functools.partial(_swa_kernel, window=False, tq=m, n=m),
        out_shape=jax.ShapeDtypeStruct((b, m, 256), F32),
        grid=(b,), in_specs=[smem, pl.BlockSpec((1, m, 256), lambda i: (i, 0, 0)), ctx, ctx],
        out_specs=pl.BlockSpec((1, m, 256), lambda i: (i, 0, 0)),
        compiler_params=_params("arbitrary"),
        name="ctx_attention",
    )(sink_perm, qc, kc, vc)


RET_CHUNK = 128


def _ret_kernel(dl_lane_ref, dl_head_ref, q_ref, k_ref, v_ref, s0_ref, y_ref, sfin_ref,
                s_scr, dm_scr, xi_scr, zeta_scr, *, ns, group):
    c = RET_CHUNK
    d = pl.program_id(1)
    si = pl.program_id(2)
    fwd = d == 0
    lane_head = lax.broadcasted_iota(jnp.int32, (1, GROUP), 1) // HEAD_DIM

    @pl.when(si == 0)
    def _():
        s_scr[...] = s0_ref[0, 0]
        lgl = -_softplus(-dl_lane_ref[0])
        row = lax.broadcasted_iota(jnp.int32, (c, 1), 0)
        xi_scr[...] = jnp.exp(lgl * jnp.where(fwd, row + 1, c - row).astype(F32))
        zeta_scr[...] = jnp.exp(lgl * jnp.where(fwd, c - 1 - row, row).astype(F32))
        ni = lax.broadcasted_iota(jnp.int32, (c, c), 0)
        mi = lax.broadcasted_iota(jnp.int32, (c, c), 1)
        dd = jnp.where(fwd, ni - mi, mi - ni)
        ddf = jnp.maximum(dd, 0).astype(F32)
        for h in range(N_HEADS):
            lgh = -_softplus(-dl_head_ref[0, h][0:1, :])
            dm_scr[h] = jnp.where(dd >= 0, jnp.exp(lgh * ddf), 0.0)

    dec = jnp.exp(-_softplus(-dl_lane_ref[0]) * float(c))
    rh = lax.broadcasted_iota(jnp.int32, (GROUP, GROUP), 0) // HEAD_DIM
    ch = lax.broadcasted_iota(jnp.int32, (GROUP, GROUP), 1) // HEAD_DIM
    same_head = rh == ch

    def chunk(g, carry):
        r0 = pl.multiple_of(jnp.where(fwd, g, group - 1 - g) * c, c)
        q = q_ref[0, pl.ds(r0, c), :]
        k = k_ref[0, pl.ds(r0, c), :]
        v = v_ref[0, pl.ds(r0, c), :]
        cross = _dot(q, s_scr[...].astype(BF16)) * xi_scr[...]
        inner = jnp.zeros((c, GROUP), F32)
        for h in range(N_HEADS):
            qh = jnp.where(lane_head == h, q, jnp.zeros_like(q))
            s = _dot_nt(qh, k) * dm_scr[h]
            ih = _dot(s.astype(BF16), v)
            inner = jnp.where(lane_head == h, ih, inner)
        y_ref[0, 0, pl.ds(r0, c), :] = inner + cross
        kz = (k.astype(F32) * zeta_scr[...]).T.astype(BF16)
        u = _dot(kz, v)
        s_scr[...] = jnp.where(same_head, dec * s_scr[...] + u, 0.0)
        return carry

    lax.fori_loop(0, group, chunk, 0, unroll=4 if group % 4 == 0 else 2)

    @pl.when(si == ns - 1)
    def _():
        sfin_ref[0, 0] = s_scr[...]


def _retention(dl_lane, dl_head, q, k, v, s0):
    b, l, _ = q.shape
    c = RET_CHUNK
    group = 8 if l % (8 * c) == 0 else 2
    ns = l // (group * c)
    sidx = lambda i, d, j: (i, jnp.where(d == 0, j, ns - 1 - j), 0)
    tok = pl.BlockSpec((1, group * c, 256), sidx)
    st = pl.BlockSpec((1, 1, 256, 256), lambda i, d, j: (i, d, 0, 0))
    return pl.pallas_call(
        functools.partial(_ret_kernel, ns=ns, group=group),
        out_shape=[jax.ShapeDtypeStruct((b, 2, l, 256), F32),
                   jax.ShapeDtypeStruct((b, 2, 256, 256), F32)],
        grid=(b, 2, ns),
        in_specs=[pl.BlockSpec((1, 1, 256), lambda i, d, j: (d, 0, 0)),
                  pl.BlockSpec((1, N_HEADS, 8, c), lambda i, d, j: (d, 0, 0, 0)),
                  tok, tok, tok, st],
        out_specs=[pl.BlockSpec((1, 1, group * c, 256),
                                lambda i, d, j: (i, d, jnp.where(d == 0, j, ns - 1 - j), 0)),
                   st],
        scratch_shapes=[pltpu.VMEM((GROUP, GROUP), F32),
                        pltpu.VMEM((N_HEADS, c, c), F32),
                        pltpu.VMEM((c, GROUP), F32),
                        pltpu.VMEM((c, GROUP), F32)],
        compiler_params=_params("arbitrary", "arbitrary", "arbitrary"),
        name="retention",
    )(dl_lane, dl_head, q, k, v, s0)


def _inner_tiles(tiles):
    for g in (8, 7, 6, 5, 4, 3, 2):
        if tiles % g == 0:
            return g
    return 1


def _diff_kernel(qt_ref, k_ref, vt_ref, o_ref, s_scr, *, tiles, nq):
    h = pl.program_id(1)
    inner = _inner_tiles(tiles)
    outer = tiles // inner
    total = nq * outer
    tq = DIFF_TQ
    rowi = lax.broadcasted_iota(jnp.int32, (LANE, 1), 0)
    hoff = (h % 2) * HEAD_DIM
    rowmask = [(rowi >= hoff + mp * DIFF_DIM) & (rowi < hoff + (mp + 1) * DIFF_DIM) for mp in range(2)]

    def step(mp, g_q, g_e, slot, m_e):
        mx = pv = None
        if g_q is not None:
            q0 = pl.multiple_of((g_q // outer) * tq, tq)
            qt = qt_ref[0, :, pl.ds(q0, tq)]
            qpad = jnp.where(rowmask[mp], qt, jnp.zeros_like(qt))
            kc = (g_q % outer) * inner
        if g_e is not None:
            vc = (g_e % outer) * inner
        for t in range(inner):
            sl = slice(t * DIFF_TK, (t + 1) * DIFF_TK)
            if g_q is not None:
                base = pl.multiple_of((kc + t) * DIFF_TK, DIFF_TK)
                s = _dot(k_ref[0, pl.ds(base, DIFF_TK), :], qpad)
                s_scr[mp, 1 - slot, sl, :] = s
                cm = jnp.max(s, axis=0, keepdims=True)
                mx = cm if mx is None else jnp.maximum(mx, cm)
            if g_e is not None:
                p = jnp.exp2(s_scr[mp, slot, sl, :] - m_e).astype(BF16)
                d = _dot(vt_ref[0, vc + t, 0], p)
                pv = d if pv is None else pv + d
        return mx, pv

    def advance(mp, g_q, g_e, slot, state):
        m_e, alpha, acc = state
        mx, pv = step(mp, g_q, g_e, slot, m_e)
        acc = alpha * acc + pv
        o0 = pl.multiple_of((g_e // outer) * tq, tq)
        o_ref[0, 0, mp, :, pl.ds(o0, tq)] = acc
        if g_q is None:
            return m_e, alpha, acc
        m_base = jnp.where(g_q % outer == 0, -jnp.inf, m_e)
        m_next = jnp.maximum(m_base, mx)
        return m_next, jnp.exp2(m_base - m_next), acc

    zeros = jnp.zeros((DIFF_VROWS, tq), F32)
    states = []
    for mp in range(2):
        m_first, _ = step(mp, 0, None, 1, None)
        states.append((m_first, jnp.zeros((1, tq), F32), zeros))
    states = tuple(states)

    def body(i, carry):
        carry = tuple(advance(mp, 2 * i + 1, 2 * i, 0, carry[mp]) for mp in range(2))
        return tuple(advance(mp, 2 * i + 2, 2 * i + 1, 1, carry[mp]) for mp in range(2))

    last = total - 1
    states = lax.fori_loop(0, last // 2, body, states, unroll=min(4, max(last // 2, 1)))
    if last % 2 == 1:
        states = tuple(advance(mp, last, last - 1, 0, states[mp]) for mp in range(2))
    for mp in range(2):
        advance(mp, None, last, last % 2, states[mp])


def _diff_attention(qt, k_all, vt_all):
    b, _, lq = qt.shape
    lk = k_all.shape[1]
    tiles = lk // DIFF_TK
    nq = lq // DIFF_TQ
    return pl.pallas_call(
        functools.partial(_diff_kernel, tiles=tiles, nq=nq),
        out_shape=jax.ShapeDtypeStruct((b, N_HEADS, 2, DIFF_VROWS, lq), F32),
        grid=(b, N_HEADS),
        in_specs=[pl.BlockSpec((1, LANE, lq), lambda i, h: (i, h // 2, 0)),
                  pl.BlockSpec((1, lk, LANE), lambda i, h: (i, 0, h // 2)),
                  pl.BlockSpec((1, tiles, 1, DIFF_VROWS, DIFF_TK), lambda i, h: (i, 0, h, 0, 0))],
        out_specs=pl.BlockSpec((1, 1, 2, DIFF_VROWS, lq), lambda i, h: (i, h, 0, 0, 0)),
        scratch_shapes=[pltpu.VMEM((2, 2, _inner_tiles(tiles) * DIFF_TK, DIFF_TQ), F32)],
        compiler_params=_params("arbitrary", "arbitrary"),
        name="diff_attention",
    )(qt, k_all, vt_all)


def _lru_kernel(x_ref, xp_ref, xn_ref, cw_ref, cb_ref, wa_ref, ba_ref, wx_ref, bx_ref, lam_ref,
                h0_ref, h_ref, carry, *, reverse, tm, nt):
    j = pl.program_id(1)
    tix = (nt - 1 - j) if reverse else j

    @pl.when(j == 0)
    def _():
        carry[...] = jnp.broadcast_to(h0_ref[0], carry.shape)

    x = x_ref[0]
    row = lax.broadcasted_iota(jnp.int32, (tm, 1), 0)
    prev = jnp.where(tix > 0, xp_ref[0][7:8, :], 0.0)
    nxt0 = jnp.where(tix < nt - 1, xn_ref[0][0:1, :], 0.0)
    nxt1 = jnp.where(tix < nt - 1, xn_ref[0][1:2, :], 0.0)
    xm1 = jnp.where(row == 0, prev, pltpu.roll(x, 1, axis=0))
    xp1 = jnp.where(row == tm - 1, nxt0, pltpu.roll(x, tm - 1, axis=0))
    xp2 = jnp.where(row == tm - 1, nxt1, jnp.where(row == tm - 2, nxt0, pltpu.roll(x, tm - 2, axis=0)))
    cw = cw_ref[...]
    u = cw[0:1] * xm1 + cw[1:2] * x + cw[2:3] * xp1 + cw[3:4] * xp2 + cb_ref[...]
    ub = u.astype(BF16)
    r = jax.nn.sigmoid(_dot(ub, wa_ref[0]) + ba_ref[0])
    gi = jax.nn.sigmoid(_dot(ub, wx_ref[0]) + bx_ref[0])
    log_a = -LRU_C * r * _softplus(-lam_ref[0])
    a = jnp.exp(log_a)
    th = jnp.tanh(log_a)
    bv = jnp.sqrt(-2.0 * th / (1.0 - th)) * (gi * u)
    k = 1
    while k < tm:
        if reverse:
            keep = row < tm - k
            shift = tm - k
        else:
            keep = row >= k
            shift = k
        a_sh = jnp.where(keep, pltpu.roll(a, shift, axis=0), 1.0)
        b_sh = jnp.where(keep, pltpu.roll(bv, shift, axis=0), 0.0)
        bv = a * b_sh + bv
        a = a * a_sh
        k *= 2
    hcur = a * carry[0:1, :] + bv
    h_ref[0] = hcur
    last = hcur[0:1, :] if reverse else hcur[tm - 1:tm, :]
    carry[...] = jnp.broadcast_to(last, carry.shape)


def _lru_scan(lx, conv_w, conv_b, wa_bd, ba, wx_bd, bx, lam, h0, reverse):
    b, l, w = lx.shape
    tm = 512 if l % 512 == 0 else 256
    nt = l // tm
    tb = tm // 8
    tidx = (lambda j: nt - 1 - j) if reverse else (lambda j: j)
    vec = pl.BlockSpec((1, w), lambda i, j: (0, 0))
    mat = pl.BlockSpec((1, w, w), lambda i, j: (0, 0, 0))
    return pl.pallas_call(
        functools.partial(_lru_kernel, reverse=reverse, tm=tm, nt=nt),
        out_shape=jax.ShapeDtypeStruct((b, l, w), F32),
        grid=(b, nt),
        in_specs=[pl.BlockSpec((1, tm, w), lambda i, j: (i, tidx(j), 0)),
                  pl.BlockSpec((1, 8, w), lambda i, j: (i, jnp.maximum(tidx(j) * tb - 1, 0), 0)),
                  pl.BlockSpec((1, 8, w), lambda i, j: (i, jnp.minimum((tidx(j) + 1) * tb, l // 8 - 1), 0)),
                  pl.BlockSpec((4, w), lambda i, j: (0, 0)), vec,
                  mat, vec, mat, vec, vec,
                  pl.BlockSpec((1, 1, w), lambda i, j: (i, 0, 0))],
        out_specs=pl.BlockSpec((1, tm, w), lambda i, j: (i, tidx(j), 0)),
        scratch_shapes=[pltpu.VMEM((8, w), F32)],
        compiler_params=_params("arbitrary", "arbitrary"),
        name="lru_scan_rev" if reverse else "lru_scan_fwd",
    )(lx, lx, lx, conv_w, conv_b.reshape(1, w), wa_bd.reshape(1, w, w), ba.reshape(1, w),
      wx_bd.reshape(1, w, w), bx.reshape(1, w), lam.reshape(1, w), h0)


def _outproj_kernel(x_ref, gt_ref, oa_ref, ry_ref, rg_ref, rgn_ref, lam_ref, dgn_ref, draw_ref, hf_ref, hb_ref,
                    ly_ref, bd_ref, w_ref, o_ref, *, lam_init):
    oa = oa_ref[0]
    y = ry_ref[0, 0] + ry_ref[0, 1]
    y2 = y * y
    hi = y2.astype(BF16)
    lo = (y2 - hi.astype(F32)).astype(BF16)
    ms = _dot(hi, bd_ref[...]) + _dot(lo, bd_ref[...])
    ob = y * lax.rsqrt(ms + NORM_EPS) * rgn_ref[...] * _silu(rg_ref[0])
    lv = lam_ref[...]
    lam = (jnp.exp(jnp.sum(lv[0:1] * lv[1:2], axis=1, keepdims=True))
           - jnp.exp(jnp.sum(lv[2:3] * lv[3:4], axis=1, keepdims=True)) + lam_init)
    heads = []
    for hd in range(N_HEADS):
        a0 = draw_ref[0, hd, 0]
        a1 = draw_ref[0, hd, 1]
        oh = (a0[0:HEAD_DIM] / a0[HEAD_DIM:HEAD_DIM + 1]
              - lam * (a1[0:HEAD_DIM] / a1[HEAD_DIM:HEAD_DIM + 1]))
        msd = jnp.mean(oh * oh, axis=0, keepdims=True)
        heads.append(oh * lax.rsqrt(msd + NORM_EPS) * (dgn_ref[hd] * (1.0 - lam_init)))
    od = jnp.concatenate(heads, axis=0).T
    ol = (hf_ref[0] + hb_ref[0]) * jax.nn.gelu(ly_ref[0])
    acc = _dot(oa.astype(BF16), w_ref[0:256, :])
    acc = acc + _dot(ob.astype(BF16), w_ref[256:512, :])
    acc = acc + _dot(od.astype(BF16), w_ref[512:768, :])
    acc = acc + _dot(ol.astype(BF16), w_ref[768:1024, :])
    o_ref[0] = x_ref[0] + gt_ref[0] * acc


def _out_projection(x, gt, oa, ry, rg, ret_gn, lam_vecs, diff_gn, lam_init, draw, hf, hb, ly, w_out):
    b, l, d = x.shape
    tm = 512 if l % 512 == 0 else 256
    row = lambda wdt: pl.BlockSpec((1, tm, wdt), lambda i, j: (i, j, 0))
    hd = jnp.arange(GROUP) // HEAD_DIM
    bd = jnp.where(hd[:, None] == hd[None, :], 1.0 / HEAD_DIM, 0.0).astype(BF16)
    return pl.pallas_call(
        functools.partial(_outproj_kernel, lam_init=lam_init),
        out_shape=jax.ShapeDtypeStruct((b, l, d), F32),
        grid=(b, l // tm),
        in_specs=[row(d), pl.BlockSpec((1, 1, d), lambda i, j: (i, 0, 0)), row(256),
                  pl.BlockSpec((1, 2, tm, 256), lambda i, j: (i, 0, j, 0)), row(256),
                  pl.BlockSpec((1, 256), lambda i, j: (0, 0)),
                  pl.BlockSpec((4, DIFF_DIM), lambda i, j: (0, 0)),
                  pl.BlockSpec((N_HEADS, HEAD_DIM, 1), lambda i, j: (0, 0, 0)),
                  pl.BlockSpec((1, N_HEADS, 2, DIFF_VROWS, tm), lambda i, j: (i, 0, 0, 0, j)),
                  row(256), row(256), row(256),
                  pl.BlockSpec((256, 256), lambda i, j: (0, 0)),
                  pl.BlockSpec((d, d), lambda i, j: (0, 0))],
        out_specs=row(d),
        compiler_params=_params("arbitrary", "arbitrary"),
        name="out_projection",
    )(x, gt, oa, ry, rg, ret_gn.reshape(1, 256), lam_vecs, diff_gn.reshape(N_HEADS, HEAD_DIM, 1), draw,
      hf, hb, ly, bd, w_out)


def _norm_modulate(x_ref, g_ref, sh_ref, sc_ref):
    x = x_ref[0]
    ms = jnp.mean(x * x, axis=-1, keepdims=True)
    y = x * lax.rsqrt(ms + NORM_EPS) * g_ref[...]
    return y * (1.0 + sc_ref[0]) + sh_ref[0]


def _ffn_kernel(x_ref, g_ref, sh_ref, sc_ref, gt_ref, w1_ref, w3_ref, w2_ref, o_ref, hb_scr, acc_scr,
                *, n_f):
    f = pl.program_id(2)

    @pl.when(f == 0)
    def _():
        hb_scr[...] = _norm_modulate(x_ref, g_ref, sh_ref, sc_ref).astype(BF16)
        acc_scr[...] = jnp.zeros_like(acc_scr)

    hb = hb_scr[...]
    a = _silu(_dot(hb, w1_ref[...])) * _dot(hb, w3_ref[...])
    acc_scr[...] += _dot(a.astype(BF16), w2_ref[...])

    @pl.when(f == n_f - 1)
    def _():
        o_ref[0] = x_ref[0] + gt_ref[0] * acc_scr[...]


def _ffn(x, g, sh, sc, gt, w1, w3, w2):
    b, l, d = x.shape
    ff = w1.shape[1]
    tm = 1024 if l % 1024 == 0 else 256
    n_f = 2
    fc = ff // n_f
    vec = pl.BlockSpec((1, 1, d), lambda i, j, f: (i, 0, 0))
    return pl.pallas_call(
        functools.partial(_ffn_kernel, n_f=n_f),
        out_shape=jax.ShapeDtypeStruct((b, l, d), F32),
        grid=(b, l // tm, n_f),
        in_specs=[pl.BlockSpec((1, tm, d), lambda i, j, f: (i, j, 0)),
                  pl.BlockSpec((1, d), lambda i, j, f: (0, 0)), vec, vec, vec,
                  pl.BlockSpec((d, fc), lambda i, j, f: (0, f)),
                  pl.BlockSpec((d, fc), lambda i, j, f: (0, f)),
                  pl.BlockSpec((fc, d), lambda i, j, f: (f, 0))],
        out_specs=pl.BlockSpec((1, tm, d), lambda i, j, f: (i, j, 0)),
        scratch_shapes=[pltpu.VMEM((tm, d), BF16), pltpu.VMEM((tm, d), F32)],
        compiler_params=_params("arbitrary", "arbitrary", "arbitrary"),
        name="dense_ffn",
    )(x, g.reshape(1, d), sh, sc, gt, w1, w3, w2)


MOE_RT = 128


def _moe_kernel(x_ref, g_ref, sh_ref, sc_ref, gt_ref, fg_ref, rt_ref, w1_ref, w3_ref, w2_ref, o_ref,
                hb_scr, gate_scr, pos_scr, gatet_scr, post_scr, x_scr, y_scr, gs_scr, cnt_smem,
                *, n_e, n_f, tm, final_norm):
    e = pl.program_id(2)
    f = pl.program_id(3)
    rt2 = 2 * MOE_RT

    @pl.when((e == 0) & (f == 0))
    def _():
        h = _norm_modulate(x_ref, g_ref, sh_ref, sc_ref)
        hb = h.astype(BF16)
        hb_scr[...] = hb
        o_ref[0] = jnp.zeros(o_ref.shape[1:], F32)
        rt = rt_ref[...]
        rhi = rt.astype(BF16)
        rlo = (rt - rhi.astype(F32)).astype(BF16)
        hlo = (h - hb.astype(F32)).astype(BF16)
        logits = _dot(hb, rhi) + _dot(hb, rlo) + _dot(hlo, rhi)
        lane = lax.broadcasted_iota(jnp.int32, logits.shape, 1)
        logits = jnp.where(lane < N_EXPERTS, logits, -jnp.inf)
        m1 = jnp.max(logits, axis=1, keepdims=True)
        i1 = jnp.min(jnp.where(logits == m1, lane, LANE), axis=1, keepdims=True)
        rest = jnp.where(lane == i1, -jnp.inf, logits)
        m2 = jnp.max(rest, axis=1, keepdims=True)
        i2 = jnp.min(jnp.where(rest == m2, lane, LANE), axis=1, keepdims=True)
        e2 = jnp.exp(m2 - m1)
        den = 1.0 + e2
        gate = jnp.where(lane == i1, 1.0 / den, 0.0) + jnp.where(lane == i2, e2 / den, 0.0)
        gate_scr[...] = gate
        sel = jnp.where(gate > 0.0, 1.0, 0.0)
        selb = sel.astype(BF16)
        strip = 256
        for r0 in range(0, tm, strip):
            tok_r = r0 + lax.broadcasted_iota(jnp.int32, (strip, tm), 0)
            tok_c = lax.broadcasted_iota(jnp.int32, (strip, tm), 1)
            earlier = jnp.where(tok_c < tok_r, 1.0, 0.0).astype(BF16)
            pos_scr[r0:r0 + strip, :] = _dot(earlier, selb)
        post_scr[...] = pos_scr[...].T
        gatet_scr[...] = gate.T
        for ee in range(n_e):
            cnt_smem[ee] = jnp.sum(jnp.where(lane == ee, sel, 0.0)).astype(jnp.int32)

    n_tiles = (cnt_smem[e] + (MOE_RT - 1)) // MOE_RT
    n_pairs = (n_tiles + 1) // 2

    @pl.when(f == 0)
    def _():
        sub = lax.broadcasted_iota(jnp.int32, (8, 1), 0)
        pos_e = jnp.sum(jnp.where(sub == e, post_scr[0:8, :], 0.0), axis=0, keepdims=True)
        gate_e = jnp.sum(jnp.where(sub == e, gatet_scr[0:8, :], 0.0), axis=0, keepdims=True)

        def gather(j, carry):
            r0 = pl.multiple_of(j * MOE_RT, MOE_RT)
            slot = (r0 + lax.broadcasted_iota(jnp.int32, (MOE_RT, 1), 0)).astype(F32)
            hit = (pos_e == slot) & (gate_e > 0.0)
            onehot = jnp.where(hit, 1.0, 0.0).astype(BF16)
            x_scr[pl.ds(r0, MOE_RT), :] = _dot(onehot, hb_scr[...]).astype(BF16)
            gs = jnp.sum(jnp.where(hit, gate_e, 0.0), axis=1, keepdims=True)
            gs_scr[pl.ds(r0, MOE_RT), :] = jnp.broadcast_to(gs, (MOE_RT, LANE))
            y_scr[pl.ds(r0, MOE_RT), :] = jnp.zeros((MOE_RT, y_scr.shape[1]), F32)
            return carry

        lax.fori_loop(0, 2 * n_pairs, gather, 0)

    def expert(r0, rows):
        xj = x_scr[pl.ds(r0, rows), :]
        a = _silu(_dot(xj, w1_ref[0])) * _dot(xj, w3_ref[0])
        y_scr[pl.ds(r0, rows), :] += _dot(a.astype(BF16), w2_ref[0])

    def expert_pair(jj, carry):
        expert(pl.multiple_of(jj * rt2, rt2), rt2)
        return carry

    lax.fori_loop(0, n_tiles // 2, expert_pair, 0)

    @pl.when(n_tiles % 2 == 1)
    def _():
        expert(pl.multiple_of((n_tiles - 1) * MOE_RT, MOE_RT), MOE_RT)

    @pl.when(f == n_f - 1)
    def _():
        lane = lax.broadcasted_iota(jnp.int32, (1, LANE), 1)
        pos_col = jnp.sum(jnp.where(lane == e, pos_scr[...], 0.0), axis=1, keepdims=True)
        gate_col = jnp.sum(jnp.where(lane == e, gate_scr[...], 0.0), axis=1, keepdims=True)

        def scatter(jj, carry):
            r0 = pl.multiple_of(jj * rt2, rt2)
            slot = (r0 + lax.broadcasted_iota(jnp.int32, (1, rt2), 1)).astype(F32)
            onehot = jnp.where((pos_col == slot) & (gate_col > 0.0), 1.0, 0.0).astype(BF16)
            ys = (y_scr[pl.ds(r0, rt2), :] * gs_scr[pl.ds(r0, rt2), 0:1]).astype(BF16)
            o_ref[0] += _dot(onehot, ys)
            return carry

        lax.fori_loop(0, n_pairs, scatter, 0)

    @pl.when((e == n_e - 1) & (f == n_f - 1))
    def _():
        out = x_ref[0] + gt_ref[0] * o_ref[0]
        if final_norm:
            ms = jnp.mean(out * out, axis=-1, keepdims=True)
            out = out * lax.rsqrt(ms + NORM_EPS) * fg_ref[...]
        o_ref[0] = out


def _moe_ffn(x, g, sh, sc, gt, final_g, w1, w3, w2, router, final_norm):
    b, l, d = x.shape
    n_e, _, ff = w1.shape
    tm = 1024 if l % 1024 == 0 else 256
    n_f = 2
    fc = ff // n_f
    vec = pl.BlockSpec((1, 1, d), lambda i, j, e, f: (i, 0, 0))
    return pl.pallas_call(
        functools.partial(_moe_kernel, n_e=n_e, n_f=n_f, tm=tm, final_norm=final_norm),
        out_shape=jax.ShapeDtypeStruct((b, l, d), F32),
        grid=(b, l // tm, n_e, n_f),
        in_specs=[pl.BlockSpec((1, tm, d), lambda i, j, e, f: (i, j, 0)),
                  pl.BlockSpec((1, d), lambda i, j, e, f: (0, 0)), vec, vec, vec,
                  pl.BlockSpec((1, d), lambda i, j, e, f: (0, 0)),
                  pl.BlockSpec((d, LANE), lambda i, j, e, f: (0, 0)),
                  pl.BlockSpec((1, d, fc), lambda i, j, e, f: (e, 0, f)),
                  pl.BlockSpec((1, d, fc), lambda i, j, e, f: (e, 0, f)),
                  pl.BlockSpec((1, fc, d), lambda i, j, e, f: (e, f, 0))],
        out_specs=pl.BlockSpec((1, tm, d), lambda i, j, e, f: (i, j, 0)),
        scratch_shapes=[pltpu.VMEM((tm, d), BF16),
                        pltpu.VMEM((tm, LANE), F32),
                        pltpu.VMEM((tm, LANE), F32),
                        pltpu.VMEM((LANE, tm), F32),
                        pltpu.VMEM((LANE, tm), F32),
                        pltpu.VMEM((tm, d), BF16),
                        pltpu.VMEM((tm, d), F32),
                        pltpu.VMEM((tm, LANE), F32),
                        pltpu.SMEM((n_e,), jnp.int32)],
        compiler_params=_params("arbitrary", "arbitrary", "arbitrary", "arbitrary"),
        name="moe_ffn",
    )(x, g.reshape(1, d), sh, sc, gt, final_g.reshape(1, d), router, w1, w3, w2)


def _rope_tables(n):
    rows = n // GRID_W
    row = jnp.repeat(jnp.arange(rows, dtype=F32), GRID_W)
    col = jnp.tile(jnp.arange(GRID_W, dtype=F32), rows)

    def axial(dim):
        nf = dim // 4
        inv = ROPE_BASE ** (-jnp.arange(nf, dtype=F32) / nf)
        return jnp.concatenate([row[:, None] * inv, col[:, None] * inv], axis=-1)

    def line(dim):
        nf = dim // 2
        inv = ROPE_BASE ** (-jnp.arange(nf, dtype=F32) / nf)
        return jnp.arange(n, dtype=F32)[:, None] * inv

    def expand(ang, dim):
        c = jnp.cos(ang)
        s = jnp.sin(ang)
        c = jnp.tile(jnp.concatenate([c, c], axis=-1), (1, LANE // dim))
        s = jnp.tile(jnp.concatenate([-s, s], axis=-1), (1, LANE // dim))
        return c, s

    ca, sa = expand(axial(HEAD_DIM), HEAD_DIM)
    cr, sr = expand(line(HEAD_DIM), HEAD_DIM)
    cd, sd = expand(axial(DIFF_DIM), DIFF_DIM)
    return ca, sa, cr, sr, cd, sd


_ATTN_HEAD_ORDER = (0, 2, 1, 3)


def _attn_perm():
    return jnp.concatenate([jnp.arange(HEAD_DIM) + HEAD_DIM * h for h in _ATTN_HEAD_ORDER])


def _block_diag(w):
    nb, blk, _ = w.shape
    out = jnp.zeros((nb * blk, nb * blk), w.dtype)
    for i in range(nb):
        out = out.at[i * blk:(i + 1) * blk, i * blk:(i + 1) * blk].set(w[i])
    return out


def kernel(x, c, ctx, c_ctx, w_ada, b_ada, g_norm1, g_norm2, w_in, w_out, attn_sink, ret_decay_logit,
           ret_gn, diff_lambda, diff_gn, conv_w, conv_b, lru_wa, lru_ba, lru_wx, lru_bx, lru_lambda,
           ffn_w1, ffn_w3, ffn_w2, moe_router, moe_w1, moe_w3, moe_w2, final_norm):
    b, n, d = x.shape
    m = ctx.shape[1]
    tables = _rope_tables(n)
    perm = _attn_perm()

    cc = jnp.concatenate([c, c_ctx[None, :], jnp.zeros((8 - b - 1, d), F32)], axis=0)
    mods = _mod_vectors(cc, w_ada, b_ada)

    xc = ctx
    for layer in range(DEPTH):
        need_ctx = layer < DEPTH - 1
        lam_init = 0.8 - 0.6 * math.exp(-0.3 * layer)
        mod = mods[layer]
        sh1, sc1, gt1, sh2, sc2, gt2 = [mod[:b, None, i * d:(i + 1) * d] for i in range(6)]
        sh1c, sc1c, gt1c, sh2c, sc2c, gt2c = [
            jnp.broadcast_to(mod[b:b + 1, None, i * d:(i + 1) * d], (b, 1, d)) for i in range(6)]

        wi = w_in[layer]
        wi = jnp.concatenate([wi[:, :256][:, perm], wi[:, 256:]], axis=1).astype(BF16)
        wo = w_out[layer]
        wo = jnp.concatenate([wo[:256][perm], wo[256:]], axis=0).astype(BF16)
        sink_perm = attn_sink[layer][jnp.array(_ATTN_HEAD_ORDER)]

        (qa, ka, va, rq, rk, rv, rg, dqt, dk, dvt, lx, ly) = _in_projection(
            x, g_norm1[layer], sh1, sc1, wi, tables)
        (qac, kac, vac, rqc, rkc, rvc, rgc, dqtc, dkc, dvtc, lxc, lyc) = _in_projection(
            xc, g_norm1[layer], sh1c, sc1c, wi, None)

        oa = _swa_attention(sink_perm, qa, ka, va, kac, vac)
        dl = ret_decay_logit[layer]
        dl_lane = jnp.repeat(dl, HEAD_DIM, axis=1).reshape(2, 1, GROUP)
        dl_head = jnp.broadcast_to(dl[:, :, None, None], (2, N_HEADS, 8, RET_CHUNK))
        ryc, s_ctx = _retention(dl_lane, dl_head, rqc, rkc, rvc, jnp.zeros((b, 2, GROUP, GROUP), F32))
        ry, _ = _retention(dl_lane, dl_head, rq, rk, rv, s_ctx)
        k_all = jnp.concatenate([dk, dkc], axis=1)
        vt_all = jnp.concatenate([dvt, dvtc], axis=1)
        draw = _diff_attention(dqt, k_all, vt_all)
        assemble = functools.partial(_out_projection, ret_gn=ret_gn[layer], lam_vecs=diff_lambda[layer],
                                     diff_gn=diff_gn[layer], lam_init=lam_init, w_out=wo)
        lru = []
        lru_c = []
        for direction in range(2):
            prm = (conv_w[layer], conv_b[layer],
                   _block_diag(lru_wa[layer, direction]).astype(BF16), lru_ba[layer, direction],
                   _block_diag(lru_wx[layer, direction]).astype(BF16), lru_bx[layer, direction],
                   lru_lambda[layer, direction])
            rev = direction == 1
            hc = _lru_scan(lxc, *prm, jnp.zeros((b, 1, GROUP), F32), rev)
            h0 = hc[:, 0:1, :] if rev else hc[:, m - 1:m, :]
            lru.append(_lru_scan(lx, *prm, h0, rev))
            lru_c.append(hc)

        x = assemble(x, gt1, oa, ry, rg, draw=draw, hf=lru[0], hb=lru[1], ly=ly)
        j = layer // 2
        if layer % 2 == 0:
            ffn = functools.partial(_ffn, w1=ffn_w1[j].astype(BF16), w3=ffn_w3[j].astype(BF16),
                                    w2=ffn_w2[j].astype(BF16))
            x = ffn(x, g_norm2[layer], sh2, sc2, gt2)
        else:
            moe = functools.partial(_moe_ffn, final_g=final_norm, w1=moe_w1[j].astype(BF16),
                                    w3=moe_w3[j].astype(BF16), w2=moe_w2[j].astype(BF16),
                                    router=jnp.pad(moe_router[j], ((0, 0), (0, LANE - N_EXPERTS))))
            x = moe(x, g_norm2[layer], sh2, sc2, gt2, final_norm=layer == DEPTH - 1)
            ffn = functools.partial(moe, final_norm=False)

        if need_ctx:
            oac = _ctx_attention(sink_perm, qac, kac, vac)
            drawc = _diff_attention(dqtc, dkc, dvtc)
            xc = assemble(xc, gt1c, oac, ryc, rgc, draw=drawc, hf=lru_c[0], hb=lru_c[1], ly=lyc)
            xc = ffn(xc, g_norm2[layer], sh2c, sc2c, gt2c)
    assert DEPTH % 2 == 0, "the closing RMSNorm is fused into the last (routed) layer"
    return x
```

```python
import functools
import math

import jax
import jax.numpy as jnp
from jax import lax
from jax.experimental import pallas as pl
from jax.experimental.pallas import tpu as pltpu

F32 = jnp.float32
BF16 = jnp.bfloat16

D_MODEL = 1024
DEPTH = 2
GRID_W = 64
HEAD_DIM = 64
GROUP = D_MODEL // 4
N_HEADS = GROUP // HEAD_DIM
DIFF_DIM = HEAD_DIM // 2
WINDOW = 128
NORM_EPS = 1e-6
ROPE_BASE = 10000.0
LRU_BLOCKS = 4
LRU_C = 8.0
N_EXPERTS = 8
LANE = 128
LOG2E = 1.4426950408889634
DIFF_TQ = 256
DIFF_TK = 256
DIFF_VROWS = HEAD_DIM + 16

OFF_AQ, OFF_AK, OFF_AV = 0, 256, 384
OFF_RQ, OFF_RK, OFF_RV, OFF_RG = 512, 768, 1024, 1280
OFF_DQ, OFF_DK, OFF_DV = 1536, 1792, 2048
OFF_LX, OFF_LY = 2304, 2560
D_IN = 2816

VMEM_LIMIT = 56 * 1024 * 1024


def _params(*sem):
    return pltpu.CompilerParams(dimension_semantics=sem, vmem_limit_bytes=VMEM_LIMIT)


def _silu(x):
    return x * jax.nn.sigmoid(x)


def _softplus(x):
    return jnp.maximum(x, 0.0) + jnp.log(1.0 + jnp.exp(-jnp.abs(x)))


def _dot(a, b):
    return jnp.dot(a, b, preferred_element_type=F32)


def _dot_nt(a, b):
    return lax.dot_general(a, b, (((1,), (1,)), ((), ())), preferred_element_type=F32)


def _mod_kernel(c_ref, w_ref, b_ref, o_ref):
    s = _silu(c_ref[...])
    o_ref[0] = _dot(s.astype(BF16), w_ref[0].astype(BF16)) + b_ref[0]


def _mod_vectors(cc, w_ada, b_ada):
    depth, d, n6 = w_ada.shape
    tn = 1536
    return pl.pallas_call(
        _mod_kernel,
        out_shape=jax.ShapeDtypeStruct((depth, 8, n6), F32),
        grid=(depth, n6 // tn),
        in_specs=[pl.BlockSpec((8, d), lambda l, j: (0, 0)),
                  pl.BlockSpec((1, d, tn), lambda l, j: (l, 0, j)),
                  pl.BlockSpec((1, 1, tn), lambda l, j: (l, 0, j))],
        out_specs=pl.BlockSpec((1, 8, tn), lambda l, j: (l, 0, j)),
        compiler_params=_params("arbitrary", "arbitrary"),
        name="mod_vectors",
    )(cc, w_ada, b_ada.reshape(depth, 1, n6))


def _rope_tile(x, cos, sin_signed, d):
    lane = lax.broadcasted_iota(jnp.int32, x.shape, 1)
    first = (lane & (d - 1)) < (d // 2)
    partner = jnp.where(first, pltpu.roll(x, LANE - d // 2, axis=1), pltpu.roll(x, d // 2, axis=1))
    return x * cos + partner * sin_signed


def _inproj_kernel(*refs, rope, tm):
    if rope:
        (x_ref, g_ref, sh_ref, sc_ref, w_ref, ca_ref, sa_ref, cr_ref, sr_ref, cd_ref, sd_ref,
         qa_ref, ka_ref, va_ref, rq_ref, rk_ref, rv_ref, rg_ref,
         dqt_ref, dk_ref, dvt_ref, lx_ref, ly_ref) = refs
    else:
        (x_ref, g_ref, sh_ref, sc_ref, w_ref,
         qa_ref, ka_ref, va_ref, rq_ref, rk_ref, rv_ref, rg_ref,
         dqt_ref, dk_ref, dvt_ref, lx_ref, ly_ref) = refs
    x = x_ref[0]
    ms = jnp.mean(x * x, axis=-1, keepdims=True)
    y = x * lax.rsqrt(ms + NORM_EPS) * g_ref[...]
    h = y * (1.0 + sc_ref[0]) + sh_ref[0]
    hb = h.astype(BF16)

    def proj(off, width):
        return _dot(hb, w_ref[:, off:off + width])

    def roped(p, c_ref, s_ref, d, scale):
        tiles = []
        for t in range(p.shape[1] // LANE):
            pt = p[:, t * LANE:(t + 1) * LANE]
            if rope:
                pt = _rope_tile(pt, c_ref[...], s_ref[...], d)
            tiles.append(pt if scale == 1.0 else pt * scale)
        return tiles[0] if len(tiles) == 1 else jnp.concatenate(tiles, axis=1)

    ca = sa = cr = sr = cd = sd = None
    if rope:
        ca, sa, cr, sr, cd, sd = ca_ref, sa_ref, cr_ref, sr_ref, cd_ref, sd_ref
    qa_ref[0] = roped(proj(OFF_AQ, 256), ca, sa, HEAD_DIM, HEAD_DIM ** -0.5).astype(BF16)
    ka_ref[0] = roped(proj(OFF_AK, 128), ca, sa, HEAD_DIM, 1.0).astype(BF16)
    va_ref[0] = proj(OFF_AV, 128).astype(BF16)
    rq_ref[0] = roped(proj(OFF_RQ, 256), cr, sr, HEAD_DIM, 1.0).astype(BF16)
    rk_ref[0] = roped(proj(OFF_RK, 256), cr, sr, HEAD_DIM, HEAD_DIM ** -0.5).astype(BF16)
    rv_ref[0] = proj(OFF_RV, 256).astype(BF16)
    rg_ref[0] = proj(OFF_RG, 256)
    dq = roped(proj(OFF_DQ, 256), cd, sd, DIFF_DIM, DIFF_DIM ** -0.5 * LOG2E)
    dqt_ref[0] = dq.T.astype(BF16)
    dk_ref[0] = roped(proj(OFF_DK, 256), cd, sd, DIFF_DIM, 1.0).astype(BF16)
    dvt = proj(OFF_DV, 256).T.astype(BF16)
    ones_rows = (lax.broadcasted_iota(jnp.int32, (DIFF_VROWS - HEAD_DIM, DIFF_TK), 0) == 0).astype(BF16)
    for j in range(tm // DIFF_TK):
        for hd in range(N_HEADS):
            dvt_ref[0, j, hd, 0:HEAD_DIM, :] = dvt[hd * HEAD_DIM:(hd + 1) * HEAD_DIM, j * DIFF_TK:(j + 1) * DIFF_TK]
            dvt_ref[0, j, hd, HEAD_DIM:DIFF_VROWS, :] = ones_rows
    lx_ref[0] = proj(OFF_LX, 256)
    ly_ref[0] = proj(OFF_LY, 256)


def _in_projection(x, g, sh, sc, w_in, tables):
    b, l, d = x.shape
    rope = tables is not None
    tm = 512 if l % 512 == 0 else 256
    nt = l // tm
    row = lambda shape: pl.BlockSpec(shape, lambda i, j: (i, j, 0))
    vec = pl.BlockSpec((1, 1, d), lambda i, j: (i, 0, 0))
    in_specs = [row((1, tm, d)), pl.BlockSpec((1, d), lambda i, j: (0, 0)), vec, vec,
                pl.BlockSpec((d, D_IN), lambda i, j: (0, 0))]
    args = [x, g.reshape(1, d), sh, sc, w_in]
    if rope:
        in_specs += [pl.BlockSpec((tm, LANE), lambda i, j: (j, 0))] * 6
        args += list(tables)
    sds = jax.ShapeDtypeStruct
    out_shape = [sds((b, l, 256), BF16), sds((b, l, 128), BF16), sds((b, l, 128), BF16),
                 sds((b, l, 256), BF16), sds((b, l, 256), BF16), sds((b, l, 256), BF16),
                 sds((b, l, 256), F32),
                 sds((b, 256, l), BF16), sds((b, l, 256), BF16),
                 sds((b, l // DIFF_TK, N_HEADS, DIFF_VROWS, DIFF_TK), BF16),
                 sds((b, l, 256), F32), sds((b, l, 256), F32)]
    out_specs = [row((1, tm, 256)), row((1, tm, 128)), row((1, tm, 128)),
                 row((1, tm, 256)), row((1, tm, 256)), row((1, tm, 256)), row((1, tm, 256)),
                 pl.BlockSpec((1, 256, tm), lambda i, j: (i, 0, j)), row((1, tm, 256)),
                 pl.BlockSpec((1, tm // DIFF_TK, N_HEADS, DIFF_VROWS, DIFF_TK), lambda i, j: (i, j, 0, 0, 0)),
                 row((1, tm, 256)), row((1, tm, 256))]
    return pl.pallas_call(
        functools.partial(_inproj_kernel, rope=rope, tm=tm),
        out_shape=out_shape, grid=(b, nt), in_specs=in_specs, out_specs=out_specs,
        compiler_params=_params("arbitrary", "arbitrary"),
        name="in_projection_rope" if rope else "in_projection_ctx",
    )(*args)


def _swa_kernel(*refs, window, tq, n):
    if window:
        sink_ref, q_ref, k_ref, v_ref, kc_ref, vc_ref, o_ref = refs
    else:
        sink_ref, q_ref, kc_ref, vc_ref, o_ref = refs
    kc = kc_ref[0]
    vc = vc_ref[0]
    m = kc.shape[0]
    if window:
        i = pl.program_id(1)
        w = tq + 2 * WINDOW
        t0 = i * tq
        start = pl.multiple_of(jnp.clip(t0 - WINDOW, 0, n - w), WINDOW)
        kcat = jnp.concatenate([k_ref[0, pl.ds(start, w), :], kc], axis=0)
        vcat = jnp.concatenate([v_ref[0, pl.ds(start, w), :], vc], axis=0)
        qpos = t0 + lax.broadcasted_iota(jnp.int32, (tq, w + m), 0)
        col = lax.broadcasted_iota(jnp.int32, (tq, w + m), 1)
        valid = (jnp.abs(qpos - (start + col)) <= WINDOW) | (col >= w)
    else:
        kcat, vcat, valid = kc, vc, None
    q = q_ref[0]
    lane = lax.broadcasted_iota(jnp.int32, (1, LANE), 1)
    out_tiles = []
    for t in range(2):
        qt = q[:, t * LANE:(t + 1) * LANE]
        out_t = None
        for half in range(2):
            in_half = (lane >= half * HEAD_DIM) & (lane < (half + 1) * HEAD_DIM)
            sink = sink_ref[2 * t + half]
            qm = jnp.where(in_half, qt, jnp.zeros_like(qt))
            s = _dot_nt(qm, kcat)
            if valid is not None:
                s = jnp.where(valid, s, -jnp.inf)
            mx = jnp.maximum(jnp.max(s, axis=1, keepdims=True), sink)
            e = jnp.exp(s - mx)
            den = jnp.sum(e, axis=1, keepdims=True) + jnp.exp(sink - mx)
            o = _dot(e.astype(BF16), vcat) / den
            out_t = o if out_t is None else jnp.where(in_half, o, out_t)
        out_tiles.append(out_t)
    o_ref[0] = jnp.concatenate(out_tiles, axis=1)


def _swa_attention(sink_perm, q, k, v, kc, vc):
    b, n, _ = q.shape
    m = kc.shape[1]
    tq = 256
    smem = pl.BlockSpec(memory_space=pltpu.SMEM)
    ctx = pl.BlockSpec((1, m, 128), lambda i, j: (i, 0, 0))
    in_specs = [smem, pl.BlockSpec((1, tq, 256), lambda i, j: (i, j, 0)),
                pl.BlockSpec((1, n, 128), lambda i, j: (i, 0, 0)),
                pl.BlockSpec((1, n, 128), lambda i, j: (i, 0, 0)), ctx, ctx]
    return pl.pallas_call(
        functools.partial(_swa_kernel, window=True, tq=tq, n=n),
        out_shape=jax.ShapeDtypeStruct((b, n, 256), F32),
        grid=(b, n // tq), in_specs=in_specs,
        out_specs=pl.BlockSpec((1, tq, 256), lambda i, j: (i, j, 0)),
        compiler_params=_params("arbitrary", "arbitrary"),
        name="swa_attention",
    )(sink_perm, q, k, v, kc, vc)


def _ctx_attention(sink_perm, qc, kc, vc):
    b, m, _ = qc.shape
    smem = pl.BlockSpec(memory_space=pltpu.SMEM)
    ctx = pl.BlockSpec((1, m, 128), lambda i: (i, 0, 0))
    return pl.pallas_call(
        functools.partial(_swa_kernel, window=False, tq=m, n=m),
        out_shape=jax.ShapeDtypeStruct((b, m, 256), F32),
        grid=(b,), in_specs=[smem, pl.BlockSpec((1, m, 256), lambda i: (i, 0, 0)), ctx, ctx],
        out_specs=pl.BlockSpec((1, m, 256), lambda i: (i, 0, 0)),
        compiler_params=_params("arbitrary"),
        name="ctx_attention",
    )(sink_perm, qc, kc, vc)


RET_CHUNK = 128


def _ret_kernel(dl_lane_ref, dl_head_ref, q_ref, k_ref, v_ref, s0_ref, y_ref, sfin_ref,
                s_scr, dm_scr, xi_scr, zeta_scr, *, ns, group):
    c = RET_CHUNK
    d = pl.program_id(1)
    si = pl.program_id(2)
    fwd = d == 0
    lane_head = lax.broadcasted_iota(jnp.int32, (1, GROUP), 1) // HEAD_DIM

    @pl.when(si == 0)
    def _():
        s_scr[...] = s0_ref[0, 0]
        lgl = -_softplus(-dl_lane_ref[0])
        row = lax.broadcasted_iota(jnp.int32, (c, 1), 0)
        xi_scr[...] = jnp.exp(lgl * jnp.where(fwd, row + 1, c - row).astype(F32))
        zeta_scr[...] = jnp.exp(lgl * jnp.where(fwd, c - 1 - row, row).astype(F32))
        ni = lax.broadcasted_iota(jnp.int32, (c, c), 0)
        mi = lax.broadcasted_iota(jnp.int32, (c, c), 1)
        dd = jnp.where(fwd, ni - mi, mi - ni)
        ddf = jnp.maximum(dd, 0).astype(F32)
        for h in range(N_HEADS):
            lgh = -_softplus(-dl_head_ref[0, h][0:1, :])
            dm_scr[h] = jnp.where(dd >= 0, jnp.exp(lgh * ddf), 0.0)

    dec = jnp.exp(-_softplus(-dl_lane_ref[0]) * float(c))
    rh = lax.broadcasted_iota(jnp.int32, (GROUP, GROUP), 0) // HEAD_DIM
    ch = lax.broadcasted_iota(jnp.int32, (GROUP, GROUP), 1) // HEAD_DIM
    same_head = rh == ch

    def chunk(g, carry):
        r0 = pl.multiple_of(jnp.where(fwd, g, group - 1 - g) * c, c)
        q = q_ref[0, pl.ds(r0, c), :]
        k = k_ref[0, pl.ds(r0, c), :]
        v = v_ref[0, pl.ds(r0, c), :]
        cross = _dot(q, s_scr[...].astype(BF16)) * xi_scr[...]
        inner = jnp.zeros((c, GROUP), F32)
        for h in range(N_HEADS):
            qh = jnp.where(lane_head == h, q, jnp.zeros_like(q))
            s = _dot_nt(qh, k) * dm_scr[h]
            ih = _dot(s.astype(BF16), v)
            inner = jnp.where(lane_head == h, ih, inner)
        y_ref[0, 0, pl.ds(r0, c), :] = inner + cross
        kz = (k.astype(F32) * zeta_scr[...]).T.astype(BF16)
        u = _dot(kz, v)
        s_scr[...] = jnp.where(same_head, dec * s_scr[...] + u, 0.0)
        return carry

    lax.fori_loop(0, group, chunk, 0, unroll=4 if group % 4 == 0 else 2)

    @pl.when(si == ns - 1)
    def _():
        sfin_ref[0, 0] = s_scr[...]


def _retention(dl_lane, dl_head, q, k, v, s0):
    b, l, _ = q.shape
    c = RET_CHUNK
    group = 8 if l % (8 * c) == 0 else 2
    ns = l // (group * c)
    sidx = lambda i, d, j: (i, jnp.where(d == 0, j, ns - 1 - j), 0)
    tok = pl.BlockSpec((1, group * c, 256), sidx)
    st = pl.BlockSpec((1, 1, 256, 256), lambda i, d, j: (i, d, 0, 0))
    return pl.pallas_call(
        functools.partial(_ret_kernel, ns=ns, group=group),
        out_shape=[jax.ShapeDtypeStruct((b, 2, l, 256), F32),
                   jax.ShapeDtypeStruct((b, 2, 256, 256), F32)],
        grid=(b, 2, ns),
        in_specs=[pl.BlockSpec((1, 1, 256), lambda i, d, j: (d, 0, 0)),
                  pl.BlockSpec((1, N_HEADS, 8, c), lambda i, d, j: (d, 0, 0, 0)),
                  tok, tok, tok, st],
        out_specs=[pl.BlockSpec((1, 1, group * c, 256),
                                lambda i, d, j: (i, d, jnp.where(d == 0, j, ns - 1 - j), 0)),
                   st],
        scratch_shapes=[pltpu.VMEM((GROUP, GROUP), F32),
                        pltpu.VMEM((N_HEADS, c, c), F32),
                        pltpu.VMEM((c, GROUP), F32),
                        pltpu.VMEM((c, GROUP), F32)],
        compiler_params=_params("arbitrary", "arbitrary", "arbitrary"),
        name="retention",
    )(dl_lane, dl_head, q, k, v, s0)


def _inner_tiles(tiles):
    for g in (8, 7, 6, 5, 4, 3, 2):
        if tiles % g == 0:
            return g
    return 1


def _diff_kernel(qt_ref, k_ref, vt_ref, o_ref, s_scr, *, tiles, nq):
    h = pl.program_id(1)
    inner = _inner_tiles(tiles)
    outer = tiles // inner
    total = nq * outer
    tq = DIFF_TQ
    rowi = lax.broadcasted_iota(jnp.int32, (LANE, 1), 0)
    hoff = (h % 2) * HEAD_DIM
    rowmask = [(rowi >= hoff + mp * DIFF_DIM) & (rowi < hoff + (mp + 1) * DIFF_DIM) for mp in range(2)]

    def step(mp, g_q, g_e, slot, m_e):
        mx = pv = None
        if g_q is not None:
            q0 = pl.multiple_of((g_q // outer) * tq, tq)
            qt = qt_ref[0, :, pl.ds(q0, tq)]
            qpad = jnp.where(rowmask[mp], qt, jnp.zeros_like(qt))
            kc = (g_q % outer) * inner
        if g_e is not None:
            vc = (g_e % outer) * inner
        for t in range(inner):
            sl = slice(t * DIFF_TK, (t + 1) * DIFF_TK)
            if g_q is not None:
                base = pl.multiple_of((kc + t) * DIFF_TK, DIFF_TK)
                s = _dot(k_ref[0, pl.ds(base, DIFF_TK), :], qpad)
                s_scr[mp, 1 - slot, sl, :] = s
                cm = jnp.max(s, axis=0, keepdims=True)
                mx = cm if mx is None else jnp.maximum(mx, cm)
            if g_e is not None:
                p = jnp.exp2(s_scr[mp, slot, sl, :] - m_e).astype(BF16)
                d = _dot(vt_ref[0, vc + t, 0], p)
                pv = d if pv is None else pv + d
        return mx, pv

    def advance(mp, g_q, g_e, slot, state):
        m_e, alpha, acc = state
        mx, pv = step(mp, g_q, g_e, slot, m_e)
        acc = alpha * acc + pv
        o0 = pl.multiple_of((g_e // outer) * tq, tq)
        o_ref[0, 0, mp, :, pl.ds(o0, tq)] = acc
        if g_q is None:
            return m_e, alpha, acc
        m_base = jnp.where(g_q % outer == 0, -jnp.inf, m_e)
        m_next = jnp.maximum(m_base, mx)
        return m_next, jnp.exp2(m_base - m_next), acc

    zeros = jnp.zeros((DIFF_VROWS, tq), F32)
    states = []
    for mp in range(2):
        m_first, _ = step(mp, 0, None, 1, None)
        states.append((m_first, jnp.zeros((1, tq), F32), zeros))
    states = tuple(states)

    def body(i, carry):
        carry = tuple(advance(mp, 2 * i + 1, 2 * i, 0, carry[mp]) for mp in range(2))
        return tuple(advance(mp, 2 * i + 2, 2 * i + 1, 1, carry[mp]) for mp in range(2))

    last = total - 1
    states = lax.fori_loop(0, last // 2, body, states, unroll=min(4, max(last // 2, 1)))
    if last % 2 == 1:
        states = tuple(advance(mp, last, last - 1, 0, states[mp]) for mp in range(2))
    for mp in range(2):
        advance(mp, None, last, last % 2, states[mp])


def _diff_attention(qt, k_all, vt_all):
    b, _, lq = qt.shape
    lk = k_all.shape[1]
    tiles = lk // DIFF_TK
    nq = lq // DIFF_TQ
    return pl.pallas_call(
        functools.partial(_diff_kernel, tiles=tiles, nq=nq),
        out_shape=jax.ShapeDtypeStruct((b, N_HEADS, 2, DIFF_VROWS, lq), F32),
        grid=(b, N_HEADS),
        in_specs=[pl.BlockSpec((1, LANE, lq), lambda i, h: (i, h // 2, 0)),
                  pl.BlockSpec((1, lk, LANE), lambda i, h: (i, 0, h // 2)),
                  pl.BlockSpec((1, tiles, 1, DIFF_VROWS, DIFF_TK), lambda i, h: (i, 0, h, 0, 0))],
        out_specs=pl.BlockSpec((1, 1, 2, DIFF_VROWS, lq), lambda i, h: (i, h, 0, 0, 0)),
        scratch_shapes=[pltpu.VMEM((2, 2, _inner_tiles(tiles) * DIFF_TK, DIFF_TQ), F32)],
        compiler_params=_params("arbitrary", "arbitrary"),
        name="diff_attention",
    )(qt, k_all, vt_all)


def _lru_kernel(x_ref, xp_ref, xn_ref, cw_ref, cb_ref, wa_ref, ba_ref, wx_ref, bx_ref, lam_ref,
                h0_ref, h_ref, carry, *, reverse, tm, nt):
    j = pl.program_id(1)
    tix = (nt - 1 - j) if reverse else j

    @pl.when(j == 0)
    def _():
        carry[...] = jnp.broadcast_to(h0_ref[0], carry.shape)

    x = x_ref[0]
    row = lax.broadcasted_iota(jnp.int32, (tm, 1), 0)
    prev = jnp.where(tix > 0, xp_ref[0][7:8, :], 0.0)
    nxt0 = jnp.where(tix < nt - 1, xn_ref[0][0:1, :], 0.0)
    nxt1 = jnp.where(tix < nt - 1, xn_ref[0][1:2, :], 0.0)
    xm1 = jnp.where(row == 0, prev, pltpu.roll(x, 1, axis=0))
    xp1 = jnp.where(row == tm - 1, nxt0, pltpu.roll(x, tm - 1, axis=0))
    xp2 = jnp.where(row == tm - 1, nxt1, jnp.where(row == tm - 2, nxt0, pltpu.roll(x, tm - 2, axis=0)))
    cw = cw_ref[...]
    u = cw[0:1] * xm1 + cw[1:2] * x + cw[2:3] * xp1 + cw[3:4] * xp2 + cb_ref[...]
    ub = u.astype(BF16)
    r = jax.nn.sigmoid(_dot(ub, wa_ref[0]) + ba_ref[0])
    gi = jax.nn.sigmoid(_dot(ub, wx_ref[0]) + bx_ref[0])
    log_a = -LRU_C * r * _softplus(-lam_ref[0])
    a = jnp.exp(log_a)
    th = jnp.tanh(log_a)
    bv = jnp.sqrt(-2.0 * th / (1.0 - th)) * (gi * u)
    k = 1
    while k < tm:
        if k < 8:
            keep = (row < tm - k) if reverse else (row >= k)
            shift = (tm - k) if reverse else k
            a_sh = jnp.where(keep, pltpu.roll(a, shift, axis=0), 1.0)
            b_sh = jnp.where(keep, pltpu.roll(bv, shift, axis=0), 0.0)
            bv = a * b_sh + bv
            a = a * a_sh
        elif reverse:
            head_b = a[:tm - k] * bv[k:] + bv[:tm - k]
            head_a = a[:tm - k] * a[k:]
            bv = jnp.concatenate([head_b, bv[tm - k:]], axis=0)
            a = jnp.concatenate([head_a, a[tm - k:]], axis=0)
        else:
            tail_b = a[k:] * bv[:tm - k] + bv[k:]
            tail_a = a[k:] * a[:tm - k]
            bv = jnp.concatenate([bv[:k], tail_b], axis=0)
            a = jnp.concatenate([a[:k], tail_a], axis=0)
        k *= 2
    hcur = a * carry[0:1, :] + bv
    h_ref[0] = hcur
    last = hcur[0:1, :] if reverse else hcur[tm - 1:tm, :]
    carry[...] = jnp.broadcast_to(last, carry.shape)


def _lru_scan(lx, conv_w, conv_b, wa_bd, ba, wx_bd, bx, lam, h0, reverse):
    b, l, w = lx.shape
    tm = 512 if l % 512 == 0 else 256
    nt = l // tm
    tb = tm // 8
    tidx = (lambda j: nt - 1 - j) if reverse else (lambda j: j)
    vec = pl.BlockSpec((1, w), lambda i, j: (0, 0))
    mat = pl.BlockSpec((1, w, w), lambda i, j: (0, 0, 0))
    return pl.pallas_call(
        functools.partial(_lru_kernel, reverse=reverse, tm=tm, nt=nt),
        out_shape=jax.ShapeDtypeStruct((b, l, w), F32),
        grid=(b, nt),
        in_specs=[pl.BlockSpec((1, tm, w), lambda i, j: (i, tidx(j), 0)),
                  pl.BlockSpec((1, 8, w), lambda i, j: (i, jnp.maximum(tidx(j) * tb - 1, 0), 0)),
                  pl.BlockSpec((1, 8, w), lambda i, j: (i, jnp.minimum((tidx(j) + 1) * tb, l // 8 - 1), 0)),
                  pl.BlockSpec((4, w), lambda i, j: (0, 0)), vec,
                  mat, vec, mat, vec, vec,
                  pl.BlockSpec((1, 1, w), lambda i, j: (i, 0, 0))],
        out_specs=pl.BlockSpec((1, tm, w), lambda i, j: (i, tidx(j), 0)),
        scratch_shapes=[pltpu.VMEM((8, w), F32)],
        compiler_params=_params("arbitrary", "arbitrary"),
        name="lru_scan_rev" if reverse else "lru_scan_fwd",
    )(lx, lx, lx, conv_w, conv_b.reshape(1, w), wa_bd.reshape(1, w, w), ba.reshape(1, w),
      wx_bd.reshape(1, w, w), bx.reshape(1, w), lam.reshape(1, w), h0)


def _outproj_kernel(x_ref, gt_ref, oa_ref, ry_ref, rg_ref, rgn_ref, lam_ref, dgn_ref, draw_ref, hf_ref, hb_ref,
                    ly_ref, bd_ref, w_ref, o_ref, *, lam_init):
    oa = oa_ref[0]
    y = ry_ref[0, 0] + ry_ref[0, 1]
    y2 = y * y
    hi = y2.astype(BF16)
    lo = (y2 - hi.astype(F32)).astype(BF16)
    ms = _dot(hi, bd_ref[...]) + _dot(lo, bd_ref[...])
    ob = y * lax.rsqrt(ms + NORM_EPS) * rgn_ref[...] * _silu(rg_ref[0])
    lv = lam_ref[...]
    lam = (jnp.exp(jnp.sum(lv[0:1] * lv[1:2], axis=1, keepdims=True))
           - jnp.exp(jnp.sum(lv[2:3] * lv[3:4], axis=1, keepdims=True)) + lam_init)
    heads = []
    for hd in range(N_HEADS):
        a0 = draw_ref[0, hd, 0]
        a1 = draw_ref[0, hd, 1]
        oh = (a0[0:HEAD_DIM] / a0[HEAD_DIM:HEAD_DIM + 1]
              - lam * (a1[0:HEAD_DIM] / a1[HEAD_DIM:HEAD_DIM + 1]))
        msd = jnp.mean(oh * oh, axis=0, keepdims=True)
        heads.append(oh * lax.rsqrt(msd + NORM_EPS) * (dgn_ref[hd] * (1.0 - lam_init)))
    od = jnp.concatenate(heads, axis=0).T
    ol = (hf_ref[0] + hb_ref[0]) * jax.nn.gelu(ly_ref[0])
    acc = _dot(oa.astype(BF16), w_ref[0:256, :])
    acc = acc + _dot(ob.astype(BF16), w_ref[256:512, :])
    acc = acc + _dot(od.astype(BF16), w_ref[512:768, :])
    acc = acc + _dot(ol.astype(BF16), w_ref[768:1024, :])
    o_ref[0] = x_ref[0] + gt_ref[0] * acc


def _out_projection(x, gt, oa, ry, rg, ret_gn, lam_vecs, diff_gn, lam_init, draw, hf, hb, ly, w_out):
    b, l, d = x.shape
    tm = 512 if l % 512 == 0 else 256
    row = lambda wdt: pl.BlockSpec((1, tm, wdt), lambda i, j: (i, j, 0))
    hd = jnp.arange(GROUP) // HEAD_DIM
    bd = jnp.where(hd[:, None] == hd[None, :], 1.0 / HEAD_DIM, 0.0).astype(BF16)
    return pl.pallas_call(
        functools.partial(_outproj_kernel, lam_init=lam_init),
        out_shape=jax.ShapeDtypeStruct((b, l, d), F32),
        grid=(b, l // tm),
        in_specs=[row(d), pl.BlockSpec((1, 1, d), lambda i, j: (i, 0, 0)), row(256),
                  pl.BlockSpec((1, 2, tm, 256), lambda i, j: (i, 0, j, 0)), row(256),
                  pl.BlockSpec((1, 256), lambda i, j: (0, 0)),
                  pl.BlockSpec((4, DIFF_DIM), lambda i, j: (0, 0)),
                  pl.BlockSpec((N_HEADS, HEAD_DIM, 1), lambda i, j: (0, 0, 0)),
                  pl.BlockSpec((1, N_HEADS, 2, DIFF_VROWS, tm), lambda i, j: (i, 0, 0, 0, j)),
                  row(256), row(256), row(256),
                  pl.BlockSpec((256, 256), lambda i, j: (0, 0)),
                  pl.BlockSpec((d, d), lambda i, j: (0, 0))],
        out_specs=row(d),
        compiler_params=_params("arbitrary", "arbitrary"),
        name="out_projection",
    )(x, gt, oa, ry, rg, ret_gn.reshape(1, 256), lam_vecs, diff_gn.reshape(N_HEADS, HEAD_DIM, 1), draw,
      hf, hb, ly, bd, w_out)


def _norm_modulate(x_ref, g_ref, sh_ref, sc_ref):
    x = x_ref[0]
    ms = jnp.mean(x * x, axis=-1, keepdims=True)
    y = x * lax.rsqrt(ms + NORM_EPS) * g_ref[...]
    return y * (1.0 + sc_ref[0]) + sh_ref[0]


def _ffn_kernel(x_ref, g_ref, sh_ref, sc_ref, gt_ref, w1_ref, w3_ref, w2_ref, o_ref, hb_scr, acc_scr,
                *, n_f):
    f = pl.program_id(2)

    @pl.when(f == 0)
    def _():
        hb_scr[...] = _norm_modulate(x_ref, g_ref, sh_ref, sc_ref).astype(BF16)
        acc_scr[...] = jnp.zeros_like(acc_scr)

    hb = hb_scr[...]
    a = _silu(_dot(hb, w1_ref[...])) * _dot(hb, w3_ref[...])
    acc_scr[...] += _dot(a.astype(BF16), w2_ref[...])

    @pl.when(f == n_f - 1)
    def _():
        o_ref[0] = x_ref[0] + gt_ref[0] * acc_scr[...]


def _ffn(x, g, sh, sc, gt, w1, w3, w2):
    b, l, d = x.shape
    ff = w1.shape[1]
    tm = 1024 if l % 1024 == 0 else 256
    n_f = 2
    fc = ff // n_f
    vec = pl.BlockSpec((1, 1, d), lambda i, j, f: (i, 0, 0))
    return pl.pallas_call(
        functools.partial(_ffn_kernel, n_f=n_f),
        out_shape=jax.ShapeDtypeStruct((b, l, d), F32),
        grid=(b, l // tm, n_f),
        in_specs=[pl.BlockSpec((1, tm, d), lambda i, j, f: (i, j, 0)),
                  pl.BlockSpec((1, d), lambda i, j, f: (0, 0)), vec, vec, vec,
                  pl.BlockSpec((d, fc), lambda i, j, f: (0, f)),
                  pl.BlockSpec((d, fc), lambda i, j, f: (0, f)),
                  pl.BlockSpec((fc, d), lambda i, j, f: (f, 0))],
        out_specs=pl.BlockSpec((1, tm, d), lambda i, j, f: (i, j, 0)),
        scratch_shapes=[pltpu.VMEM((tm, d), BF16), pltpu.VMEM((tm, d), F32)],
        compiler_params=_params("arbitrary", "arbitrary", "arbitrary"),
        name="dense_ffn",
    )(x, g.reshape(1, d), sh, sc, gt, w1, w3, w2)


MOE_RT = 128


def _moe_kernel(x_ref, g_ref, sh_ref, sc_ref, gt_ref, fg_ref, rt_ref, w1_ref, w3_ref, w2_ref, o_ref,
                hb_scr, gate_scr, pos_scr, gatet_scr, post_scr, x_scr, y_scr, gs_scr, cnt_smem,
                *, n_e, n_f, tm, final_norm):
    e = pl.program_id(2)
    f = pl.program_id(3)
    rt2 = 2 * MOE_RT

    @pl.when((e == 0) & (f == 0))
    def _():
        h = _norm_modulate(x_ref, g_ref, sh_ref, sc_ref)
        hb = h.astype(BF16)
        hb_scr[...] = hb
        o_ref[0] = jnp.zeros(o_ref.shape[1:], F32)
        rt = rt_ref[...]
        rhi = rt.astype(BF16)
        rlo = (rt - rhi.astype(F32)).astype(BF16)
        hlo = (h - hb.astype(F32)).astype(BF16)
        logits = _dot(hb, rhi) + _dot(hb, rlo) + _dot(hlo, rhi)
        lane = lax.broadcasted_iota(jnp.int32, logits.shape, 1)
        logits = jnp.where(lane < N_EXPERTS, logits, -jnp.inf)
        m1 = jnp.max(logits, axis=1, keepdims=True)
        i1 = jnp.min(jnp.where(logits == m1, lane, LANE), axis=1, keepdims=True)
        rest = jnp.where(lane == i1, -jnp.inf, logits)
        m2 = jnp.max(rest, axis=1, keepdims=True)
        i2 = jnp.min(jnp.where(rest == m2, lane, LANE), axis=1, keepdims=True)
        e2 = jnp.exp(m2 - m1)
        den = 1.0 + e2
        gate = jnp.where(lane == i1, 1.0 / den, 0.0) + jnp.where(lane == i2, e2 / den, 0.0)
        gate_scr[...] = gate
        sel = jnp.where(gate > 0.0, 1.0, 0.0)
        selb = sel.astype(BF16)
        strip = 256
        for r0 in range(0, tm, strip):
            tok_r = r0 + lax.broadcasted_iota(jnp.int32, (strip, tm), 0)
            tok_c = lax.broadcasted_iota(jnp.int32, (strip, tm), 1)
            earlier = jnp.where(tok_c < tok_r, 1.0, 0.0).astype(BF16)
            pos_scr[r0:r0 + strip, :] = _dot(earlier, selb)
        post_scr[...] = pos_scr[...].T
        gatet_scr[...] = gate.T
        for ee in range(n_e):
            cnt_smem[ee] = jnp.sum(jnp.where(lane == ee, sel, 0.0)).astype(jnp.int32)

    n_tiles = (cnt_smem[e] + (MOE_RT - 1)) // MOE_RT
    n_pairs = (n_tiles + 1) // 2

    @pl.when(f == 0)
    def _():
        sub = lax.broadcasted_iota(jnp.int32, (8, 1), 0)
        pos_e = jnp.sum(jnp.where(sub == e, post_scr[0:8, :], 0.0), axis=0, keepdims=True)
        gate_e = jnp.sum(jnp.where(sub == e, gatet_scr[0:8, :], 0.0), axis=0, keepdims=True)

        def gather(jj, carry):
            r0 = pl.multiple_of(jj * rt2, rt2)
            slot = (r0 + lax.broadcasted_iota(jnp.int32, (rt2, 1), 0)).astype(F32)
            hit = (pos_e == slot) & (gate_e > 0.0)
            onehot = jnp.where(hit, 1.0, 0.0).astype(BF16)
            x_scr[pl.ds(r0, rt2), :] = _dot(onehot, hb_scr[...]).astype(BF16)
            gs = jnp.sum(jnp.where(hit, gate_e, 0.0), axis=1, keepdims=True)
            gs_scr[pl.ds(r0, rt2), :] = jnp.broadcast_to(gs, (rt2, LANE))
            y_scr[pl.ds(r0, rt2), :] = jnp.zeros((rt2, y_scr.shape[1]), F32)
            return carry

        lax.fori_loop(0, n_pairs, gather, 0)

    def expert(r0, rows):
        xj = x_scr[pl.ds(r0, rows), :]
        a = _silu(_dot(xj, w1_ref[0])) * _dot(xj, w3_ref[0])
        y_scr[pl.ds(r0, rows), :] += _dot(a.astype(BF16), w2_ref[0])

    def expert_pair(jj, carry):
        expert(pl.multiple_of(jj * rt2, rt2), rt2)
        return carry

    lax.fori_loop(0, n_tiles // 2, expert_pair, 0)

    @pl.when(n_tiles % 2 == 1)
    def _():
        expert(pl.multiple_of((n_tiles - 1) * MOE_RT, MOE_RT), MOE_RT)

    @pl.when(f == n_f - 1)
    def _():
        lane = lax.broadcasted_iota(jnp.int32, (1, LANE), 1)
        pos_col = jnp.sum(jnp.where(lane == e, pos_scr[...], 0.0), axis=1, keepdims=True)
        gate_col = jnp.sum(jnp.where(lane == e, gate_scr[...], 0.0), axis=1, keepdims=True)

        def scatter(jj, carry):
            r0 = pl.multiple_of(jj * rt2, rt2)
            slot = (r0 + lax.broadcasted_iota(jnp.int32, (1, rt2), 1)).astype(F32)
            onehot = jnp.where((pos_col == slot) & (gate_col > 0.0), 1.0, 0.0).astype(BF16)
            ys = (y_scr[pl.ds(r0, rt2), :] * gs_scr[pl.ds(r0, rt2), 0:1]).astype(BF16)
            o_ref[0] += _dot(onehot, ys)
            return carry

        lax.fori_loop(0, n_pairs, scatter, 0)

    @pl.when((e == n_e - 1) & (f == n_f - 1))
    def _():
        out = x_ref[0] + gt_ref[0] * o_ref[0]
        if final_norm:
            ms = jnp.mean(out * out, axis=-1, keepdims=True)
            out = out * lax.rsqrt(ms + NORM_EPS) * fg_ref[...]
        o_ref[0] = out


def _moe_ffn(x, g, sh, sc, gt, final_g, w1, w3, w2, router, final_norm):
    b, l, d = x.shape
    n_e, _, ff = w1.shape
    tm = 1024 if l % 1024 == 0 else 256
    n_f = 2
    fc = ff // n_f
    vec = pl.BlockSpec((1, 1, d), lambda i, j, e, f: (i, 0, 0))
    return pl.pallas_call(
        functools.partial(_moe_kernel, n_e=n_e, n_f=n_f, tm=tm, final_norm=final_norm),
        out_shape=jax.ShapeDtypeStruct((b, l, d), F32),
        grid=(b, l // tm, n_e, n_f),
        in_specs=[pl.BlockSpec((1, tm, d), lambda i, j, e, f: (i, j, 0)),
                  pl.BlockSpec((1, d), lambda i, j, e, f: (0, 0)), vec, vec, vec,
                  pl.BlockSpec((1, d), lambda i, j, e, f: (0, 0)),
                  pl.BlockSpec((d, LANE), lambda i, j, e, f: (0, 0)),
                  pl.BlockSpec((1, d, fc), lambda i, j, e, f: (e, 0, f)),
                  pl.BlockSpec((1, d, fc), lambda i, j, e, f: (e, 0, f)),
                  pl.BlockSpec((1, fc, d), lambda i, j, e, f: (e, f, 0))],
        out_specs=pl.BlockSpec((1, tm, d), lambda i, j, e, f: (i, j, 0)),
        scratch_shapes=[pltpu.VMEM((tm, d), BF16),
                        pltpu.VMEM((tm, LANE), F32),
                        pltpu.VMEM((tm, LANE), F32),
                        pltpu.VMEM((LANE, tm), F32),
                        pltpu.VMEM((LANE, tm), F32),
                        pltpu.VMEM((tm, d), BF16),
                        pltpu.VMEM((tm, d), F32),
                        pltpu.VMEM((tm, LANE), F32),
                        pltpu.SMEM((n_e,), jnp.int32)],
        compiler_params=_params("arbitrary", "arbitrary", "arbitrary", "arbitrary"),
        name="moe_ffn",
    )(x, g.reshape(1, d), sh, sc, gt, final_g.reshape(1, d), router, w1, w3, w2)


def _rope_tables(n):
    rows = n // GRID_W
    row = jnp.repeat(jnp.arange(rows, dtype=F32), GRID_W)
    col = jnp.tile(jnp.arange(GRID_W, dtype=F32), rows)

    def axial(dim):
        nf = dim // 4
        inv = ROPE_BASE ** (-jnp.arange(nf, dtype=F32) / nf)
        return jnp.concatenate([row[:, None] * inv, col[:, None] * inv], axis=-1)

    def line(dim):
        nf = dim // 2
        inv = ROPE_BASE ** (-jnp.arange(nf, dtype=F32) / nf)
        return jnp.arange(n, dtype=F32)[:, None] * inv

    def expand(ang, dim):
        c = jnp.cos(ang)
        s = jnp.sin(ang)
        c = jnp.tile(jnp.concatenate([c, c], axis=-1), (1, LANE // dim))
        s = jnp.tile(jnp.concatenate([-s, s], axis=-1), (1, LANE // dim))
        return c, s

    ca, sa = expand(axial(HEAD_DIM), HEAD_DIM)
    cr, sr = expand(line(HEAD_DIM), HEAD_DIM)
    cd, sd = expand(axial(DIFF_DIM), DIFF_DIM)
    return ca, sa, cr, sr, cd, sd


_ATTN_HEAD_ORDER = (0, 2, 1, 3)


def _attn_perm():
    return jnp.concatenate([jnp.arange(HEAD_DIM) + HEAD_DIM * h for h in _ATTN_HEAD_ORDER])


def _block_diag(w):
    nb, blk, _ = w.shape
    out = jnp.zeros((nb * blk, nb * blk), w.dtype)
    for i in range(nb):
        out = out.at[i * blk:(i + 1) * blk, i * blk:(i + 1) * blk].set(w[i])
    return out


def kernel(x, c, ctx, c_ctx, w_ada, b_ada, g_norm1, g_norm2, w_in, w_out, attn_sink, ret_decay_logit,
           ret_gn, diff_lambda, diff_gn, conv_w, conv_b, lru_wa, lru_ba, lru_wx, lru_bx, lru_lambda,
           ffn_w1, ffn_w3, ffn_w2, moe_router, moe_w1, moe_w3, moe_w2, final_norm):
    b, n, d = x.shape
    m = ctx.shape[1]
    tables = _rope_tables(n)
    perm = _attn_perm()

    cc = jnp.concatenate([c, c_ctx[None, :], jnp.zeros((8 - b - 1, d), F32)], axis=0)
    mods = _mod_vectors(cc, w_ada, b_ada)

    xc = ctx
    for layer in range(DEPTH):
        need_ctx = layer < DEPTH - 1
        lam_init = 0.8 - 0.6 * math.exp(-0.3 * layer)
        mod = mods[layer]
        sh1, sc1, gt1, sh2, sc2, gt2 = [mod[:b, None, i * d:(i + 1) * d] for i in range(6)]
        sh1c, sc1c, gt1c, sh2c, sc2c, gt2c = [
            jnp.broadcast_to(mod[b:b + 1, None, i * d:(i + 1) * d], (b, 1, d)) for i in range(6)]

        wi = w_in[layer]
        wi = jnp.concatenate([wi[:, :256][:, perm], wi[:, 256:]], axis=1).astype(BF16)
        wo = w_out[layer]
        wo = jnp.concatenate([wo[:256][perm], wo[256:]], axis=0).astype(BF16)
        sink_perm = attn_sink[layer][jnp.array(_ATTN_HEAD_ORDER)]

        (qa, ka, va, rq, rk, rv, rg, dqt, dk, dvt, lx, ly) = _in_projection(
            x, g_norm1[layer], sh1, sc1, wi, tables)
        (qac, kac, vac, rqc, rkc, rvc, rgc, dqtc, dkc, dvtc, lxc, lyc) = _in_projection(
            xc, g_norm1[layer], sh1c, sc1c, wi, None)

        oa = _swa_attention(sink_perm, qa, ka, va, kac, vac)
        dl = ret_decay_logit[layer]
        dl_lane = jnp.repeat(dl, HEAD_DIM, axis=1).reshape(2, 1, GROUP)
        dl_head = jnp.broadcast_to(dl[:, :, None, None], (2, N_HEADS, 8, RET_CHUNK))
        ryc, s_ctx = _retention(dl_lane, dl_head, rqc, rkc, rvc, jnp.zeros((b, 2, GROUP, GROUP), F32))
        ry, _ = _retention(dl_lane, dl_head, rq, rk, rv, s_ctx)
        k_all = jnp.concatenate([dk, dkc], axis=1)
        vt_all = jnp.concatenate([dvt, dvtc], axis=1)
        draw = _diff_attention(dqt, k_all, vt_all)
        assemble = functools.partial(_out_projection, ret_gn=ret_gn[layer], lam_vecs=diff_lambda[layer],
                                     diff_gn=diff_gn[layer], lam_init=lam_init, w_out=wo)
        lru = []
        lru_c = []
        for direction in range(2):
            prm = (conv_w[layer], conv_b[layer],
                   _block_diag(lru_wa[layer, direction]).astype(BF16), lru_ba[layer, direction],
                   _block_diag(lru_wx[layer, direction]).astype(BF16), lru_bx[layer, direction],
                   lru_lambda[layer, direction])
            rev = direction == 1
            hc = _lru_scan(lxc, *prm, jnp.zeros((b, 1, GROUP), F32), rev)
            h0 = hc[:, 0:1, :] if rev else hc[:, m - 1:m, :]
            lru.append(_lru_scan(lx, *prm, h0, rev))
            lru_c.append(hc)

        x = assemble(x, gt1, oa, ry, rg, draw=draw, hf=lru[0], hb=lru[1], ly=ly)
        j = layer // 2
        if layer % 2 == 0:
            ffn = functools.partial(_ffn, w1=ffn_w1[j].astype(BF16), w3=ffn_w3[j].astype(BF16),
                                    w2=ffn_w2[j].astype(BF16))
            x = ffn(x, g_norm2[layer], sh2, sc2, gt2)
        else:
            moe = functools.partial(_moe_ffn, final_g=final_norm, w1=moe_w1[j].astype(BF16),
                                    w3=moe_w3[j].astype(BF16), w2=moe_w2[j].astype(BF16),
                                    router=jnp.pad(moe_router[j], ((0, 0), (0, LANE - N_EXPERTS))))
            x = moe(x, g_norm2[layer], sh2, sc2, gt2, final_norm=layer == DEPTH - 1)
            ffn = functools.partial(moe, final_norm=False)

        if need_ctx:
            oac = _ctx_attention(sink_perm, qac, kac, vac)
            drawc = _diff_attention(dqtc, dkc, dvtc)
            xc = assemble(xc, gt1c, oac, ryc, rgc, draw=drawc, hf=lru_c[0], hb=lru_c[1], ly=lyc)
            xc = ffn(xc, g_norm2[layer], sh2c, sc2c, gt2c)
    assert DEPTH % 2 == 0, "the closing RMSNorm is fused into the last (routed) layer"
    return x
```

```python
import functools
import math

import jax
import jax.numpy as jnp
from jax import lax
from jax.experimental import pallas as pl
from jax.experimental.pallas import tpu as pltpu

F32 = jnp.float32
BF16 = jnp.bfloat16

D_MODEL = 1024
DEPTH = 2
GRID_W = 64
HEAD_DIM = 64
GROUP = D_MODEL // 4
N_HEADS = GROUP // HEAD_DIM
DIFF_DIM = HEAD_DIM // 2
WINDOW = 128
NORM_EPS = 1e-6
ROPE_BASE = 10000.0
LRU_BLOCKS = 4
LRU_C = 8.0
N_EXPERTS = 8
LANE = 128
LOG2E = 1.4426950408889634
DIFF_TQ = 256
DIFF_TK = 256
DIFF_VROWS = HEAD_DIM + 16

OFF_AQ, OFF_AK, OFF_AV = 0, 256, 384
OFF_RQ, OFF_RK, OFF_RV, OFF_RG = 512, 768, 1024, 1280
OFF_DQ, OFF_DK, OFF_DV = 1536, 1792, 2048
OFF_LX, OFF_LY = 2304, 2560
D_IN = 2816

VMEM_LIMIT = 56 * 1024 * 1024
ROWS_PROJ = 512
ROWS_FFN = 1024
ROWS_MIN = 256


def _row_tile(l, preferred):
    tm = preferred if l % preferred == 0 else ROWS_MIN
    assert l % tm == 0, (l, tm)
    return tm


def _params(*sem):
    return pltpu.CompilerParams(dimension_semantics=sem, vmem_limit_bytes=VMEM_LIMIT)


def _silu(x):
    return x * jax.nn.sigmoid(x)


def _softplus(x):
    return jnp.maximum(x, 0.0) + jnp.log(1.0 + jnp.exp(-jnp.abs(x)))


def _dot(a, b):
    return jnp.dot(a, b, preferred_element_type=F32)


def _dot_nt(a, b):
    return lax.dot_general(a, b, (((1,), (1,)), ((), ())), preferred_element_type=F32)


def _mod_kernel(c_ref, w_ref, b_ref, o_ref):
    s = _silu(c_ref[...])
    o_ref[0] = _dot(s.astype(BF16), w_ref[0].astype(BF16)) + b_ref[0]


def _mod_vectors(cc, w_ada, b_ada):
    depth, d, n6 = w_ada.shape
    tn = 1536
    return pl.pallas_call(
        _mod_kernel,
        out_shape=jax.ShapeDtypeStruct((depth, 8, n6), F32),
        grid=(depth, n6 // tn),
        in_specs=[pl.BlockSpec((8, d), lambda l, j: (0, 0)),
                  pl.BlockSpec((1, d, tn), lambda l, j: (l, 0, j)),
                  pl.BlockSpec((1, 1, tn), lambda l, j: (l, 0, j))],
        out_specs=pl.BlockSpec((1, 8, tn), lambda l, j: (l, 0, j)),
        compiler_params=_params("arbitrary", "arbitrary"),
        name="mod_vectors",
    )(cc, w_ada, b_ada.reshape(depth, 1, n6))


def _rope_tile(x, cos, sin_signed, d):
    lane = lax.broadcasted_iota(jnp.int32, x.shape, 1)
    first = (lane & (d - 1)) < (d // 2)
    partner = jnp.where(first, pltpu.roll(x, LANE - d // 2, axis=1), pltpu.roll(x, d // 2, axis=1))
    return x * cos + partner * sin_signed


def _inproj_kernel(*refs, rope, tm):
    if rope:
        (x_ref, g_ref, sh_ref, sc_ref, w_ref, ca_ref, sa_ref, cr_ref, sr_ref, cd_ref, sd_ref,
         qa_ref, ka_ref, va_ref, rq_ref, rk_ref, rv_ref, rg_ref,
         dqt_ref, dk_ref, dvt_ref, lx_ref, ly_ref) = refs
    else:
        (x_ref, g_ref, sh_ref, sc_ref, w_ref,
         qa_ref, ka_ref, va_ref, rq_ref, rk_ref, rv_ref, rg_ref,
         dqt_ref, dk_ref, dvt_ref, lx_ref, ly_ref) = refs
    x = x_ref[0]
    ms = jnp.mean(x * x, axis=-1, keepdims=True)
    y = x * lax.rsqrt(ms + NORM_EPS) * g_ref[...]
    h = y * (1.0 + sc_ref[0]) + sh_ref[0]
    hb = h.astype(BF16)

    def proj(off, width):
        return _dot(hb, w_ref[:, off:off + width])

    def roped(p, c_ref, s_ref, d, scale):
        tiles = []
        for t in range(p.shape[1] // LANE):
            pt = p[:, t * LANE:(t + 1) * LANE]
            if rope:
                pt = _rope_tile(pt, c_ref[...], s_ref[...], d)
            tiles.append(pt if scale == 1.0 else pt * scale)
        return tiles[0] if len(tiles) == 1 else jnp.concatenate(tiles, axis=1)

    ca = sa = cr = sr = cd = sd = None
    if rope:
        ca, sa, cr, sr, cd, sd = ca_ref, sa_ref, cr_ref, sr_ref, cd_ref, sd_ref
    qa_ref[0] = roped(proj(OFF_AQ, 256), ca, sa, HEAD_DIM, HEAD_DIM ** -0.5).astype(BF16)
    ka_ref[0] = roped(proj(OFF_AK, 128), ca, sa, HEAD_DIM, 1.0).astype(BF16)
    va_ref[0] = proj(OFF_AV, 128).astype(BF16)
    rq_ref[0] = roped(proj(OFF_RQ, 256), cr, sr, HEAD_DIM, 1.0).astype(BF16)
    rk_ref[0] = roped(proj(OFF_RK, 256), cr, sr, HEAD_DIM, HEAD_DIM ** -0.5).astype(BF16)
    rv_ref[0] = proj(OFF_RV, 256).astype(BF16)
    rg_ref[0] = proj(OFF_RG, 256)
    dq = roped(proj(OFF_DQ, 256), cd, sd, DIFF_DIM, DIFF_DIM ** -0.5 * LOG2E)
    dqt_ref[0] = dq.T.astype(BF16)
    dk_ref[0] = roped(proj(OFF_DK, 256), cd, sd, DIFF_DIM, 1.0).astype(BF16)
    dvt = proj(OFF_DV, 256).T.astype(BF16)
    ones_rows = (lax.broadcasted_iota(jnp.int32, (DIFF_VROWS - HEAD_DIM, DIFF_TK), 0) == 0).astype(BF16)
    for j in range(tm // DIFF_TK):
        for hd in range(N_HEADS):
            dvt_ref[0, j, hd, 0:HEAD_DIM, :] = dvt[hd * HEAD_DIM:(hd + 1) * HEAD_DIM, j * DIFF_TK:(j + 1) * DIFF_TK]
            dvt_ref[0, j, hd, HEAD_DIM:DIFF_VROWS, :] = ones_rows
    lx_ref[0] = proj(OFF_LX, 256)
    ly_ref[0] = proj(OFF_LY, 256)


def _in_projection(x, g, sh, sc, w_in, tables):
    b, l, d = x.shape
    rope = tables is not None
    tm = _row_tile(l, ROWS_PROJ)
    nt = l // tm
    row = lambda shape: pl.BlockSpec(shape, lambda i, j: (i, j, 0))
    vec = pl.BlockSpec((1, 1, d), lambda i, j: (i, 0, 0))
    in_specs = [row((1, tm, d)), pl.BlockSpec((1, d), lambda i, j: (0, 0)), vec, vec,
                pl.BlockSpec((d, D_IN), lambda i, j: (0, 0))]
    args = [x, g.reshape(1, d), sh, sc, w_in]
    if rope:
        in_specs += [pl.BlockSpec((tm, LANE), lambda i, j: (j, 0))] * 6
        args += list(tables)
    sds = jax.ShapeDtypeStruct
    out_shape = [sds((b, l, 256), BF16), sds((b, l, 128), BF16), sds((b, l, 128), BF16),
                 sds((b, l, 256), BF16), sds((b, l, 256), BF16), sds((b, l, 256), BF16),
                 sds((b, l, 256), F32),
                 sds((b, 256, l), BF16), sds((b, l, 256), BF16),
                 sds((b, l // DIFF_TK, N_HEADS, DIFF_VROWS, DIFF_TK), BF16),
                 sds((b, l, 256), F32), sds((b, l, 256), F32)]
    out_specs = [row((1, tm, 256)), row((1, tm, 128)), row((1, tm, 128)),
                 row((1, tm, 256)), row((1, tm, 256)), row((1, tm, 256)), row((1, tm, 256)),
                 pl.BlockSpec((1, 256, tm), lambda i, j: (i, 0, j)), row((1, tm, 256)),
                 pl.BlockSpec((1, tm // DIFF_TK, N_HEADS, DIFF_VROWS, DIFF_TK), lambda i, j: (i, j, 0, 0, 0)),
                 row((1, tm, 256)), row((1, tm, 256))]
    return pl.pallas_call(
        functools.partial(_inproj_kernel, rope=rope, tm=tm),
        out_shape=out_shape, grid=(b, nt), in_specs=in_specs, out_specs=out_specs,
        compiler_params=_params("arbitrary", "arbitrary"),
        name="in_projection_rope" if rope else "in_projection_ctx",
    )(*args)


def _swa_kernel(*refs, window, tq, n):
    if window:
        sink_ref, q_ref, k_ref, v_ref, kc_ref, vc_ref, o_ref = refs
    else:
        sink_ref, q_ref, kc_ref, vc_ref, o_ref = refs
    kc = kc_ref[0]
    vc = vc_ref[0]
    m = kc.shape[0]
    lane = lax.broadcasted_iota(jnp.int32, (1, LANE), 1)

    def attend(q, kcat, vcat, valid):
        out_tiles = []
        for t in range(2):
            qt = q[:, t * LANE:(t + 1) * LANE]
            out_t = None
            for half in range(2):
                in_half = (lane >= half * HEAD_DIM) & (lane < (half + 1) * HEAD_DIM)
                sink = sink_ref[2 * t + half]
                qm = jnp.where(in_half, qt, jnp.zeros_like(qt))
                s = _dot_nt(qm, kcat)
                if valid is not None:
                    s = jnp.where(valid, s, -jnp.inf)
                mx = jnp.maximum(jnp.max(s, axis=1, keepdims=True), sink)
                e = jnp.exp(s - mx)
                den = jnp.sum(e, axis=1, keepdims=True) + jnp.exp(sink - mx)
                o = _dot(e.astype(BF16), vcat) / den
                out_t = o if out_t is None else jnp.where(in_half, o, out_t)
            out_tiles.append(out_t)
        return jnp.concatenate(out_tiles, axis=1)

    if not window:
        o_ref[0] = attend(q_ref[0], kc, vc, None)
        return
    w = tq + 2 * WINDOW
    t0 = pl.program_id(1) * tq
    start = pl.multiple_of(jnp.clip(t0 - WINDOW, 0, n - w), WINDOW)
    kcat = jnp.concatenate([k_ref[0, pl.ds(start, w), :], kc], axis=0)
    vcat = jnp.concatenate([v_ref[0, pl.ds(start, w), :], vc], axis=0)
    qpos = t0 + lax.broadcasted_iota(jnp.int32, (tq, w + m), 0)
    col = lax.broadcasted_iota(jnp.int32, (tq, w + m), 1)
    valid = (jnp.abs(qpos - (start + col)) <= WINDOW) | (col >= w)
    o_ref[0] = attend(q_ref[0], kcat, vcat, valid)


def _swa_attention(sink_perm, q, k, v, kc, vc):
    b, n, _ = q.shape
    m = kc.shape[1]
    tq = 256
    smem = pl.BlockSpec(memory_space=pltpu.SMEM)
    ctx = pl.BlockSpec((1, m, 128), lambda i, j: (i, 0, 0))
    in_specs = [smem, pl.BlockSpec((1, tq, 256), lambda i, j: (i, j, 0)),
                pl.BlockSpec((1, n, 128), lambda i, j: (i, 0, 0)),
                pl.BlockSpec((1, n, 128), lambda i, j: (i, 0, 0)), ctx, ctx]
    return pl.pallas_call(
        functools.partial(_swa_kernel, window=True, tq=tq, n=n),
        out_shape=jax.ShapeDtypeStruct((b, n, 256), F32),
        grid=(b, n // tq), in_specs=in_specs,
        out_specs=pl.BlockSpec((1, tq, 256), lambda i, j: (i, j, 0)),
        compiler_params=_params("arbitrary", "arbitrary"),
        name="swa_attention",
    )(sink_perm, q, k, v, kc, vc)


def _ctx_attention(sink_perm, qc, kc, vc):
    b, m, _ = qc.shape
    smem = pl.BlockSpec(memory_space=pltpu.SMEM)
    ctx = pl.BlockSpec((1, m, 128), lambda i: (i, 0, 0))
    return pl.pallas_call(
        functools.partial(_swa_kernel, window=False, tq=m, n=m),
        out_shape=jax.ShapeDtypeStruct((b, m, 256), F32),
        grid=(b,), in_specs=[smem, pl.BlockSpec((1, m, 256), lambda i: (i, 0, 0)), ctx, ctx],
        out_specs=pl.BlockSpec((1, m, 256), lambda i: (i, 0, 0)),
        compiler_params=_params("arbitrary"),
        name="ctx_attention",
    )(sink_perm, qc, kc, vc)


RET_CHUNK = 128


def _ret_kernel(dl_lane_ref, dl_head_ref, q_ref, k_ref, v_ref, s0_ref, y_ref, sfin_ref,
                s_scr, dm_scr, xi_scr, zeta_scr, *, ns, group):
    c = RET_CHUNK
    d = pl.program_id(1)
    si = pl.program_id(2)
    fwd = d == 0
    lane_head = lax.broadcasted_iota(jnp.int32, (1, GROUP), 1) // HEAD_DIM

    @pl.when(si == 0)
    def _():
        s_scr[...] = s0_ref[0, 0]
        lgl = -_softplus(-dl_lane_ref[0])
        row = lax.broadcasted_iota(jnp.int32, (c, 1), 0)
        xi_scr[...] = jnp.exp(lgl * jnp.where(fwd, row + 1, c - row).astype(F32))
        zeta_scr[...] = jnp.exp(lgl * jnp.where(fwd, c - 1 - row, row).astype(F32))
        ni = lax.broadcasted_iota(jnp.int32, (c, c), 0)
        mi = lax.broadcasted_iota(jnp.int32, (c, c), 1)
        dd = jnp.where(fwd, ni - mi, mi - ni)
        ddf = jnp.maximum(dd, 0).astype(F32)
        for h in range(N_HEADS):
            lgh = -_softplus(-dl_head_ref[0, h][0:1, :])
            dm_scr[h] = jnp.where(dd >= 0, jnp.exp(lgh * ddf), 0.0)

    dec = jnp.exp(-_softplus(-dl_lane_ref[0]) * float(c))
    rh = lax.broadcasted_iota(jnp.int32, (GROUP, GROUP), 0) // HEAD_DIM
    ch = lax.broadcasted_iota(jnp.int32, (GROUP, GROUP), 1) // HEAD_DIM
    same_head = rh == ch

    def chunk(g, carry):
        r0 = pl.multiple_of(jnp.where(fwd, g, group - 1 - g) * c, c)
        q = q_ref[0, pl.ds(r0, c), :]
        k = k_ref[0, pl.ds(r0, c), :]
        v = v_ref[0, pl.ds(r0, c), :]
        cross = _dot(q, s_scr[...].astype(BF16)) * xi_scr[...]
        inner = jnp.zeros((c, GROUP), F32)
        for h in range(N_HEADS):
            qh = jnp.where(lane_head == h, q, jnp.zeros_like(q))
            s = _dot_nt(qh, k) * dm_scr[h]
            ih = _dot(s.astype(BF16), v)
            inner = jnp.where(lane_head == h, ih, inner)
        y_ref[0, 0, pl.ds(r0, c), :] = inner + cross
        kz = (k.astype(F32) * zeta_scr[...]).T.astype(BF16)
        u = _dot(kz, v)
        s_scr[...] = jnp.where(same_head, dec * s_scr[...] + u, 0.0)
        return carry

    lax.fori_loop(0, group, chunk, 0, unroll=4 if group % 4 == 0 else 2)

    @pl.when(si == ns - 1)
    def _():
        sfin_ref[0, 0] = s_scr[...]


def _retention(dl_lane, dl_head, q, k, v, s0):
    b, l, _ = q.shape
    c = RET_CHUNK
    group = 8 if l % (8 * c) == 0 else 2
    ns = l // (group * c)
    sidx = lambda i, d, j: (i, jnp.where(d == 0, j, ns - 1 - j), 0)
    tok = pl.BlockSpec((1, group * c, 256), sidx)
    st = pl.BlockSpec((1, 1, 256, 256), lambda i, d, j: (i, d, 0, 0))
    return pl.pallas_call(
        functools.partial(_ret_kernel, ns=ns, group=group),
        out_shape=[jax.ShapeDtypeStruct((b, 2, l, 256), F32),
                   jax.ShapeDtypeStruct((b, 2, 256, 256), F32)],
        grid=(b, 2, ns),
        in_specs=[pl.BlockSpec((1, 1, 256), lambda i, d, j: (d, 0, 0)),
                  pl.BlockSpec((1, N_HEADS, 8, c), lambda i, d, j: (d, 0, 0, 0)),
                  tok, tok, tok, st],
        out_specs=[pl.BlockSpec((1, 1, group * c, 256),
                                lambda i, d, j: (i, d, jnp.where(d == 0, j, ns - 1 - j), 0)),
                   st],
        scratch_shapes=[pltpu.VMEM((GROUP, GROUP), F32),
                        pltpu.VMEM((N_HEADS, c, c), F32),
                        pltpu.VMEM((c, GROUP), F32),
                        pltpu.VMEM((c, GROUP), F32)],
        compiler_params=_params("arbitrary", "arbitrary", "arbitrary"),
        name="retention",
    )(dl_lane, dl_head, q, k, v, s0)


def _inner_tiles(tiles):
    for g in (8, 7, 6, 5, 4, 3, 2):
        if tiles % g == 0:
            return g
    return 1


def _diff_kernel(qt_ref, k_ref, vt_ref, o_ref, s_scr, *, tiles, nq):
    h = pl.program_id(1)
    inner = _inner_tiles(tiles)
    outer = tiles // inner
    total = nq * outer
    tq = DIFF_TQ
    rowi = lax.broadcasted_iota(jnp.int32, (LANE, 1), 0)
    hoff = (h % 2) * HEAD_DIM
    rowmask = [(rowi >= hoff + mp * DIFF_DIM) & (rowi < hoff + (mp + 1) * DIFF_DIM) for mp in range(2)]

    def step(mp, g_q, g_e, slot, m_e):
        mx = pv = None
        if g_q is not None:
            q0 = pl.multiple_of((g_q // outer) * tq, tq)
            qt = qt_ref[0, :, pl.ds(q0, tq)]
            qpad = jnp.where(rowmask[mp], qt, jnp.zeros_like(qt))
            kc = (g_q % outer) * inner
        if g_e is not None:
            vc = (g_e % outer) * inner
        for t in range(inner):
            sl = slice(t * DIFF_TK, (t + 1) * DIFF_TK)
            if g_q is not None:
                base = pl.multiple_of((kc + t) * DIFF_TK, DIFF_TK)
                s = _dot(k_ref[0, pl.ds(base, DIFF_TK), :], qpad)
                s_scr[mp, 1 - slot, sl, :] = s
                cm = jnp.max(s, axis=0, keepdims=True)
                mx = cm if mx is None else jnp.maximum(mx, cm)
            if g_e is not None:
                p = jnp.exp2(s_scr[mp, slot, sl, :] - m_e).astype(BF16)
                d = _dot(vt_ref[0, vc + t, 0], p)
                pv = d if pv is None else pv + d
        return mx, pv

    def advance(mp, g_q, g_e, slot, state):
        m_e, alpha, acc = state
        mx, pv = step(mp, g_q, g_e, slot, m_e)
        acc = alpha * acc + pv
        o0 = pl.multiple_of((g_e // outer) * tq, tq)
        o_ref[0, 0, mp, :, pl.ds(o0, tq)] = acc
        if g_q is None:
            return m_e, alpha, acc
        m_base = jnp.where(g_q % outer == 0, -jnp.inf, m_e)
        m_next = jnp.maximum(m_base, mx)
        return m_next, jnp.exp2(m_base - m_next), acc

    zeros = jnp.zeros((DIFF_VROWS, tq), F32)
    states = []
    for mp in range(2):
        m_first, _ = step(mp, 0, None, 1, None)
        states.append((m_first, jnp.zeros((1, tq), F32), zeros))
    states = tuple(states)

    def body(i, carry):
        carry = tuple(advance(mp, 2 * i + 1, 2 * i, 0, carry[mp]) for mp in range(2))
        return tuple(advance(mp, 2 * i + 2, 2 * i + 1, 1, carry[mp]) for mp in range(2))

    last = total - 1
    states = lax.fori_loop(0, last // 2, body, states, unroll=min(4, max(last // 2, 1)))
    if last % 2 == 1:
        states = tuple(advance(mp, last, last - 1, 0, states[mp]) for mp in range(2))
    for mp in range(2):
        advance(mp, None, last, last % 2, states[mp])


def _diff_attention(qt, k_all, vt_all):
    b, _, lq = qt.shape
    lk = k_all.shape[1]
    tiles = lk // DIFF_TK
    nq = lq // DIFF_TQ
    return pl.pallas_call(
        functools.partial(_diff_kernel, tiles=tiles, nq=nq),
        out_shape=jax.ShapeDtypeStruct((b, N_HEADS, 2, DIFF_VROWS, lq), F32),
        grid=(b, N_HEADS),
        in_specs=[pl.BlockSpec((1, LANE, lq), lambda i, h: (i, h // 2, 0)),
                  pl.BlockSpec((1, lk, LANE), lambda i, h: (i, 0, h // 2)),
                  pl.BlockSpec((1, tiles, 1, DIFF_VROWS, DIFF_TK), lambda i, h: (i, 0, h, 0, 0))],
        out_specs=pl.BlockSpec((1, 1, 2, DIFF_VROWS, lq), lambda i, h: (i, h, 0, 0, 0)),
        scratch_shapes=[pltpu.VMEM((2, 2, _inner_tiles(tiles) * DIFF_TK, DIFF_TQ), F32)],
        compiler_params=_params("arbitrary", "arbitrary"),
        name="diff_attention",
    )(qt, k_all, vt_all)


def _lru_kernel(x_ref, xp_ref, xn_ref, cw_ref, cb_ref, wa_ref, ba_ref, wx_ref, bx_ref, lam_ref,
                h0_ref, h_ref, carry, *, reverse, tm, nt):
    j = pl.program_id(1)
    tix = (nt - 1 - j) if reverse else j

    @pl.when(j == 0)
    def _():
        carry[...] = jnp.broadcast_to(h0_ref[0], carry.shape)

    x = x_ref[0]
    row = lax.broadcasted_iota(jnp.int32, (tm, 1), 0)
    prev = jnp.where(tix > 0, xp_ref[0][7:8, :], 0.0)
    nxt0 = jnp.where(tix < nt - 1, xn_ref[0][0:1, :], 0.0)
    nxt1 = jnp.where(tix < nt - 1, xn_ref[0][1:2, :], 0.0)
    xm1 = jnp.where(row == 0, prev, pltpu.roll(x, 1, axis=0))
    xp1 = jnp.where(row == tm - 1, nxt0, pltpu.roll(x, tm - 1, axis=0))
    xp2 = jnp.where(row == tm - 1, nxt1, jnp.where(row == tm - 2, nxt0, pltpu.roll(x, tm - 2, axis=0)))
    cw = cw_ref[...]
    u = cw[0:1] * xm1 + cw[1:2] * x + cw[2:3] * xp1 + cw[3:4] * xp2 + cb_ref[...]
    ub = u.astype(BF16)
    r = jax.nn.sigmoid(_dot(ub, wa_ref[0]) + ba_ref[0])
    gi = jax.nn.sigmoid(_dot(ub, wx_ref[0]) + bx_ref[0])
    log_a = -LRU_C * r * _softplus(-lam_ref[0])
    a = jnp.exp(log_a)
    th = jnp.tanh(log_a)
    bv = jnp.sqrt(-2.0 * th / (1.0 - th)) * (gi * u)
    k = 1
    while k < tm:
        if k < 8:
            keep = (row < tm - k) if reverse else (row >= k)
            shift = (tm - k) if reverse else k
            a_sh = jnp.where(keep, pltpu.roll(a, shift, axis=0), 1.0)
            b_sh = jnp.where(keep, pltpu.roll(bv, shift, axis=0), 0.0)
            bv = a * b_sh + bv
            a = a * a_sh
        elif reverse:
            head_b = a[:tm - k] * bv[k:] + bv[:tm - k]
            head_a = a[:tm - k] * a[k:]
            bv = jnp.concatenate([head_b, bv[tm - k:]], axis=0)
            a = jnp.concatenate([head_a, a[tm - k:]], axis=0)
        else:
            tail_b = a[k:] * bv[:tm - k] + bv[k:]
            tail_a = a[k:] * a[:tm - k]
            bv = jnp.concatenate([bv[:k], tail_b], axis=0)
            a = jnp.concatenate([a[:k], tail_a], axis=0)
        k *= 2
    hcur = a * carry[0:1, :] + bv
    h_ref[0] = hcur
    last = hcur[0:1, :] if reverse else hcur[tm - 1:tm, :]
    carry[...] = jnp.broadcast_to(last, carry.shape)


def _lru_scan(lx, conv_w, conv_b, wa_bd, ba, wx_bd, bx, lam, h0, reverse):
    b, l, w = lx.shape
    tm = _row_tile(l, ROWS_PROJ)
    nt = l // tm
    tb = tm // 8
    tidx = (lambda j: nt - 1 - j) if reverse else (lambda j: j)
    vec = pl.BlockSpec((1, w), lambda i, j: (0, 0))
    mat = pl.BlockSpec((1, w, w), lambda i, j: (0, 0, 0))
    return pl.pallas_call(
        functools.partial(_lru_kernel, reverse=reverse, tm=tm, nt=nt),
        out_shape=jax.ShapeDtypeStruct((b, l, w), F32),
        grid=(b, nt),
        in_specs=[pl.BlockSpec((1, tm, w), lambda i, j: (i, tidx(j), 0)),
                  pl.BlockSpec((1, 8, w), lambda i, j: (i, jnp.maximum(tidx(j) * tb - 1, 0), 0)),
                  pl.BlockSpec((1, 8, w), lambda i, j: (i, jnp.minimum((tidx(j) + 1) * tb, l // 8 - 1), 0)),
                  pl.BlockSpec((4, w), lambda i, j: (0, 0)), vec,
                  mat, vec, mat, vec, vec,
                  pl.BlockSpec((1, 1, w), lambda i, j: (i, 0, 0))],
        out_specs=pl.BlockSpec((1, tm, w), lambda i, j: (i, tidx(j), 0)),
        scratch_shapes=[pltpu.VMEM((8, w), F32)],
        compiler_params=_params("arbitrary", "arbitrary"),
        name="lru_scan_rev" if reverse else "lru_scan_fwd",
    )(lx, lx, lx, conv_w, conv_b.reshape(1, w), wa_bd.reshape(1, w, w), ba.reshape(1, w),
      wx_bd.reshape(1, w, w), bx.reshape(1, w), lam.reshape(1, w), h0)


def _outproj_kernel(x_ref, gt_ref, oa_ref, ry_ref, rg_ref, rgn_ref, lam_ref, dgn_ref, draw_ref, hf_ref, hb_ref,
                    ly_ref, bd_ref, w_ref, o_ref, *, lam_init):
    oa = oa_ref[0]
    y = ry_ref[0, 0] + ry_ref[0, 1]
    y2 = y * y
    hi = y2.astype(BF16)
    lo = (y2 - hi.astype(F32)).astype(BF16)
    ms = _dot(hi, bd_ref[...]) + _dot(lo, bd_ref[...])
    ob = y * lax.rsqrt(ms + NORM_EPS) * rgn_ref[...] * _silu(rg_ref[0])
    lv = lam_ref[...]
    lam = (jnp.exp(jnp.sum(lv[0:1] * lv[1:2], axis=1, keepdims=True))
           - jnp.exp(jnp.sum(lv[2:3] * lv[3:4], axis=1, keepdims=True)) + lam_init)
    heads = []
    for hd in range(N_HEADS):
        a0 = draw_ref[0, hd, 0]
        a1 = draw_ref[0, hd, 1]
        oh = (a0[0:HEAD_DIM] / a0[HEAD_DIM:HEAD_DIM + 1]
              - lam * (a1[0:HEAD_DIM] / a1[HEAD_DIM:HEAD_DIM + 1]))
        msd = jnp.mean(oh * oh, axis=0, keepdims=True)
        heads.append(oh * lax.rsqrt(msd + NORM_EPS) * (dgn_ref[hd] * (1.0 - lam_init)))
    od = jnp.concatenate(heads, axis=0).T
    ol = (hf_ref[0] + hb_ref[0]) * jax.nn.gelu(ly_ref[0])
    acc = _dot(oa.astype(BF16), w_ref[0:256, :])
    acc = acc + _dot(ob.astype(BF16), w_ref[256:512, :])
    acc = acc + _dot(od.astype(BF16), w_ref[512:768, :])
    acc = acc + _dot(ol.astype(BF16), w_ref[768:1024, :])
    o_ref[0] = x_ref[0] + gt_ref[0] * acc


def _out_projection(x, gt, oa, ry, rg, ret_gn, lam_vecs, diff_gn, lam_init, draw, hf, hb, ly, w_out):
    b, l, d = x.shape
    tm = _row_tile(l, ROWS_PROJ)
    row = lambda wdt: pl.BlockSpec((1, tm, wdt), lambda i, j: (i, j, 0))
    hd = jnp.arange(GROUP) // HEAD_DIM
    bd = jnp.where(hd[:, None] == hd[None, :], 1.0 / HEAD_DIM, 0.0).astype(BF16)
    return pl.pallas_call(
        functools.partial(_outproj_kernel, lam_init=lam_init),
        out_shape=jax.ShapeDtypeStruct((b, l, d), F32),
        grid=(b, l // tm),
        in_specs=[row(d), pl.BlockSpec((1, 1, d), lambda i, j: (i, 0, 0)), row(256),
                  pl.BlockSpec((1, 2, tm, 256), lambda i, j: (i, 0, j, 0)), row(256),
                  pl.BlockSpec((1, 256), lambda i, j: (0, 0)),
                  pl.BlockSpec((4, DIFF_DIM), lambda i, j: (0, 0)),
                  pl.BlockSpec((N_HEADS, HEAD_DIM, 1), lambda i, j: (0, 0, 0)),
                  pl.BlockSpec((1, N_HEADS, 2, DIFF_VROWS, tm), lambda i, j: (i, 0, 0, 0, j)),
                  row(256), row(256), row(256),
                  pl.BlockSpec((256, 256), lambda i, j: (0, 0)),
                  pl.BlockSpec((d, d), lambda i, j: (0, 0))],
        out_specs=row(d),
        compiler_params=_params("arbitrary", "arbitrary"),
        name="out_projection",
    )(x, gt, oa, ry, rg, ret_gn.reshape(1, 256), lam_vecs, diff_gn.reshape(N_HEADS, HEAD_DIM, 1), draw,
      hf, hb, ly, bd, w_out)


def _norm_modulate(x_ref, g_ref, sh_ref, sc_ref):
    x = x_ref[0]
    ms = jnp.mean(x * x, axis=-1, keepdims=True)
    y = x * lax.rsqrt(ms + NORM_EPS) * g_ref[...]
    return y * (1.0 + sc_ref[0]) + sh_ref[0]


def _ffn_kernel(x_ref, g_ref, sh_ref, sc_ref, gt_ref, w1_ref, w3_ref, w2_ref, o_ref, hb_scr, acc_scr,
                *, n_f):
    f = pl.program_id(2)

    @pl.when(f == 0)
    def _():
        hb_scr[...] = _norm_modulate(x_ref, g_ref, sh_ref, sc_ref).astype(BF16)
        acc_scr[...] = jnp.zeros_like(acc_scr)

    hb = hb_scr[...]
    a = _silu(_dot(hb, w1_ref[...])) * _dot(hb, w3_ref[...])
    acc_scr[...] += _dot(a.astype(BF16), w2_ref[...])

    @pl.when(f == n_f - 1)
    def _():
        o_ref[0] = x_ref[0] + gt_ref[0] * acc_scr[...]


def _ffn(x, g, sh, sc, gt, w1, w3, w2):
    b, l, d = x.shape
    ff = w1.shape[1]
    tm = _row_tile(l, ROWS_FFN)
    n_f = 2
    fc = ff // n_f
    vec = pl.BlockSpec((1, 1, d), lambda i, j, f: (i, 0, 0))
    return pl.pallas_call(
        functools.partial(_ffn_kernel, n_f=n_f),
        out_shape=jax.ShapeDtypeStruct((b, l, d), F32),
        grid=(b, l // tm, n_f),
        in_specs=[pl.BlockSpec((1, tm, d), lambda i, j, f: (i, j, 0)),
                  pl.BlockSpec((1, d), lambda i, j, f: (0, 0)), vec, vec, vec,
                  pl.BlockSpec((d, fc), lambda i, j, f: (0, f)),
                  pl.BlockSpec((d, fc), lambda i, j, f: (0, f)),
                  pl.BlockSpec((fc, d), lambda i, j, f: (f, 0))],
        out_specs=pl.BlockSpec((1, tm, d), lambda i, j, f: (i, j, 0)),
        scratch_shapes=[pltpu.VMEM((tm, d), BF16), pltpu.VMEM((tm, d), F32)],
        compiler_params=_params("arbitrary", "arbitrary", "arbitrary"),
        name="dense_ffn",
    )(x, g.reshape(1, d), sh, sc, gt, w1, w3, w2)


MOE_RT = 128


def _moe_kernel(x_ref, g_ref, sh_ref, sc_ref, gt_ref, fg_ref, rt_ref, w1_ref, w3_ref, w2_ref, o_ref,
                hb_scr, gate_scr, pos_scr, gatet_scr, post_scr, x_scr, y_scr, gs_scr, cnt_smem,
                *, n_e, n_f, tm, final_norm):
    e = pl.program_id(2)
    f = pl.program_id(3)
    rt2 = 2 * MOE_RT

    @pl.when((e == 0) & (f == 0))
    def _():
        h = _norm_modulate(x_ref, g_ref, sh_ref, sc_ref)
        hb = h.astype(BF16)
        hb_scr[...] = hb
        o_ref[0] = jnp.zeros(o_ref.shape[1:], F32)
        rt = rt_ref[...]
        rhi = rt.astype(BF16)
        rlo = (rt - rhi.astype(F32)).astype(BF16)
        hlo = (h - hb.astype(F32)).astype(BF16)
        logits = _dot(hb, rhi) + _dot(hb, rlo) + _dot(hlo, rhi)
        lane = lax.broadcasted_iota(jnp.int32, logits.shape, 1)
        logits = jnp.where(lane < N_EXPERTS, logits, -jnp.inf)
        m1 = jnp.max(logits, axis=1, keepdims=True)
        i1 = jnp.min(jnp.where(logits == m1, lane, LANE), axis=1, keepdims=True)
        rest = jnp.where(lane == i1, -jnp.inf, logits)
        m2 = jnp.max(rest, axis=1, keepdims=True)
        i2 = jnp.min(jnp.where(rest == m2, lane, LANE), axis=1, keepdims=True)
        e2 = jnp.exp(m2 - m1)
        den = 1.0 + e2
        gate = jnp.where(lane == i1, 1.0 / den, 0.0) + jnp.where(lane == i2, e2 / den, 0.0)
        gate_scr[...] = gate
        sel = jnp.where(gate > 0.0, 1.0, 0.0)
        selb = sel.astype(BF16)
        strip = 256
        for r0 in range(0, tm, strip):
            tok_r = r0 + lax.broadcasted_iota(jnp.int32, (strip, tm), 0)
            tok_c = lax.broadcasted_iota(jnp.int32, (strip, tm), 1)
            earlier = jnp.where(tok_c < tok_r, 1.0, 0.0).astype(BF16)
            pos_scr[r0:r0 + strip, :] = _dot(earlier, selb)
        post_scr[...] = pos_scr[...].T
        gatet_scr[...] = gate.T
        for ee in range(n_e):
            cnt_smem[ee] = jnp.sum(jnp.where(lane == ee, sel, 0.0)).astype(jnp.int32)

    n_tiles = (cnt_smem[e] + (MOE_RT - 1)) // MOE_RT
    n_pairs = (n_tiles + 1) // 2

    @pl.when(f == 0)
    def _():
        sub = lax.broadcasted_iota(jnp.int32, (8, 1), 0)
        pos_e = jnp.sum(jnp.where(sub == e, post_scr[0:8, :], 0.0), axis=0, keepdims=True)
        gate_e = jnp.sum(jnp.where(sub == e, gatet_scr[0:8, :], 0.0), axis=0, keepdims=True)

        def gather(jj, carry):
            r0 = pl.multiple_of(jj * rt2, rt2)
            slot = (r0 + lax.broadcasted_iota(jnp.int32, (rt2, 1), 0)).astype(F32)
            hit = (pos_e == slot) & (gate_e > 0.0)
            onehot = jnp.where(hit, 1.0, 0.0).astype(BF16)
            x_scr[pl.ds(r0, rt2), :] = _dot(onehot, hb_scr[...]).astype(BF16)
            gs = jnp.sum(jnp.where(hit, gate_e, 0.0), axis=1, keepdims=True)
            gs_scr[pl.ds(r0, rt2), :] = jnp.broadcast_to(gs, (rt2, LANE))
            y_scr[pl.ds(r0, rt2), :] = jnp.zeros((rt2, y_scr.shape[1]), F32)
            return carry

        lax.fori_loop(0, n_pairs, gather, 0)

    def expert(r0, rows):
        xj = x_scr[pl.ds(r0, rows), :]
        a = _silu(_dot(xj, w1_ref[0])) * _dot(xj, w3_ref[0])
        y_scr[pl.ds(r0, rows), :] += _dot(a.astype(BF16), w2_ref[0])

    def expert_pair(jj, carry):
        expert(pl.multiple_of(jj * rt2, rt2), rt2)
        return carry

    lax.fori_loop(0, n_tiles // 2, expert_pair, 0)

    @pl.when(n_tiles % 2 == 1)
    def _():
        expert(pl.multiple_of((n_tiles - 1) * MOE_RT, MOE_RT), MOE_RT)

    @pl.when(f == n_f - 1)
    def _():
        lane = lax.broadcasted_iota(jnp.int32, (1, LANE), 1)
        pos_col = jnp.sum(jnp.where(lane == e, pos_scr[...], 0.0), axis=1, keepdims=True)
        gate_col = jnp.sum(jnp.where(lane == e, gate_scr[...], 0.0), axis=1, keepdims=True)

        def scatter(jj, carry):
            r0 = pl.multiple_of(jj * rt2, rt2)
            slot = (r0 + lax.broadcasted_iota(jnp.int32, (1, rt2), 1)).astype(F32)
            onehot = jnp.where((pos_col == slot) & (gate_col > 0.0), 1.0, 0.0).astype(BF16)
            ys = (y_scr[pl.ds(r0, rt2), :] * gs_scr[pl.ds(r0, rt2), 0:1]).astype(BF16)
            o_ref[0] += _dot(onehot, ys)
            return carry

        lax.fori_loop(0, n_pairs, scatter, 0)

    @pl.when((e == n_e - 1) & (f == n_f - 1))
    def _():
        out = x_ref[0] + gt_ref[0] * o_ref[0]
        if final_norm:
            ms = jnp.mean(out * out, axis=-1, keepdims=True)
            out = out * lax.rsqrt(ms + NORM_EPS) * fg_ref[...]
        o_ref[0] = out


def _moe_ffn(x, g, sh, sc, gt, final_g, w1, w3, w2, router, final_norm):
    b, l, d = x.shape
    n_e, _, ff = w1.shape
    tm = _row_tile(l, ROWS_FFN)
    n_f = 2
    fc = ff // n_f
    vec = pl.BlockSpec((1, 1, d), lambda i, j, e, f: (i, 0, 0))
    return pl.pallas_call(
        functools.partial(_moe_kernel, n_e=n_e, n_f=n_f, tm=tm, final_norm=final_norm),
        out_shape=jax.ShapeDtypeStruct((b, l, d), F32),
        grid=(b, l // tm, n_e, n_f),
        in_specs=[pl.BlockSpec((1, tm, d), lambda i, j, e, f: (i, j, 0)),
                  pl.BlockSpec((1, d), lambda i, j, e, f: (0, 0)), vec, vec, vec,
                  pl.BlockSpec((1, d), lambda i, j, e, f: (0, 0)),
                  pl.BlockSpec((d, LANE), lambda i, j, e, f: (0, 0)),
                  pl.BlockSpec((1, d, fc), lambda i, j, e, f: (e, 0, f)),
                  pl.BlockSpec((1, d, fc), lambda i, j, e, f: (e, 0, f)),
                  pl.BlockSpec((1, fc, d), lambda i, j, e, f: (e, f, 0))],
        out_specs=pl.BlockSpec((1, tm, d), lambda i, j, e, f: (i, j, 0)),
        scratch_shapes=[pltpu.VMEM((tm, d), BF16),
                        pltpu.VMEM((tm, LANE), F32),
                        pltpu.VMEM((tm, LANE), F32),
                        pltpu.VMEM((LANE, tm), F32),
                        pltpu.VMEM((LANE, tm), F32),
                        pltpu.VMEM((tm, d), BF16),
                        pltpu.VMEM((tm, d), F32),
                        pltpu.VMEM((tm, LANE), F32),
                        pltpu.SMEM((n_e,), jnp.int32)],
        compiler_params=_params("arbitrary", "arbitrary", "arbitrary", "arbitrary"),
        name="moe_ffn",
    )(x, g.reshape(1, d), sh, sc, gt, final_g.reshape(1, d), router, w1, w3, w2)


def _rope_tables(n):
    rows = n // GRID_W
    row = jnp.repeat(jnp.arange(rows, dtype=F32), GRID_W)
    col = jnp.tile(jnp.arange(GRID_W, dtype=F32), rows)

    def axial(dim):
        nf = dim // 4
        inv = ROPE_BASE ** (-jnp.arange(nf, dtype=F32) / nf)
        return jnp.concatenate([row[:, None] * inv, col[:, None] * inv], axis=-1)

    def line(dim):
        nf = dim // 2
        inv = ROPE_BASE ** (-jnp.arange(nf, dtype=F32) / nf)
        return jnp.arange(n, dtype=F32)[:, None] * inv

    def expand(ang, dim):
        c = jnp.cos(ang)
        s = jnp.sin(ang)
        c = jnp.tile(jnp.concatenate([c, c], axis=-1), (1, LANE // dim))
        s = jnp.tile(jnp.concatenate([-s, s], axis=-1), (1, LANE // dim))
        return c, s

    ca, sa = expand(axial(HEAD_DIM), HEAD_DIM)
    cr, sr = expand(line(HEAD_DIM), HEAD_DIM)
    cd, sd = expand(axial(DIFF_DIM), DIFF_DIM)
    return ca, sa, cr, sr, cd, sd


_ATTN_HEAD_ORDER = (0, 2, 1, 3)


def _attn_perm():
    return jnp.concatenate([jnp.arange(HEAD_DIM) + HEAD_DIM * h for h in _ATTN_HEAD_ORDER])


def _block_diag(w):
    nb, blk, _ = w.shape
    out = jnp.zeros((nb * blk, nb * blk), w.dtype)
    for i in range(nb):
        out = out.at[i * blk:(i + 1) * blk, i * blk:(i + 1) * blk].set(w[i])
    return out


def kernel(x, c, ctx, c_ctx, w_ada, b_ada, g_norm1, g_norm2, w_in, w_out, attn_sink, ret_decay_logit,
           ret_gn, diff_lambda, diff_gn, conv_w, conv_b, lru_wa, lru_ba, lru_wx, lru_bx, lru_lambda,
           ffn_w1, ffn_w3, ffn_w2, moe_router, moe_w1, moe_w3, moe_w2, final_norm):
    b, n, d = x.shape
    m = ctx.shape[1]
    tables = _rope_tables(n)
    perm = _attn_perm()

    cc = jnp.concatenate([c, c_ctx[None, :], jnp.zeros((8 - b - 1, d), F32)], axis=0)
    mods = _mod_vectors(cc, w_ada, b_ada)

    xc = ctx
    for layer in range(DEPTH):
        need_ctx = layer < DEPTH - 1
        lam_init = 0.8 - 0.6 * math.exp(-0.3 * layer)
        mod = mods[layer]
        sh1, sc1, gt1, sh2, sc2, gt2 = [mod[:b, None, i * d:(i + 1) * d] for i in range(6)]
        sh1c, sc1c, gt1c, sh2c, sc2c, gt2c = [
            jnp.broadcast_to(mod[b:b + 1, None, i * d:(i + 1) * d], (b, 1, d)) for i in range(6)]

        wi = w_in[layer]
        wi = jnp.concatenate([wi[:, :256][:, perm], wi[:, 256:]], axis=1).astype(BF16)
        wo = w_out[layer]
        wo = jnp.concatenate([wo[:256][perm], wo[256:]], axis=0).astype(BF16)
        sink_perm = attn_sink[layer][jnp.array(_ATTN_HEAD_ORDER)]

        (qa, ka, va, rq, rk, rv, rg, dqt, dk, dvt, lx, ly) = _in_projection(
            x, g_norm1[layer], sh1, sc1, wi, tables)
        (qac, kac, vac, rqc, rkc, rvc, rgc, dqtc, dkc, dvtc, lxc, lyc) = _in_projection(
            xc, g_norm1[layer], sh1c, sc1c, wi, None)

        oa = _swa_attention(sink_perm, qa, ka, va, kac, vac)
        dl = ret_decay_logit[layer]
        dl_lane = jnp.repeat(dl, HEAD_DIM, axis=1).reshape(2, 1, GROUP)
        dl_head = jnp.broadcast_to(dl[:, :, None, None], (2, N_HEADS, 8, RET_CHUNK))
        ryc, s_ctx = _retention(dl_lane, dl_head, rqc, rkc, rvc, jnp.zeros((b, 2, GROUP, GROUP), F32))
        ry, _ = _retention(dl_lane, dl_head, rq, rk, rv, s_ctx)
        k_all = jnp.concatenate([dk, dkc], axis=1)
        vt_all = jnp.concatenate([dvt, dvtc], axis=1)
        draw = _diff_attention(dqt, k_all, vt_all)
        assemble = functools.partial(_out_projection, ret_gn=ret_gn[layer], lam_vecs=diff_lambda[layer],
                                     diff_gn=diff_gn[layer], lam_init=lam_init, w_out=wo)
        lru = []
        lru_c = []
        for direction in range(2):
            prm = (conv_w[layer], conv_b[layer],
                   _block_diag(lru_wa[layer, direction]).astype(BF16), lru_ba[layer, direction],
                   _block_diag(lru_wx[layer, direction]).astype(BF16), lru_bx[layer, direction],
                   lru_lambda[layer, direction])
            rev = direction == 1
            hc = _lru_scan(lxc, *prm, jnp.zeros((b, 1, GROUP), F32), rev)
            h0 = hc[:, 0:1, :] if rev else hc[:, m - 1:m, :]
            lru.append(_lru_scan(lx, *prm, h0, rev))
            lru_c.append(hc)

        x = assemble(x, gt1, oa, ry, rg, draw=draw, hf=lru[0], hb=lru[1], ly=ly)
        j = layer // 2
        if layer % 2 == 0:
            ffn = functools.partial(_ffn, w1=ffn_w1[j].astype(BF16), w3=ffn_w3[j].astype(BF16),
                                    w2=ffn_w2[j].astype(BF16))
            x = ffn(x, g_norm2[layer], sh2, sc2, gt2)
        else:
            moe = functools.partial(_moe_ffn, final_g=final_norm, w1=moe_w1[j].astype(BF16),
                                    w3=moe_w3[j].astype(BF16), w2=moe_w2[j].astype(BF16),
                                    router=jnp.pad(moe_router[j], ((0, 0), (0, LANE - N_EXPERTS))))
            x = moe(x, g_norm2[layer], sh2, sc2, gt2, final_norm=layer == DEPTH - 1)
            ffn = functools.partial(moe, final_norm=False)

        if need_ctx:
            oac = _ctx_attention(sink_perm, qac, kac, vac)
            drawc = _diff_attention(dqtc, dkc, dvtc)
            xc = assemble(xc, gt1c, oac, ryc, rgc, draw=drawc, hf=lru_c[0], hb=lru_c[1], ly=lyc)
            xc = ffn(xc, g_norm2[layer], sh2c, sc2c, gt2c)
    assert DEPTH % 2 == 0, "the closing RMSNorm is fused into the last (routed) layer"
    return x
```

```python
import functools
import math

import jax
import jax.numpy as jnp
from jax import lax
from jax.experimental import pallas as pl
from jax.experimental.pallas import tpu as pltpu

F32 = jnp.float32
BF16 = jnp.bfloat16

D_MODEL = 1024
DEPTH = 2
GRID_W = 64
HEAD_DIM = 64
GROUP = D_MODEL // 4
N_HEADS = GROUP // HEAD_DIM
DIFF_DIM = HEAD_DIM // 2
WINDOW = 128
NORM_EPS = 1e-6
ROPE_BASE = 10000.0
LRU_BLOCKS = 4
LRU_C = 8.0
N_EXPERTS = 8
LANE = 128
LOG2E = 1.4426950408889634
DIFF_TQ = 256
DIFF_TK = 256
DIFF_VROWS = HEAD_DIM + 16

OFF_AQ, OFF_AK, OFF_AV = 0, 256, 384
OFF_RQ, OFF_RK, OFF_RV, OFF_RG = 512, 768, 1024, 1280
OFF_DQ, OFF_DK, OFF_DV = 1536, 1792, 2048
OFF_LX, OFF_LY = 2304, 2560
D_IN = 2816

VMEM_LIMIT = 56 * 1024 * 1024
ROWS_PROJ = 512
ROWS_FFN = 1024
ROWS_MIN = 256


def _row_tile(l, preferred):
    tm = preferred if l % preferred == 0 else ROWS_MIN
    assert l % tm == 0, (l, tm)
    return tm


def _params(*sem):
    return pltpu.CompilerParams(dimension_semantics=sem, vmem_limit_bytes=VMEM_LIMIT)


def _silu(x):
    return x * jax.nn.sigmoid(x)


def _softplus(x):
    return jnp.maximum(x, 0.0) + jnp.log(1.0 + jnp.exp(-jnp.abs(x)))


def _dot(a, b):
    return jnp.dot(a, b, preferred_element_type=F32)


def _dot_nt(a, b):
    return lax.dot_general(a, b, (((1,), (1,)), ((), ())), preferred_element_type=F32)


def _mod_kernel(c_ref, w_ref, b_ref, o_ref):
    s = _silu(c_ref[...])
    o_ref[0] = _dot(s.astype(BF16), w_ref[0].astype(BF16)) + b_ref[0]


def _mod_vectors(cc, w_ada, b_ada):
    depth, d, n6 = w_ada.shape
    tn = 1536
    return pl.pallas_call(
        _mod_kernel,
        out_shape=jax.ShapeDtypeStruct((depth, 8, n6), F32),
        grid=(depth, n6 // tn),
        in_specs=[pl.BlockSpec((8, d), lambda l, j: (0, 0)),
                  pl.BlockSpec((1, d, tn), lambda l, j: (l, 0, j)),
                  pl.BlockSpec((1, 1, tn), lambda l, j: (l, 0, j))],
        out_specs=pl.BlockSpec((1, 8, tn), lambda l, j: (l, 0, j)),
        compiler_params=_params("arbitrary", "arbitrary"),
        name="mod_vectors",
    )(cc, w_ada, b_ada.reshape(depth, 1, n6))


def _rope_tile(x, cos, sin_signed, d):
    lane = lax.broadcasted_iota(jnp.int32, x.shape, 1)
    first = (lane & (d - 1)) < (d // 2)
    partner = jnp.where(first, pltpu.roll(x, LANE - d // 2, axis=1), pltpu.roll(x, d // 2, axis=1))
    return x * cos + partner * sin_signed


def _inproj_kernel(*refs, rope, tm):
    if rope:
        (x_ref, g_ref, sh_ref, sc_ref, w_ref, ca_ref, sa_ref, cr_ref, sr_ref, cd_ref, sd_ref,
         qa_ref, ka_ref, va_ref, rq_ref, rk_ref, rv_ref, rg_ref,
         dqt_ref, dk_ref, dvt_ref, lx_ref, ly_ref) = refs
    else:
        (x_ref, g_ref, sh_ref, sc_ref, w_ref,
         qa_ref, ka_ref, va_ref, rq_ref, rk_ref, rv_ref, rg_ref,
         dqt_ref, dk_ref, dvt_ref, lx_ref, ly_ref) = refs
    x = x_ref[0]
    ms = jnp.mean(x * x, axis=-1, keepdims=True)
    y = x * lax.rsqrt(ms + NORM_EPS) * g_ref[...]
    h = y * (1.0 + sc_ref[0]) + sh_ref[0]
    hb = h.astype(BF16)

    def proj(off, width):
        return _dot(hb, w_ref[:, off:off + width])

    def roped(p, c_ref, s_ref, d, scale):
        tiles = []
        for t in range(p.shape[1] // LANE):
            pt = p[:, t * LANE:(t + 1) * LANE]
            if rope:
                pt = _rope_tile(pt, c_ref[...], s_ref[...], d)
            tiles.append(pt if scale == 1.0 else pt * scale)
        return tiles[0] if len(tiles) == 1 else jnp.concatenate(tiles, axis=1)

    ca = sa = cr = sr = cd = sd = None
    if rope:
        ca, sa, cr, sr, cd, sd = ca_ref, sa_ref, cr_ref, sr_ref, cd_ref, sd_ref
    qa_ref[0] = roped(proj(OFF_AQ, 256), ca, sa, HEAD_DIM, HEAD_DIM ** -0.5).astype(BF16)
    ka_ref[0] = roped(proj(OFF_AK, 128), ca, sa, HEAD_DIM, 1.0).astype(BF16)
    va_ref[0] = proj(OFF_AV, 128).astype(BF16)
    rq_ref[0] = roped(proj(OFF_RQ, 256), cr, sr, HEAD_DIM, 1.0).astype(BF16)
    rk_ref[0] = roped(proj(OFF_RK, 256), cr, sr, HEAD_DIM, HEAD_DIM ** -0.5).astype(BF16)
    rv_ref[0] = proj(OFF_RV, 256).astype(BF16)
    rg_ref[0] = proj(OFF_RG, 256)
    dq = roped(proj(OFF_DQ, 256), cd, sd, DIFF_DIM, DIFF_DIM ** -0.5 * LOG2E)
    dqt_ref[0] = dq.T.astype(BF16)
    dk_ref[0] = roped(proj(OFF_DK, 256), cd, sd, DIFF_DIM, 1.0).astype(BF16)
    dvt = proj(OFF_DV, 256).T.astype(BF16)
    ones_rows = (lax.broadcasted_iota(jnp.int32, (DIFF_VROWS - HEAD_DIM, DIFF_TK), 0) == 0).astype(BF16)
    for j in range(tm // DIFF_TK):
        for hd in range(N_HEADS):
            dvt_ref[0, j, hd, 0:HEAD_DIM, :] = dvt[hd * HEAD_DIM:(hd + 1) * HEAD_DIM, j * DIFF_TK:(j + 1) * DIFF_TK]
            dvt_ref[0, j, hd, HEAD_DIM:DIFF_VROWS, :] = ones_rows
    lx_ref[0] = proj(OFF_LX, 256)
    ly_ref[0] = proj(OFF_LY, 256)


def _in_projection(x, g, sh, sc, w_in, tables):
    b, l, d = x.shape
    rope = tables is not None
    tm = _row_tile(l, ROWS_PROJ)
    nt = l // tm
    row = lambda shape: pl.BlockSpec(shape, lambda i, j: (i, j, 0))
    vec = pl.BlockSpec((1, 1, d), lambda i, j: (i, 0, 0))
    in_specs = [row((1, tm, d)), pl.BlockSpec((1, d), lambda i, j: (0, 0)), vec, vec,
                pl.BlockSpec((d, D_IN), lambda i, j: (0, 0))]
    args = [x, g.reshape(1, d), sh, sc, w_in]
    if rope:
        in_specs += [pl.BlockSpec((tm, LANE), lambda i, j: (j, 0))] * 6
        args += list(tables)
    sds = jax.ShapeDtypeStruct
    out_shape = [sds((b, l, 256), BF16), sds((b, l, 128), BF16), sds((b, l, 128), BF16),
                 sds((b, l, 256), BF16), sds((b, l, 256), BF16), sds((b, l, 256), BF16),
                 sds((b, l, 256), F32),
                 sds((b, 256, l), BF16), sds((b, l, 256), BF16),
                 sds((b, l // DIFF_TK, N_HEADS, DIFF_VROWS, DIFF_TK), BF16),
                 sds((b, l, 256), F32), sds((b, l, 256), F32)]
    out_specs = [row((1, tm, 256)), row((1, tm, 128)), row((1, tm, 128)),
                 row((1, tm, 256)), row((1, tm, 256)), row((1, tm, 256)), row((1, tm, 256)),
                 pl.BlockSpec((1, 256, tm), lambda i, j: (i, 0, j)), row((1, tm, 256)),
                 pl.BlockSpec((1, tm // DIFF_TK, N_HEADS, DIFF_VROWS, DIFF_TK), lambda i, j: (i, j, 0, 0, 0)),
                 row((1, tm, 256)), row((1, tm, 256))]
    return pl.pallas_call(
        functools.partial(_inproj_kernel, rope=rope, tm=tm),
        out_shape=out_shape, grid=(b, nt), in_specs=in_specs, out_specs=out_specs,
        compiler_params=_params("arbitrary", "arbitrary"),
        name="in_projection_rope" if rope else "in_projection_ctx",
    )(*args)


def _swa_kernel(*refs, window, tq, n):
    if window:
        sink_ref, q_ref, k_ref, v_ref, kc_ref, vc_ref, o_ref = refs
    else:
        sink_ref, q_ref, kc_ref, vc_ref, o_ref = refs
    kc = kc_ref[0]
    vc = vc_ref[0]
    m = kc.shape[0]
    lane = lax.broadcasted_iota(jnp.int32, (1, LANE), 1)

    def attend(q, kcat, vcat, valid):
        out_tiles = []
        for t in range(2):
            qt = q[:, t * LANE:(t + 1) * LANE]
            out_t = None
            for half in range(2):
                in_half = (lane >= half * HEAD_DIM) & (lane < (half + 1) * HEAD_DIM)
                sink = sink_ref[2 * t + half]
                qm = jnp.where(in_half, qt, jnp.zeros_like(qt))
                s = _dot_nt(qm, kcat)
                if valid is not None:
                    s = jnp.where(valid, s, -jnp.inf)
                mx = jnp.maximum(jnp.max(s, axis=1, keepdims=True), sink)
                e = jnp.exp(s - mx)
                den = jnp.sum(e, axis=1, keepdims=True) + jnp.exp(sink - mx)
                o = _dot(e.astype(BF16), vcat) / den
                out_t = o if out_t is None else jnp.where(in_half, o, out_t)
            out_tiles.append(out_t)
        return jnp.concatenate(out_tiles, axis=1)

    if not window:
        o_ref[0] = attend(q_ref[0], kc, vc, None)
        return
    w = tq + 2 * WINDOW
    t0 = pl.program_id(1) * tq
    start = pl.multiple_of(jnp.clip(t0 - WINDOW, 0, n - w), WINDOW)
    kcat = jnp.concatenate([k_ref[0, pl.ds(start, w), :], kc], axis=0)
    vcat = jnp.concatenate([v_ref[0, pl.ds(start, w), :], vc], axis=0)
    qpos = t0 + lax.broadcasted_iota(jnp.int32, (tq, w + m), 0)
    col = lax.broadcasted_iota(jnp.int32, (tq, w + m), 1)
    valid = (jnp.abs(qpos - (start + col)) <= WINDOW) | (col >= w)
    o_ref[0] = attend(q_ref[0], kcat, vcat, valid)


def _swa_attention(sink_perm, q, k, v, kc, vc):
    b, n, _ = q.shape
    m = kc.shape[1]
    tq = 256
    smem = pl.BlockSpec(memory_space=pltpu.SMEM)
    ctx = pl.BlockSpec((1, m, 128), lambda i, j: (i, 0, 0))
    in_specs = [smem, pl.BlockSpec((1, tq, 256), lambda i, j: (i, j, 0)),
                pl.BlockSpec((1, n, 128), lambda i, j: (i, 0, 0)),
                pl.BlockSpec((1, n, 128), lambda i, j: (i, 0, 0)), ctx, ctx]
    return pl.pallas_call(
        functools.partial(_swa_kernel, window=True, tq=tq, n=n),
        out_shape=jax.ShapeDtypeStruct((b, n, 256), F32),
        grid=(b, n // tq), in_specs=in_specs,
        out_specs=pl.BlockSpec((1, tq, 256), lambda i, j: (i, j, 0)),
        compiler_params=_params("arbitrary", "arbitrary"),
        name="swa_attention",
    )(sink_perm, q, k, v, kc, vc)


def _ctx_attention(sink_perm, qc, kc, vc):
    b, m, _ = qc.shape
    smem = pl.BlockSpec(memory_space=pltpu.SMEM)
    ctx = pl.BlockSpec((1, m, 128), lambda i: (i, 0, 0))
    return pl.pallas_call(
        functools.partial(_swa_kernel, window=False, tq=m, n=m),
        out_shape=jax.ShapeDtypeStruct((b, m, 256), F32),
        grid=(b,), in_specs=[smem, pl.BlockSpec((1, m, 256), lambda i: (i, 0, 0)), ctx, ctx],
        out_specs=pl.BlockSpec((1, m, 256), lambda i: (i, 0, 0)),
        compiler_params=_params("arbitrary"),
        name="ctx_attention",
    )(sink_perm, qc, kc, vc)


RET_CHUNK = 128


def _ret_kernel(dl_lane_ref, dl_head_ref, q_ref, k_ref, v_ref, s0_ref, y_ref, sfin_ref,
                s_scr, dm_scr, xi_scr, zeta_scr, *, ns, group):
    c = RET_CHUNK
    nb = q_ref.shape[0]
    d = pl.program_id(0)
    si = pl.program_id(1)
    fwd = d == 0
    lane_head = lax.broadcasted_iota(jnp.int32, (1, GROUP), 1) // HEAD_DIM

    @pl.when(si == 0)
    def _():
        s_scr[...] = s0_ref[:, 0]
        lgl = -_softplus(-dl_lane_ref[0])
        row = lax.broadcasted_iota(jnp.int32, (c, 1), 0)
        xi_scr[...] = jnp.exp(lgl * jnp.where(fwd, row + 1, c - row).astype(F32))
        zeta_scr[...] = jnp.exp(lgl * jnp.where(fwd, c - 1 - row, row).astype(F32))
        ni = lax.broadcasted_iota(jnp.int32, (c, c), 0)
        mi = lax.broadcasted_iota(jnp.int32, (c, c), 1)
        dd = jnp.where(fwd, ni - mi, mi - ni)
        ddf = jnp.maximum(dd, 0).astype(F32)
        for h in range(N_HEADS):
            lgh = -_softplus(-dl_head_ref[0, h][0:1, :])
            dm_scr[h] = jnp.where(dd >= 0, jnp.exp(lgh * ddf), 0.0)

    dec = jnp.exp(-_softplus(-dl_lane_ref[0]) * float(c))
    rh = lax.broadcasted_iota(jnp.int32, (GROUP, GROUP), 0) // HEAD_DIM
    ch = lax.broadcasted_iota(jnp.int32, (GROUP, GROUP), 1) // HEAD_DIM
    same_head = rh == ch

    def chunk(g, carry):
        r0 = pl.multiple_of(jnp.where(fwd, g, group - 1 - g) * c, c)
        for bi in range(nb):
            q = q_ref[bi, pl.ds(r0, c), :]
            k = k_ref[bi, pl.ds(r0, c), :]
            v = v_ref[bi, pl.ds(r0, c), :]
            cross = _dot(q, s_scr[bi].astype(BF16)) * xi_scr[...]
            inner = jnp.zeros((c, GROUP), F32)
            for h in range(N_HEADS):
                qh = jnp.where(lane_head == h, q, jnp.zeros_like(q))
                s = _dot_nt(qh, k) * dm_scr[h]
                ih = _dot(s.astype(BF16), v)
                inner = jnp.where(lane_head == h, ih, inner)
            y_ref[bi, 0, pl.ds(r0, c), :] = inner + cross
            kz = (k.astype(F32) * zeta_scr[...]).T.astype(BF16)
            u = _dot(kz, v)
            s_scr[bi] = jnp.where(same_head, dec * s_scr[bi] + u, 0.0)
        return carry

    lax.fori_loop(0, group, chunk, 0, unroll=2)

    @pl.when(si == ns - 1)
    def _():
        sfin_ref[:, 0] = s_scr[...]


def _retention(dl_lane, dl_head, q, k, v, s0):
    b, l, _ = q.shape
    c = RET_CHUNK
    group = 8 if l % (8 * c) == 0 else 2
    ns = l // (group * c)
    sidx = lambda d, j: (0, jnp.where(d == 0, j, ns - 1 - j), 0)
    tok = pl.BlockSpec((b, group * c, 256), sidx)
    st = pl.BlockSpec((b, 1, 256, 256), lambda d, j: (0, d, 0, 0))
    return pl.pallas_call(
        functools.partial(_ret_kernel, ns=ns, group=group),
        out_shape=[jax.ShapeDtypeStruct((b, 2, l, 256), F32),
                   jax.ShapeDtypeStruct((b, 2, 256, 256), F32)],
        grid=(2, ns),
        in_specs=[pl.BlockSpec((1, 1, 256), lambda d, j: (d, 0, 0)),
                  pl.BlockSpec((1, N_HEADS, 8, c), lambda d, j: (d, 0, 0, 0)),
                  tok, tok, tok, st],
        out_specs=[pl.BlockSpec((b, 1, group * c, 256),
                                lambda d, j: (0, d, jnp.where(d == 0, j, ns - 1 - j), 0)),
                   st],
        scratch_shapes=[pltpu.VMEM((b, GROUP, GROUP), F32),
                        pltpu.VMEM((N_HEADS, c, c), F32),
                        pltpu.VMEM((c, GROUP), F32),
                        pltpu.VMEM((c, GROUP), F32)],
        compiler_params=_params("arbitrary", "arbitrary"),
        name="retention",
    )(dl_lane, dl_head, q, k, v, s0)


def _inner_tiles(tiles):
    for g in (8, 7, 6, 5, 4, 3, 2):
        if tiles % g == 0:
            return g
    return 1


def _diff_kernel(qt_ref, k_ref, vt_ref, o_ref, s_scr, *, tiles, nq):
    h = pl.program_id(1)
    inner = _inner_tiles(tiles)
    outer = tiles // inner
    total = nq * outer
    tq = DIFF_TQ
    rowi = lax.broadcasted_iota(jnp.int32, (LANE, 1), 0)
    hoff = (h % 2) * HEAD_DIM
    rowmask = [(rowi >= hoff + mp * DIFF_DIM) & (rowi < hoff + (mp + 1) * DIFF_DIM) for mp in range(2)]

    def step(mp, g_q, g_e, slot, m_e):
        mx = pv = None
        if g_q is not None:
            q0 = pl.multiple_of((g_q // outer) * tq, tq)
            qt = qt_ref[0, :, pl.ds(q0, tq)]
            qpad = jnp.where(rowmask[mp], qt, jnp.zeros_like(qt))
            kc = (g_q % outer) * inner
        if g_e is not None:
            vc = (g_e % outer) * inner
        for t in range(inner):
            sl = slice(t * DIFF_TK, (t + 1) * DIFF_TK)
            if g_q is not None:
                base = pl.multiple_of((kc + t) * DIFF_TK, DIFF_TK)
                s = _dot(k_ref[0, pl.ds(base, DIFF_TK), :], qpad)
                s_scr[mp, 1 - slot, sl, :] = s
                cm = jnp.max(s, axis=0, keepdims=True)
                mx = cm if mx is None else jnp.maximum(mx, cm)
            if g_e is not None:
                p = jnp.exp2(s_scr[mp, slot, sl, :] - m_e).astype(BF16)
                d = _dot(vt_ref[0, vc + t, 0], p)
                pv = d if pv is None else pv + d
        return mx, pv

    def advance(mp, g_q, g_e, slot, state):
        m_e, alpha, acc = state
        mx, pv = step(mp, g_q, g_e, slot, m_e)
        acc = alpha * acc + pv
        o0 = pl.multiple_of((g_e // outer) * tq, tq)
        o_ref[0, 0, mp, :, pl.ds(o0, tq)] = acc
        if g_q is None:
            return m_e, alpha, acc
        m_base = jnp.where(g_q % outer == 0, -jnp.inf, m_e)
        m_next = jnp.maximum(m_base, mx)
        return m_next, jnp.exp2(m_base - m_next), acc

    zeros = jnp.zeros((DIFF_VROWS, tq), F32)
    states = []
    for mp in range(2):
        m_first, _ = step(mp, 0, None, 1, None)
        states.append((m_first, jnp.zeros((1, tq), F32), zeros))
    states = tuple(states)

    def body(i, carry):
        carry = tuple(advance(mp, 2 * i + 1, 2 * i, 0, carry[mp]) for mp in range(2))
        return tuple(advance(mp, 2 * i + 2, 2 * i + 1, 1, carry[mp]) for mp in range(2))

    last = total - 1
    states = lax.fori_loop(0, last // 2, body, states, unroll=min(4, max(last // 2, 1)))
    if last % 2 == 1:
        states = tuple(advance(mp, last, last - 1, 0, states[mp]) for mp in range(2))
    for mp in range(2):
        advance(mp, None, last, last % 2, states[mp])


def _diff_attention(qt, k_all, vt_all):
    b, _, lq = qt.shape
    lk = k_all.shape[1]
    tiles = lk // DIFF_TK
    nq = lq // DIFF_TQ
    return pl.pallas_call(
        functools.partial(_diff_kernel, tiles=tiles, nq=nq),
        out_shape=jax.ShapeDtypeStruct((b, N_HEADS, 2, DIFF_VROWS, lq), F32),
        grid=(b, N_HEADS),
        in_specs=[pl.BlockSpec((1, LANE, lq), lambda i, h: (i, h // 2, 0)),
                  pl.BlockSpec((1, lk, LANE), lambda i, h: (i, 0, h // 2)),
                  pl.BlockSpec((1, tiles, 1, DIFF_VROWS, DIFF_TK), lambda i, h: (i, 0, h, 0, 0))],
        out_specs=pl.BlockSpec((1, 1, 2, DIFF_VROWS, lq), lambda i, h: (i, h, 0, 0, 0)),
        scratch_shapes=[pltpu.VMEM((2, 2, _inner_tiles(tiles) * DIFF_TK, DIFF_TQ), F32)],
        compiler_params=_params("arbitrary", "arbitrary"),
        name="diff_attention",
    )(qt, k_all, vt_all)


def _lru_kernel(x_ref, xp_ref, xn_ref, cw_ref, cb_ref, wa_ref, ba_ref, wx_ref, bx_ref, lam_ref,
                h0_ref, h_ref, carry, *, reverse, tm, nt):
    j = pl.program_id(1)
    tix = (nt - 1 - j) if reverse else j

    @pl.when(j == 0)
    def _():
        carry[...] = jnp.broadcast_to(h0_ref[0], carry.shape)

    x = x_ref[0]
    row = lax.broadcasted_iota(jnp.int32, (tm, 1), 0)
    prev = jnp.where(tix > 0, xp_ref[0][7:8, :], 0.0)
    nxt0 = jnp.where(tix < nt - 1, xn_ref[0][0:1, :], 0.0)
    nxt1 = jnp.where(tix < nt - 1, xn_ref[0][1:2, :], 0.0)
    xm1 = jnp.where(row == 0, prev, pltpu.roll(x, 1, axis=0))
    xp1 = jnp.where(row == tm - 1, nxt0, pltpu.roll(x, tm - 1, axis=0))
    xp2 = jnp.where(row == tm - 1, nxt1, jnp.where(row == tm - 2, nxt0, pltpu.roll(x, tm - 2, axis=0)))
    cw = cw_ref[...]
    u = cw[0:1] * xm1 + cw[1:2] * x + cw[2:3] * xp1 + cw[3:4] * xp2 + cb_ref[...]
    ub = u.astype(BF16)
    r = jax.nn.sigmoid(_dot(ub, wa_ref[0]) + ba_ref[0])
    gi = jax.nn.sigmoid(_dot(ub, wx_ref[0]) + bx_ref[0])
    log_a = -LRU_C * r * _softplus(-lam_ref[0])
    a = jnp.exp(log_a)
    th = jnp.tanh(log_a)
    bv = jnp.sqrt(-2.0 * th / (1.0 - th)) * (gi * u)
    k = 1
    while k < tm:
        if k < 8:
            keep = (row < tm - k) if reverse else (row >= k)
            shift = (tm - k) if reverse else k
            a_sh = jnp.where(keep, pltpu.roll(a, shift, axis=0), 1.0)
            b_sh = jnp.where(keep, pltpu.roll(bv, shift, axis=0), 0.0)
            bv = a * b_sh + bv
            a = a * a_sh
        elif reverse:
            head_b = a[:tm - k] * bv[k:] + bv[:tm - k]
            head_a = a[:tm - k] * a[k:]
            bv = jnp.concatenate([head_b, bv[tm - k:]], axis=0)
            a = jnp.concatenate([head_a, a[tm - k:]], axis=0)
        else:
            tail_b = a[k:] * bv[:tm - k] + bv[k:]
            tail_a = a[k:] * a[:tm - k]
            bv = jnp.concatenate([bv[:k], tail_b], axis=0)
            a = jnp.concatenate([a[:k], tail_a], axis=0)
        k *= 2
    hcur = a * carry[0:1, :] + bv
    h_ref[0] = hcur
    last = hcur[0:1, :] if reverse else hcur[tm - 1:tm, :]
    carry[...] = jnp.broadcast_to(last, carry.shape)


def _lru_scan(lx, conv_w, conv_b, wa_bd, ba, wx_bd, bx, lam, h0, reverse):
    b, l, w = lx.shape
    tm = _row_tile(l, ROWS_PROJ)
    nt = l // tm
    tb = tm // 8
    tidx = (lambda j: nt - 1 - j) if reverse else (lambda j: j)
    vec = pl.BlockSpec((1, w), lambda i, j: (0, 0))
    mat = pl.BlockSpec((1, w, w), lambda i, j: (0, 0, 0))
    return pl.pallas_call(
        functools.partial(_lru_kernel, reverse=reverse, tm=tm, nt=nt),
        out_shape=jax.ShapeDtypeStruct((b, l, w), F32),
        grid=(b, nt),
        in_specs=[pl.BlockSpec((1, tm, w), lambda i, j: (i, tidx(j), 0)),
                  pl.BlockSpec((1, 8, w), lambda i, j: (i, jnp.maximum(tidx(j) * tb - 1, 0), 0)),
                  pl.BlockSpec((1, 8, w), lambda i, j: (i, jnp.minimum((tidx(j) + 1) * tb, l // 8 - 1), 0)),
                  pl.BlockSpec((4, w), lambda i, j: (0, 0)), vec,
                  mat, vec, mat, vec, vec,
                  pl.BlockSpec((1, 1, w), lambda i, j: (i, 0, 0))],
        out_specs=pl.BlockSpec((1, tm, w), lambda i, j: (i, tidx(j), 0)),
        scratch_shapes=[pltpu.VMEM((8, w), F32)],
        compiler_params=_params("arbitrary", "arbitrary"),
        name="lru_scan_rev" if reverse else "lru_scan_fwd",
    )(lx, lx, lx, conv_w, conv_b.reshape(1, w), wa_bd.reshape(1, w, w), ba.reshape(1, w),
      wx_bd.reshape(1, w, w), bx.reshape(1, w), lam.reshape(1, w), h0)


def _outproj_kernel(x_ref, gt_ref, oa_ref, ry_ref, rg_ref, rgn_ref, lam_ref, dgn_ref, draw_ref, hf_ref, hb_ref,
                    ly_ref, bd_ref, w_ref, o_ref, *, lam_init):
    oa = oa_ref[0]
    y = ry_ref[0, 0] + ry_ref[0, 1]
    y2 = y * y
    hi = y2.astype(BF16)
    lo = (y2 - hi.astype(F32)).astype(BF16)
    ms = _dot(hi, bd_ref[...]) + _dot(lo, bd_ref[...])
    ob = y * lax.rsqrt(ms + NORM_EPS) * rgn_ref[...] * _silu(rg_ref[0])
    lv = lam_ref[...]
    lam = (jnp.exp(jnp.sum(lv[0:1] * lv[1:2], axis=1, keepdims=True))
           - jnp.exp(jnp.sum(lv[2:3] * lv[3:4], axis=1, keepdims=True)) + lam_init)
    heads = []
    for hd in range(N_HEADS):
        a0 = draw_ref[0, hd, 0]
        a1 = draw_ref[0, hd, 1]
        oh = (a0[0:HEAD_DIM] / a0[HEAD_DIM:HEAD_DIM + 1]
              - lam * (a1[0:HEAD_DIM] / a1[HEAD_DIM:HEAD_DIM + 1]))
        msd = jnp.mean(oh * oh, axis=0, keepdims=True)
        heads.append(oh * lax.rsqrt(msd + NORM_EPS) * (dgn_ref[hd] * (1.0 - lam_init)))
    od = jnp.concatenate(heads, axis=0).T
    ol = (hf_ref[0] + hb_ref[0]) * jax.nn.gelu(ly_ref[0])
    acc = _dot(oa.astype(BF16), w_ref[0:256, :])
    acc = acc + _dot(ob.astype(BF16), w_ref[256:512, :])
    acc = acc + _dot(od.astype(BF16), w_ref[512:768, :])
    acc = acc + _dot(ol.astype(BF16), w_ref[768:1024, :])
    o_ref[0] = x_ref[0] + gt_ref[0] * acc


def _out_projection(x, gt, oa, ry, rg, ret_gn, lam_vecs, diff_gn, lam_init, draw, hf, hb, ly, w_out):
    b, l, d = x.shape
    tm = _row_tile(l, ROWS_PROJ)
    row = lambda wdt: pl.BlockSpec((1, tm, wdt), lambda i, j: (i, j, 0))
    hd = jnp.arange(GROUP) // HEAD_DIM
    bd = jnp.where(hd[:, None] == hd[None, :], 1.0 / HEAD_DIM, 0.0).astype(BF16)
    return pl.pallas_call(
        functools.partial(_outproj_kernel, lam_init=lam_init),
        out_shape=jax.ShapeDtypeStruct((b, l, d), F32),
        grid=(b, l // tm),
        in_specs=[row(d), pl.BlockSpec((1, 1, d), lambda i, j: (i, 0, 0)), row(256),
                  pl.BlockSpec((1, 2, tm, 256), lambda i, j: (i, 0, j, 0)), row(256),
                  pl.BlockSpec((1, 256), lambda i, j: (0, 0)),
                  pl.BlockSpec((4, DIFF_DIM), lambda i, j: (0, 0)),
                  pl.BlockSpec((N_HEADS, HEAD_DIM, 1), lambda i, j: (0, 0, 0)),
                  pl.BlockSpec((1, N_HEADS, 2, DIFF_VROWS, tm), lambda i, j: (i, 0, 0, 0, j)),
                  row(256), row(256), row(256),
                  pl.BlockSpec((256, 256), lambda i, j: (0, 0)),
                  pl.BlockSpec((d, d), lambda i, j: (0, 0))],
        out_specs=row(d),
        compiler_params=_params("arbitrary", "arbitrary"),
        name="out_projection",
    )(x, gt, oa, ry, rg, ret_gn.reshape(1, 256), lam_vecs, diff_gn.reshape(N_HEADS, HEAD_DIM, 1), draw,
      hf, hb, ly, bd, w_out)


def _norm_modulate(x_ref, g_ref, sh_ref, sc_ref):
    x = x_ref[0]
    ms = jnp.mean(x * x, axis=-1, keepdims=True)
    y = x * lax.rsqrt(ms + NORM_EPS) * g_ref[...]
    return y * (1.0 + sc_ref[0]) + sh_ref[0]


def _ffn_kernel(x_ref, g_ref, sh_ref, sc_ref, gt_ref, w1_ref, w3_ref, w2_ref, o_ref, hb_scr, acc_scr,
                *, n_f):
    f = pl.program_id(2)

    @pl.when(f == 0)
    def _():
        hb_scr[...] = _norm_modulate(x_ref, g_ref, sh_ref, sc_ref).astype(BF16)
        acc_scr[...] = jnp.zeros_like(acc_scr)

    hb = hb_scr[...]
    a = _silu(_dot(hb, w1_ref[...])) * _dot(hb, w3_ref[...])
    acc_scr[...] += _dot(a.astype(BF16), w2_ref[...])

    @pl.when(f == n_f - 1)
    def _():
        o_ref[0] = x_ref[0] + gt_ref[0] * acc_scr[...]


def _ffn(x, g, sh, sc, gt, w1, w3, w2):
    b, l, d = x.shape
    ff = w1.shape[1]
    tm = _row_tile(l, ROWS_FFN)
    n_f = 2
    fc = ff // n_f
    vec = pl.BlockSpec((1, 1, d), lambda i, j, f: (i, 0, 0))
    return pl.pallas_call(
        functools.partial(_ffn_kernel, n_f=n_f),
        out_shape=jax.ShapeDtypeStruct((b, l, d), F32),
        grid=(b, l // tm, n_f),
        in_specs=[pl.BlockSpec((1, tm, d), lambda i, j, f: (i, j, 0)),
                  pl.BlockSpec((1, d), lambda i, j, f: (0, 0)), vec, vec, vec,
                  pl.BlockSpec((d, fc), lambda i, j, f: (0, f)),
                  pl.BlockSpec((d, fc), lambda i, j, f: (0, f)),
                  pl.BlockSpec((fc, d), lambda i, j, f: (f, 0))],
        out_specs=pl.BlockSpec((1, tm, d), lambda i, j, f: (i, j, 0)),
        scratch_shapes=[pltpu.VMEM((tm, d), BF16), pltpu.VMEM((tm, d), F32)],
        compiler_params=_params("arbitrary", "arbitrary", "arbitrary"),
        name="dense_ffn",
    )(x, g.reshape(1, d), sh, sc, gt, w1, w3, w2)


MOE_RT = 128


def _moe_kernel(x_ref, g_ref, sh_ref, sc_ref, gt_ref, fg_ref, rt_ref, w1_ref, w3_ref, w2_ref, o_ref,
                hb_scr, gate_scr, pos_scr, gatet_scr, post_scr, x_scr, y_scr, gs_scr, cnt_smem,
                *, n_e, n_f, tm, final_norm):
    e = pl.program_id(2)
    f = pl.program_id(3)
    rt2 = 2 * MOE_RT

    @pl.when((e == 0) & (f == 0))
    def _():
        h = _norm_modulate(x_ref, g_ref, sh_ref, sc_ref)
        hb = h.astype(BF16)
        hb_scr[...] = hb
        o_ref[0] = jnp.zeros(o_ref.shape[1:], F32)
        rt = rt_ref[...]
        rhi = rt.astype(BF16)
        rlo = (rt - rhi.astype(F32)).astype(BF16)
        hlo = (h - hb.astype(F32)).astype(BF16)
        logits = _dot(hb, rhi) + _dot(hb, rlo) + _dot(hlo, rhi)
        lane = lax.broadcasted_iota(jnp.int32, logits.shape, 1)
        logits = jnp.where(lane < N_EXPERTS, logits, -jnp.inf)
        m1 = jnp.max(logits, axis=1, keepdims=True)
        i1 = jnp.min(jnp.where(logits == m1, lane, LANE), axis=1, keepdims=True)
        rest = jnp.where(lane == i1, -jnp.inf, logits)
        m2 = jnp.max(rest, axis=1, keepdims=True)
        i2 = jnp.min(jnp.where(rest == m2, lane, LANE), axis=1, keepdims=True)
        e2 = jnp.exp(m2 - m1)
        den = 1.0 + e2
        gate = jnp.where(lane == i1, 1.0 / den, 0.0) + jnp.where(lane == i2, e2 / den, 0.0)
        gate_scr[...] = gate
        sel = jnp.where(gate > 0.0, 1.0, 0.0)
        selb = sel.astype(BF16)
        strip = 256
        for r0 in range(0, tm, strip):
            tok_r = r0 + lax.broadcasted_iota(jnp.int32, (strip, tm), 0)
            tok_c = lax.broadcasted_iota(jnp.int32, (strip, tm), 1)
            earlier = jnp.where(tok_c < tok_r, 1.0, 0.0).astype(BF16)
            pos_scr[r0:r0 + strip, :] = _dot(earlier, selb)
        post_scr[...] = pos_scr[...].T
        gatet_scr[...] = gate.T
        counts = jnp.sum(sel, axis=0, keepdims=True)
        for ee in range(n_e):
            cnt_smem[ee] = jnp.sum(jnp.where(lane[0:1] == ee, counts, 0.0)).astype(jnp.int32)

    n_tiles = (cnt_smem[e] + (MOE_RT - 1)) // MOE_RT
    n_pairs = (n_tiles + 1) // 2

    @pl.when(f == 0)
    def _():
        sub = lax.broadcasted_iota(jnp.int32, (8, 1), 0)
        pos_e = jnp.sum(jnp.where(sub == e, post_scr[0:8, :], 0.0), axis=0, keepdims=True)
        gate_e = jnp.sum(jnp.where(sub == e, gatet_scr[0:8, :], 0.0), axis=0, keepdims=True)

        def gather(jj, carry):
            r0 = pl.multiple_of(jj * rt2, rt2)
            slot = (r0 + lax.broadcasted_iota(jnp.int32, (rt2, 1), 0)).astype(F32)
            hit = (pos_e == slot) & (gate_e > 0.0)
            onehot = jnp.where(hit, 1.0, 0.0).astype(BF16)
            x_scr[pl.ds(r0, rt2), :] = _dot(onehot, hb_scr[...]).astype(BF16)
            gs = jnp.sum(jnp.where(hit, gate_e, 0.0), axis=1, keepdims=True)
            gs_scr[pl.ds(r0, rt2), :] = jnp.broadcast_to(gs, (rt2, LANE))
            y_scr[pl.ds(r0, rt2), :] = jnp.zeros((rt2, y_scr.shape[1]), F32)
            return carry

        lax.fori_loop(0, n_pairs, gather, 0)

    def expert(r0, rows):
        xj = x_scr[pl.ds(r0, rows), :]
        a = _silu(_dot(xj, w1_ref[0])) * _dot(xj, w3_ref[0])
        y_scr[pl.ds(r0, rows), :] += _dot(a.astype(BF16), w2_ref[0])

    def expert_pair(jj, carry):
        expert(pl.multiple_of(jj * rt2, rt2), rt2)
        return carry

    lax.fori_loop(0, n_tiles // 2, expert_pair, 0)

    @pl.when(n_tiles % 2 == 1)
    def _():
        expert(pl.multiple_of((n_tiles - 1) * MOE_RT, MOE_RT), MOE_RT)

    @pl.when(f == n_f - 1)
    def _():
        lane = lax.broadcasted_iota(jnp.int32, (1, LANE), 1)
        pos_col = jnp.sum(jnp.where(lane == e, pos_scr[...], 0.0), axis=1, keepdims=True)
        gate_col = jnp.sum(jnp.where(lane == e, gate_scr[...], 0.0), axis=1, keepdims=True)

        def scatter(jj, carry):
            r0 = pl.multiple_of(jj * rt2, rt2)
            slot = (r0 + lax.broadcasted_iota(jnp.int32, (1, rt2), 1)).astype(F32)
            onehot = jnp.where((pos_col == slot) & (gate_col > 0.0), 1.0, 0.0).astype(BF16)
            ys = (y_scr[pl.ds(r0, rt2), :] * gs_scr[pl.ds(r0, rt2), 0:1]).astype(BF16)
            o_ref[0] += _dot(onehot, ys)
            return carry

        lax.fori_loop(0, n_pairs, scatter, 0)

    @pl.when((e == n_e - 1) & (f == n_f - 1))
    def _():
        out = x_ref[0] + gt_ref[0] * o_ref[0]
        if final_norm:
            ms = jnp.mean(out * out, axis=-1, keepdims=True)
            out = out * lax.rsqrt(ms + NORM_EPS) * fg_ref[...]
        o_ref[0] = out


def _moe_ffn(x, g, sh, sc, gt, final_g, w1, w3, w2, router, final_norm):
    b, l, d = x.shape
    n_e, _, ff = w1.shape
    tm = _row_tile(l, ROWS_FFN)
    n_f = 2
    fc = ff // n_f
    vec = pl.BlockSpec((1, 1, d), lambda i, j, e, f: (i, 0, 0))
    return pl.pallas_call(
        functools.partial(_moe_kernel, n_e=n_e, n_f=n_f, tm=tm, final_norm=final_norm),
        out_shape=jax.ShapeDtypeStruct((b, l, d), F32),
        grid=(b, l // tm, n_e, n_f),
        in_specs=[pl.BlockSpec((1, tm, d), lambda i, j, e, f: (i, j, 0)),
                  pl.BlockSpec((1, d), lambda i, j, e, f: (0, 0)), vec, vec, vec,
                  pl.BlockSpec((1, d), lambda i, j, e, f: (0, 0)),
                  pl.BlockSpec((d, LANE), lambda i, j, e, f: (0, 0)),
                  pl.BlockSpec((1, d, fc), lambda i, j, e, f: (e, 0, f)),
                  pl.BlockSpec((1, d, fc), lambda i, j, e, f: (e, 0, f)),
                  pl.BlockSpec((1, fc, d), lambda i, j, e, f: (e, f, 0))],
        out_specs=pl.BlockSpec((1, tm, d), lambda i, j, e, f: (i, j, 0)),
        scratch_shapes=[pltpu.VMEM((tm, d), BF16),
                        pltpu.VMEM((tm, LANE), F32),
                        pltpu.VMEM((tm, LANE), F32),
                        pltpu.VMEM((LANE, tm), F32),
                        pltpu.VMEM((LANE, tm), F32),
                        pltpu.VMEM((tm, d), BF16),
                        pltpu.VMEM((tm, d), F32),
                        pltpu.VMEM((tm, LANE), F32),
                        pltpu.SMEM((n_e,), jnp.int32)],
        compiler_params=_params("arbitrary", "arbitrary", "arbitrary", "arbitrary"),
        name="moe_ffn",
    )(x, g.reshape(1, d), sh, sc, gt, final_g.reshape(1, d), router, w1, w3, w2)


def _rope_tables(n):
    rows = n // GRID_W
    row = jnp.repeat(jnp.arange(rows, dtype=F32), GRID_W)
    col = jnp.tile(jnp.arange(GRID_W, dtype=F32), rows)

    def axial(dim):
        nf = dim // 4
        inv = ROPE_BASE ** (-jnp.arange(nf, dtype=F32) / nf)
        return jnp.concatenate([row[:, None] * inv, col[:, None] * inv], axis=-1)

    def line(dim):
        nf = dim // 2
        inv = ROPE_BASE ** (-jnp.arange(nf, dtype=F32) / nf)
        return jnp.arange(n, dtype=F32)[:, None] * inv

    def expand(ang, dim):
        c = jnp.cos(ang)
        s = jnp.sin(ang)
        c = jnp.tile(jnp.concatenate([c, c], axis=-1), (1, LANE // dim))
        s = jnp.tile(jnp.concatenate([-s, s], axis=-1), (1, LANE // dim))
        return c, s

    ca, sa = expand(axial(HEAD_DIM), HEAD_DIM)
    cr, sr = expand(line(HEAD_DIM), HEAD_DIM)
    cd, sd = expand(axial(DIFF_DIM), DIFF_DIM)
    return ca, sa, cr, sr, cd, sd


_ATTN_HEAD_ORDER = (0, 2, 1, 3)


def _attn_perm():
    return jnp.concatenate([jnp.arange(HEAD_DIM) + HEAD_DIM * h for h in _ATTN_HEAD_ORDER])


def _block_diag(w):
    nb, blk, _ = w.shape
    out = jnp.zeros((nb * blk, nb * blk), w.dtype)
    for i in range(nb):
        out = out.at[i * blk:(i + 1) * blk, i * blk:(i + 1) * blk].set(w[i])
    return out


def kernel(x, c, ctx, c_ctx, w_ada, b_ada, g_norm1, g_norm2, w_in, w_out, attn_sink, ret_decay_logit,
           ret_gn, diff_lambda, diff_gn, conv_w, conv_b, lru_wa, lru_ba, lru_wx, lru_bx, lru_lambda,
           ffn_w1, ffn_w3, ffn_w2, moe_router, moe_w1, moe_w3, moe_w2, final_norm):
    b, n, d = x.shape
    m = ctx.shape[1]
    tables = _rope_tables(n)
    perm = _attn_perm()

    cc = jnp.concatenate([c, c_ctx[None, :], jnp.zeros((8 - b - 1, d), F32)], axis=0)
    mods = _mod_vectors(cc, w_ada, b_ada)

    xc = ctx
    for layer in range(DEPTH):
        need_ctx = layer < DEPTH - 1
        lam_init = 0.8 - 0.6 * math.exp(-0.3 * layer)
        mod = mods[layer]
        sh1, sc1, gt1, sh2, sc2, gt2 = [mod[:b, None, i * d:(i + 1) * d] for i in range(6)]
        sh1c, sc1c, gt1c, sh2c, sc2c, gt2c = [
            jnp.broadcast_to(mod[b:b + 1, None, i * d:(i + 1) * d], (b, 1, d)) for i in range(6)]

        wi = w_in[layer]
        wi = jnp.concatenate([wi[:, :256][:, perm], wi[:, 256:]], axis=1).astype(BF16)
        wo = w_out[layer]
        wo = jnp.concatenate([wo[:256][perm], wo[256:]], axis=0).astype(BF16)
        sink_perm = attn_sink[layer][jnp.array(_ATTN_HEAD_ORDER)]

        (qa, ka, va, rq, rk, rv, rg, dqt, dk, dvt, lx, ly) = _in_projection(
            x, g_norm1[layer], sh1, sc1, wi, tables)
        (qac, kac, vac, rqc, rkc, rvc, rgc, dqtc, dkc, dvtc, lxc, lyc) = _in_projection(
            xc, g_norm1[layer], sh1c, sc1c, wi, None)

        oa = _swa_attention(sink_perm, qa, ka, va, kac, vac)
        dl = ret_decay_logit[layer]
        dl_lane = jnp.repeat(dl, HEAD_DIM, axis=1).reshape(2, 1, GROUP)
        dl_head = jnp.broadcast_to(dl[:, :, None, None], (2, N_HEADS, 8, RET_CHUNK))
        ryc, s_ctx = _retention(dl_lane, dl_head, rqc, rkc, rvc, jnp.zeros((b, 2, GROUP, GROUP), F32))
        ry, _ = _retention(dl_lane, dl_head, rq, rk, rv, s_ctx)
        k_all = jnp.concatenate([dk, dkc], axis=1)
        vt_all = jnp.concatenate([dvt, dvtc], axis=1)
        draw = _diff_attention(dqt, k_all, vt_all)
        assemble = functools.partial(_out_projection, ret_gn=ret_gn[layer], lam_vecs=diff_lambda[layer],
                                     diff_gn=diff_gn[layer], lam_init=lam_init, w_out=wo)
        lru = []
        lru_c = []
        for direction in range(2):
            prm = (conv_w[layer], conv_b[layer],
                   _block_diag(lru_wa[layer, direction]).astype(BF16), lru_ba[layer, direction],
                   _block_diag(lru_wx[layer, direction]).astype(BF16), lru_bx[layer, direction],
                   lru_lambda[layer, direction])
            rev = direction == 1
            hc = _lru_scan(lxc, *prm, jnp.zeros((b, 1, GROUP), F32), rev)
            h0 = hc[:, 0:1, :] if rev else hc[:, m - 1:m, :]
            lru.append(_lru_scan(lx, *prm, h0, rev))
            lru_c.append(hc)

        x = assemble(x, gt1, oa, ry, rg, draw=draw, hf=lru[0], hb=lru[1], ly=ly)
        j = layer // 2
        if layer % 2 == 0:
            ffn = functools.partial(_ffn, w1=ffn_w1[j].astype(BF16), w3=ffn_w3[j].astype(BF16),
                                    w2=ffn_w2[j].astype(BF16))
            x = ffn(x, g_norm2[layer], sh2, sc2, gt2)
        else:
            moe = functools.partial(_moe_ffn, final_g=final_norm, w1=moe_w1[j].astype(BF16),
                                    w3=moe_w3[j].astype(BF16), w2=moe_w2[j].astype(BF16),
                                    router=jnp.pad(moe_router[j], ((0, 0), (0, LANE - N_EXPERTS))))
            x = moe(x, g_norm2[layer], sh2, sc2, gt2, final_norm=layer == DEPTH - 1)
            ffn = functools.partial(moe, final_norm=False)

        if need_ctx:
            oac = _ctx_attention(sink_perm, qac, kac, vac)
            drawc = _diff_attention(dqtc, dkc, dvtc)
            xc = assemble(xc, gt1c, oac, ryc, rgc, draw=drawc, hf=lru_c[0], hb=lru_c[1], ly=lyc)
            xc = ffn(xc, g_norm2[layer], sh2c, sc2c, gt2c)
    assert DEPTH % 2 == 0, "the closing RMSNorm is fused into the last (routed) layer"
    return x
```

```python
import functools
import math

import jax
import jax.numpy as jnp
from jax import lax
from jax.experimental import pallas as pl
from jax.experimental.pallas import tpu as pltpu

F32 = jnp.float32
BF16 = jnp.bfloat16

D_MODEL = 1024
DEPTH = 2
GRID_W = 64
HEAD_DIM = 64
GROUP = D_MODEL // 4
N_HEADS = GROUP // HEAD_DIM
DIFF_DIM = HEAD_DIM // 2
WINDOW = 128
NORM_EPS = 1e-6
ROPE_BASE = 10000.0
LRU_BLOCKS = 4
LRU_C = 8.0
N_EXPERTS = 8
LANE = 128
LOG2E = 1.4426950408889634
DIFF_TQ = 256
DIFF_TK = 256
DIFF_VROWS = HEAD_DIM + 16

OFF_AQ, OFF_AK, OFF_AV = 0, 256, 384
OFF_RQ, OFF_RK, OFF_RV, OFF_RG = 512, 768, 1024, 1280
OFF_DQ, OFF_DK, OFF_DV = 1536, 1792, 2048
OFF_LX, OFF_LY = 2304, 2560
D_IN = 2816

VMEM_LIMIT = 56 * 1024 * 1024
ROWS_PROJ = 512
ROWS_FFN = 1024
ROWS_MIN = 256


def _row_tile(l, preferred):
    tm = preferred if l % preferred == 0 else ROWS_MIN
    assert l % tm == 0, (l, tm)
    return tm


def _params(*sem):
    return pltpu.CompilerParams(dimension_semantics=sem, vmem_limit_bytes=VMEM_LIMIT)


def _silu(x):
    return x * jax.nn.sigmoid(x)


def _softplus(x):
    return jnp.maximum(x, 0.0) + jnp.log(1.0 + jnp.exp(-jnp.abs(x)))


def _dot(a, b):
    return jnp.dot(a, b, preferred_element_type=F32)


def _dot_nt(a, b):
    return lax.dot_general(a, b, (((1,), (1,)), ((), ())), preferred_element_type=F32)


def _mod_kernel(c_ref, w_ref, b_ref, o_ref):
    s = _silu(c_ref[...])
    o_ref[0] = _dot(s.astype(BF16), w_ref[0].astype(BF16)) + b_ref[0]


def _mod_vectors(cc, w_ada, b_ada):
    depth, d, n6 = w_ada.shape
    tn = 1536
    return pl.pallas_call(
        _mod_kernel,
        out_shape=jax.ShapeDtypeStruct((depth, 8, n6), F32),
        grid=(depth, n6 // tn),
        in_specs=[pl.BlockSpec((8, d), lambda l, j: (0, 0)),
                  pl.BlockSpec((1, d, tn), lambda l, j: (l, 0, j)),
                  pl.BlockSpec((1, 1, tn), lambda l, j: (l, 0, j))],
        out_specs=pl.BlockSpec((1, 8, tn), lambda l, j: (l, 0, j)),
        compiler_params=_params("arbitrary", "arbitrary"),
        name="mod_vectors",
    )(cc, w_ada, b_ada.reshape(depth, 1, n6))


def _rope_tile(x, cos, sin_signed, d):
    lane = lax.broadcasted_iota(jnp.int32, x.shape, 1)
    first = (lane & (d - 1)) < (d // 2)
    partner = jnp.where(first, pltpu.roll(x, LANE - d // 2, axis=1), pltpu.roll(x, d // 2, axis=1))
    return x * cos + partner * sin_signed


def _inproj_kernel(*refs, rope, tm):
    if rope:
        (x_ref, g_ref, sh_ref, sc_ref, w_ref, ca_ref, sa_ref, cr_ref, sr_ref, cd_ref, sd_ref,
         qa_ref, ka_ref, va_ref, rq_ref, rk_ref, rv_ref, rg_ref,
         dqt_ref, dk_ref, dvt_ref, lx_ref, ly_ref) = refs
    else:
        (x_ref, g_ref, sh_ref, sc_ref, w_ref,
         qa_ref, ka_ref, va_ref, rq_ref, rk_ref, rv_ref, rg_ref,
         dqt_ref, dk_ref, dvt_ref, lx_ref, ly_ref) = refs
    x = x_ref[0]
    ms = jnp.mean(x * x, axis=-1, keepdims=True)
    y = x * lax.rsqrt(ms + NORM_EPS) * g_ref[...]
    h = y * (1.0 + sc_ref[0]) + sh_ref[0]
    hb = h.astype(BF16)

    def proj(off, width):
        return _dot(hb, w_ref[:, off:off + width])

    def roped(p, c_ref, s_ref, d, scale):
        tiles = []
        for t in range(p.shape[1] // LANE):
            pt = p[:, t * LANE:(t + 1) * LANE]
            if rope:
                pt = _rope_tile(pt, c_ref[...], s_ref[...], d)
            tiles.append(pt if scale == 1.0 else pt * scale)
        return tiles[0] if len(tiles) == 1 else jnp.concatenate(tiles, axis=1)

    ca = sa = cr = sr = cd = sd = None
    if rope:
        ca, sa, cr, sr, cd, sd = ca_ref, sa_ref, cr_ref, sr_ref, cd_ref, sd_ref
    qa_ref[0] = roped(proj(OFF_AQ, 256), ca, sa, HEAD_DIM, HEAD_DIM ** -0.5).astype(BF16)
    ka_ref[0] = roped(proj(OFF_AK, 128), ca, sa, HEAD_DIM, 1.0).astype(BF16)
    va_ref[0] = proj(OFF_AV, 128).astype(BF16)
    rq_ref[0] = roped(proj(OFF_RQ, 256), cr, sr, HEAD_DIM, 1.0).astype(BF16)
    rk_ref[0] = roped(proj(OFF_RK, 256), cr, sr, HEAD_DIM, HEAD_DIM ** -0.5).astype(BF16)
    rv_ref[0] = proj(OFF_RV, 256).astype(BF16)
    rg_ref[0] = proj(OFF_RG, 256)
    dq = roped(proj(OFF_DQ, 256), cd, sd, DIFF_DIM, DIFF_DIM ** -0.5 * LOG2E)
    dqt_ref[0] = dq.T.astype(BF16)
    dk_ref[0] = roped(proj(OFF_DK, 256), cd, sd, DIFF_DIM, 1.0).astype(BF16)
    dvt = proj(OFF_DV, 256).T.astype(BF16)
    ones_rows = (lax.broadcasted_iota(jnp.int32, (DIFF_VROWS - HEAD_DIM, DIFF_TK), 0) == 0).astype(BF16)
    for j in range(tm // DIFF_TK):
        for hd in range(N_HEADS):
            dvt_ref[0, j, hd, 0:HEAD_DIM, :] = dvt[hd * HEAD_DIM:(hd + 1) * HEAD_DIM, j * DIFF_TK:(j + 1) * DIFF_TK]
            dvt_ref[0, j, hd, HEAD_DIM:DIFF_VROWS, :] = ones_rows
    lx_ref[0] = proj(OFF_LX, 256)
    ly_ref[0] = proj(OFF_LY, 256)


def _in_projection(x, g, sh, sc, w_in, tables):
    b, l, d = x.shape
    rope = tables is not None
    tm = _row_tile(l, ROWS_PROJ)
    nt = l // tm
    row = lambda shape: pl.BlockSpec(shape, lambda i, j: (i, j, 0))
    vec = pl.BlockSpec((1, 1, d), lambda i, j: (i, 0, 0))
    in_specs = [row((1, tm, d)), pl.BlockSpec((1, d), lambda i, j: (0, 0)), vec, vec,
                pl.BlockSpec((d, D_IN), lambda i, j: (0, 0))]
    args = [x, g.reshape(1, d), sh, sc, w_in]
    if rope:
        in_specs += [pl.BlockSpec((tm, LANE), lambda i, j: (j, 0))] * 6
        args += list(tables)
    sds = jax.ShapeDtypeStruct
    out_shape = [sds((b, l, 256), BF16), sds((b, l, 128), BF16), sds((b, l, 128), BF16),
                 sds((b, l, 256), BF16), sds((b, l, 256), BF16), sds((b, l, 256), BF16),
                 sds((b, l, 256), F32),
                 sds((b, 256, l), BF16), sds((b, l, 256), BF16),
                 sds((b, l // DIFF_TK, N_HEADS, DIFF_VROWS, DIFF_TK), BF16),
                 sds((b, l, 256), F32), sds((b, l, 256), F32)]
    out_specs = [row((1, tm, 256)), row((1, tm, 128)), row((1, tm, 128)),
                 row((1, tm, 256)), row((1, tm, 256)), row((1, tm, 256)), row((1, tm, 256)),
                 pl.BlockSpec((1, 256, tm), lambda i, j: (i, 0, j)), row((1, tm, 256)),
                 pl.BlockSpec((1, tm // DIFF_TK, N_HEADS, DIFF_VROWS, DIFF_TK), lambda i, j: (i, j, 0, 0, 0)),
                 row((1, tm, 256)), row((1, tm, 256))]
    return pl.pallas_call(
        functools.partial(_inproj_kernel, rope=rope, tm=tm),
        out_shape=out_shape, grid=(b, nt), in_specs=in_specs, out_specs=out_specs,
        compiler_params=_params("arbitrary", "arbitrary"),
        name="in_projection_rope" if rope else "in_projection_ctx",
    )(*args)


def _swa_kernel(*refs, window, tq, n):
    if window:
        sink_ref, q_ref, k_ref, v_ref, kc_ref, vc_ref, o_ref, s_scr = refs
    else:
        sink_ref, q_ref, kc_ref, vc_ref, o_ref, s_scr = refs
    kc = kc_ref[0]
    vc = vc_ref[0]
    m = kc.shape[0]
    lane = lax.broadcasted_iota(jnp.int32, (1, LANE), 1)
    halves = [(lane >= half * HEAD_DIM) & (lane < (half + 1) * HEAD_DIM) for half in range(2)]

    def attend(q, kcat, vcat, valid):
        for t in range(2):
            qt = q[:, t * LANE:(t + 1) * LANE]
            for half in range(2):
                qm = jnp.where(halves[half], qt, jnp.zeros_like(qt))
                s = _dot_nt(qm, kcat)
                if valid is not None:
                    s = jnp.where(valid, s, -jnp.inf)
                s_scr[2 * t + half] = s
        out_tiles = []
        for t in range(2):
            out_t = None
            for half in range(2):
                sink = sink_ref[2 * t + half]
                s = s_scr[2 * t + half]
                mx = jnp.maximum(jnp.max(s, axis=1, keepdims=True), sink)
                e = jnp.exp(s - mx)
                den = jnp.sum(e, axis=1, keepdims=True) + jnp.exp(sink - mx)
                o = _dot(e.astype(BF16), vcat) / den
                out_t = o if out_t is None else jnp.where(halves[half], o, out_t)
            out_tiles.append(out_t)
        return jnp.concatenate(out_tiles, axis=1)

    if not window:
        o_ref[0] = attend(q_ref[0], kc, vc, None)
        return
    w = tq + 2 * WINDOW
    t0 = pl.program_id(1) * tq
    start = pl.multiple_of(jnp.clip(t0 - WINDOW, 0, n - w), WINDOW)
    kcat = jnp.concatenate([k_ref[0, pl.ds(start, w), :], kc], axis=0)
    vcat = jnp.concatenate([v_ref[0, pl.ds(start, w), :], vc], axis=0)
    qpos = t0 + lax.broadcasted_iota(jnp.int32, (tq, w + m), 0)
    col = lax.broadcasted_iota(jnp.int32, (tq, w + m), 1)
    valid = (jnp.abs(qpos - (start + col)) <= WINDOW) | (col >= w)
    o_ref[0] = attend(q_ref[0], kcat, vcat, valid)


def _swa_attention(sink_perm, q, k, v, kc, vc):
    b, n, _ = q.shape
    m = kc.shape[1]
    tq = 256
    smem = pl.BlockSpec(memory_space=pltpu.SMEM)
    ctx = pl.BlockSpec((1, m, 128), lambda i, j: (i, 0, 0))
    in_specs = [smem, pl.BlockSpec((1, tq, 256), lambda i, j: (i, j, 0)),
                pl.BlockSpec((1, n, 128), lambda i, j: (i, 0, 0)),
                pl.BlockSpec((1, n, 128), lambda i, j: (i, 0, 0)), ctx, ctx]
    return pl.pallas_call(
        functools.partial(_swa_kernel, window=True, tq=tq, n=n),
        out_shape=jax.ShapeDtypeStruct((b, n, 256), F32),
        grid=(b, n // tq), in_specs=in_specs,
        out_specs=pl.BlockSpec((1, tq, 256), lambda i, j: (i, j, 0)),
        scratch_shapes=[pltpu.VMEM((N_HEADS, tq, tq + 2 * WINDOW + m), F32)],
        compiler_params=_params("arbitrary", "arbitrary"),
        name="swa_attention",
    )(sink_perm, q, k, v, kc, vc)


def _ctx_attention(sink_perm, qc, kc, vc):
    b, m, _ = qc.shape
    smem = pl.BlockSpec(memory_space=pltpu.SMEM)
    ctx = pl.BlockSpec((1, m, 128), lambda i: (i, 0, 0))
    return pl.pallas_call(
        functools.partial(_swa_kernel, window=False, tq=m, n=m),
        out_shape=jax.ShapeDtypeStruct((b, m, 256), F32),
        grid=(b,), in_specs=[smem, pl.BlockSpec((1, m, 256), lambda i: (i, 0, 0)), ctx, ctx],
        out_specs=pl.BlockSpec((1, m, 256), lambda i: (i, 0, 0)),
        scratch_shapes=[pltpu.VMEM((N_HEADS, m, m), F32)],
        compiler_params=_params("arbitrary"),
        name="ctx_attention",
    )(sink_perm, qc, kc, vc)


RET_CHUNK = 128


def _ret_kernel(dl_lane_ref, dl_head_ref, q_ref, k_ref, v_ref, s0_ref, y_ref, sfin_ref,
                s_scr, dm_scr, xi_scr, zeta_scr, *, ns, group):
    c = RET_CHUNK
    nb = q_ref.shape[0]
    d = pl.program_id(0)
    si = pl.program_id(1)
    fwd = d == 0
    lane_head = lax.broadcasted_iota(jnp.int32, (1, GROUP), 1) // HEAD_DIM

    @pl.when(si == 0)
    def _():
        s_scr[...] = s0_ref[:, 0]
        lgl = -_softplus(-dl_lane_ref[0])
        row = lax.broadcasted_iota(jnp.int32, (c, 1), 0)
        xi_scr[...] = jnp.exp(lgl * jnp.where(fwd, row + 1, c - row).astype(F32))
        zeta_scr[...] = jnp.exp(lgl * jnp.where(fwd, c - 1 - row, row).astype(F32))
        ni = lax.broadcasted_iota(jnp.int32, (c, c), 0)
        mi = lax.broadcasted_iota(jnp.int32, (c, c), 1)
        dd = jnp.where(fwd, ni - mi, mi - ni)
        ddf = jnp.maximum(dd, 0).astype(F32)
        for h in range(N_HEADS):
            lgh = -_softplus(-dl_head_ref[0, h][0:1, :])
            dm_scr[h] = jnp.where(dd >= 0, jnp.exp(lgh * ddf), 0.0)

    dec = jnp.exp(-_softplus(-dl_lane_ref[0]) * float(c))
    rh = lax.broadcasted_iota(jnp.int32, (GROUP, GROUP), 0) // HEAD_DIM
    ch = lax.broadcasted_iota(jnp.int32, (GROUP, GROUP), 1) // HEAD_DIM
    same_head = rh == ch

    def chunk(g, carry):
        r0 = pl.multiple_of(jnp.where(fwd, g, group - 1 - g) * c, c)
        for bi in range(nb):
            q = q_ref[bi, pl.ds(r0, c), :]
            k = k_ref[bi, pl.ds(r0, c), :]
            v = v_ref[bi, pl.ds(r0, c), :]
            cross = _dot(q, s_scr[bi].astype(BF16)) * xi_scr[...]
            inner = jnp.zeros((c, GROUP), F32)
            for h in range(N_HEADS):
                qh = jnp.where(lane_head == h, q, jnp.zeros_like(q))
                s = _dot_nt(qh, k) * dm_scr[h]
                ih = _dot(s.astype(BF16), v)
                inner = jnp.where(lane_head == h, ih, inner)
            y_ref[bi, 0, pl.ds(r0, c), :] = inner + cross
            kz = (k.astype(F32) * zeta_scr[...]).T.astype(BF16)
            u = _dot(kz, v)
            s_scr[bi] = jnp.where(same_head, dec * s_scr[bi] + u, 0.0)
        return carry

    lax.fori_loop(0, group, chunk, 0, unroll=2)

    @pl.when(si == ns - 1)
    def _():
        sfin_ref[:, 0] = s_scr[...]


def _retention(dl_lane, dl_head, q, k, v, s0):
    b, l, _ = q.shape
    c = RET_CHUNK
    group = 8 if l % (8 * c) == 0 else 2
    ns = l // (group * c)
    sidx = lambda d, j: (0, jnp.where(d == 0, j, ns - 1 - j), 0)
    tok = pl.BlockSpec((b, group * c, 256), sidx)
    st = pl.BlockSpec((b, 1, 256, 256), lambda d, j: (0, d, 0, 0))
    return pl.pallas_call(
        functools.partial(_ret_kernel, ns=ns, group=group),
        out_shape=[jax.ShapeDtypeStruct((b, 2, l, 256), F32),
                   jax.ShapeDtypeStruct((b, 2, 256, 256), F32)],
        grid=(2, ns),
        in_specs=[pl.BlockSpec((1, 1, 256), lambda d, j: (d, 0, 0)),
                  pl.BlockSpec((1, N_HEADS, 8, c), lambda d, j: (d, 0, 0, 0)),
                  tok, tok, tok, st],
        out_specs=[pl.BlockSpec((b, 1, group * c, 256),
                                lambda d, j: (0, d, jnp.where(d == 0, j, ns - 1 - j), 0)),
                   st],
        scratch_shapes=[pltpu.VMEM((b, GROUP, GROUP), F32),
                        pltpu.VMEM((N_HEADS, c, c), F32),
                        pltpu.VMEM((c, GROUP), F32),
                        pltpu.VMEM((c, GROUP), F32)],
        compiler_params=_params("arbitrary", "arbitrary"),
        name="retention",
    )(dl_lane, dl_head, q, k, v, s0)


def _inner_tiles(tiles):
    for g in (8, 7, 6, 5, 4, 3, 2):
        if tiles % g == 0:
            return g
    return 1


def _diff_kernel(qt_ref, k_ref, vt_ref, o_ref, s_scr, *, tiles, nq):
    h = pl.program_id(1)
    inner = _inner_tiles(tiles)
    outer = tiles // inner
    total = nq * outer
    tq = DIFF_TQ
    rowi = lax.broadcasted_iota(jnp.int32, (LANE, 1), 0)
    hoff = (h % 2) * HEAD_DIM
    rowmask = [(rowi >= hoff + mp * DIFF_DIM) & (rowi < hoff + (mp + 1) * DIFF_DIM) for mp in range(2)]

    def step(mp, g_q, g_e, slot, m_e):
        mx = pv = None
        if g_q is not None:
            q0 = pl.multiple_of((g_q // outer) * tq, tq)
            qt = qt_ref[0, :, pl.ds(q0, tq)]
            qpad = jnp.where(rowmask[mp], qt, jnp.zeros_like(qt))
            kc = (g_q % outer) * inner
        if g_e is not None:
            vc = (g_e % outer) * inner
        for t in range(inner):
            sl = slice(t * DIFF_TK, (t + 1) * DIFF_TK)
            if g_q is not None:
                base = pl.multiple_of((kc + t) * DIFF_TK, DIFF_TK)
                s = _dot(k_ref[0, pl.ds(base, DIFF_TK), :], qpad)
                s_scr[mp, 1 - slot, sl, :] = s
                cm = jnp.max(s, axis=0, keepdims=True)
                mx = cm if mx is None else jnp.maximum(mx, cm)
            if g_e is not None:
                p = jnp.exp2(s_scr[mp, slot, sl, :] - m_e).astype(BF16)
                d = _dot(vt_ref[0, vc + t, 0], p)
                pv = d if pv is None else pv + d
        return mx, pv

    def advance(mp, g_q, g_e, slot, state):
        m_e, alpha, acc = state
        mx, pv = step(mp, g_q, g_e, slot, m_e)
        acc = alpha * acc + pv
        o0 = pl.multiple_of((g_e // outer) * tq, tq)
        o_ref[0, 0, mp, :, pl.ds(o0, tq)] = acc
        if g_q is None:
            return m_e, alpha, acc
        m_base = jnp.where(g_q % outer == 0, -jnp.inf, m_e)
        m_next = jnp.maximum(m_base, mx)
        return m_next, jnp.exp2(m_base - m_next), acc

    zeros = jnp.zeros((DIFF_VROWS, tq), F32)
    states = []
    for mp in range(2):
        m_first, _ = step(mp, 0, None, 1, None)
        states.append((m_first, jnp.zeros((1, tq), F32), zeros))
    states = tuple(states)

    def body(i, carry):
        carry = tuple(advance(mp, 2 * i + 1, 2 * i, 0, carry[mp]) for mp in range(2))
        return tuple(advance(mp, 2 * i + 2, 2 * i + 1, 1, carry[mp]) for mp in range(2))

    last = total - 1
    states = lax.fori_loop(0, last // 2, body, states, unroll=min(4, max(last // 2, 1)))
    if last % 2 == 1:
        states = tuple(advance(mp, last, last - 1, 0, states[mp]) for mp in range(2))
    for mp in range(2):
        advance(mp, None, last, last % 2, states[mp])


def _diff_attention(qt, k_all, vt_all):
    b, _, lq = qt.shape
    lk = k_all.shape[1]
    tiles = lk // DIFF_TK
    nq = lq // DIFF_TQ
    return pl.pallas_call(
        functools.partial(_diff_kernel, tiles=tiles, nq=nq),
        out_shape=jax.ShapeDtypeStruct((b, N_HEADS, 2, DIFF_VROWS, lq), F32),
        grid=(b, N_HEADS),
        in_specs=[pl.BlockSpec((1, LANE, lq), lambda i, h: (i, h // 2, 0)),
                  pl.BlockSpec((1, lk, LANE), lambda i, h: (i, 0, h // 2)),
                  pl.BlockSpec((1, tiles, 1, DIFF_VROWS, DIFF_TK), lambda i, h: (i, 0, h, 0, 0))],
        out_specs=pl.BlockSpec((1, 1, 2, DIFF_VROWS, lq), lambda i, h: (i, h, 0, 0, 0)),
        scratch_shapes=[pltpu.VMEM((2, 2, _inner_tiles(tiles) * DIFF_TK, DIFF_TQ), F32)],
        compiler_params=_params("arbitrary", "arbitrary"),
        name="diff_attention",
    )(qt, k_all, vt_all)


def _lru_kernel(x_ref, xp_ref, xn_ref, cw_ref, cb_ref, wa_ref, ba_ref, wx_ref, bx_ref, lam_ref,
                h0_ref, h_ref, carry, *, reverse, tm, nt):
    j = pl.program_id(1)
    tix = (nt - 1 - j) if reverse else j

    @pl.when(j == 0)
    def _():
        carry[...] = jnp.broadcast_to(h0_ref[0], carry.shape)

    x = x_ref[0]
    row = lax.broadcasted_iota(jnp.int32, (tm, 1), 0)
    prev = jnp.where(tix > 0, xp_ref[0][7:8, :], 0.0)
    nxt0 = jnp.where(tix < nt - 1, xn_ref[0][0:1, :], 0.0)
    nxt1 = jnp.where(tix < nt - 1, xn_ref[0][1:2, :], 0.0)
    xm1 = jnp.where(row == 0, prev, pltpu.roll(x, 1, axis=0))
    xp1 = jnp.where(row == tm - 1, nxt0, pltpu.roll(x, tm - 1, axis=0))
    xp2 = jnp.where(row == tm - 1, nxt1, jnp.where(row == tm - 2, nxt0, pltpu.roll(x, tm - 2, axis=0)))
    cw = cw_ref[...]
    u = cw[0:1] * xm1 + cw[1:2] * x + cw[2:3] * xp1 + cw[3:4] * xp2 + cb_ref[...]
    ub = u.astype(BF16)
    r = jax.nn.sigmoid(_dot(ub, wa_ref[0]) + ba_ref[0])
    gi = jax.nn.sigmoid(_dot(ub, wx_ref[0]) + bx_ref[0])
    log_a = -LRU_C * r * _softplus(-lam_ref[0])
    a = jnp.exp(log_a)
    th = jnp.tanh(log_a)
    bv = jnp.sqrt(-2.0 * th / (1.0 - th)) * (gi * u)
    k = 1
    while k < tm:
        if k < 8:
            keep = (row < tm - k) if reverse else (row >= k)
            shift = (tm - k) if reverse else k
            a_sh = jnp.where(keep, pltpu.roll(a, shift, axis=0), 1.0)
            b_sh = jnp.where(keep, pltpu.roll(bv, shift, axis=0), 0.0)
            bv = a * b_sh + bv
            a = a * a_sh
        elif reverse:
            head_b = a[:tm - k] * bv[k:] + bv[:tm - k]
            head_a = a[:tm - k] * a[k:]
            bv = jnp.concatenate([head_b, bv[tm - k:]], axis=0)
            a = jnp.concatenate([head_a, a[tm - k:]], axis=0)
        else:
            tail_b = a[k:] * bv[:tm - k] + bv[k:]
            tail_a = a[k:] * a[:tm - k]
            bv = jnp.concatenate([bv[:k], tail_b], axis=0)
            a = jnp.concatenate([a[:k], tail_a], axis=0)
        k *= 2
    hcur = a * carry[0:1, :] + bv
    h_ref[0] = hcur
    last = hcur[0:1, :] if reverse else hcur[tm - 1:tm, :]
    carry[...] = jnp.broadcast_to(last, carry.shape)


def _lru_scan(lx, conv_w, conv_b, wa_bd, ba, wx_bd, bx, lam, h0, reverse):
    b, l, w = lx.shape
    tm = _row_tile(l, ROWS_PROJ)
    nt = l // tm
    tb = tm // 8
    tidx = (lambda j: nt - 1 - j) if reverse else (lambda j: j)
    vec = pl.BlockSpec((1, w), lambda i, j: (0, 0))
    mat = pl.BlockSpec((1, w, w), lambda i, j: (0, 0, 0))
    return pl.pallas_call(
        functools.partial(_lru_kernel, reverse=reverse, tm=tm, nt=nt),
        out_shape=jax.ShapeDtypeStruct((b, l, w), F32),
        grid=(b, nt),
        in_specs=[pl.BlockSpec((1, tm, w), lambda i, j: (i, tidx(j), 0)),
                  pl.BlockSpec((1, 8, w), lambda i, j: (i, jnp.maximum(tidx(j) * tb - 1, 0), 0)),
                  pl.BlockSpec((1, 8, w), lambda i, j: (i, jnp.minimum((tidx(j) + 1) * tb, l // 8 - 1), 0)),
                  pl.BlockSpec((4, w), lambda i, j: (0, 0)), vec,
                  mat, vec, mat, vec, vec,
                  pl.BlockSpec((1, 1, w), lambda i, j: (i, 0, 0))],
        out_specs=pl.BlockSpec((1, tm, w), lambda i, j: (i, tidx(j), 0)),
        scratch_shapes=[pltpu.VMEM((8, w), F32)],
        compiler_params=_params("arbitrary", "arbitrary"),
        name="lru_scan_rev" if reverse else "lru_scan_fwd",
    )(lx, lx, lx, conv_w, conv_b.reshape(1, w), wa_bd.reshape(1, w, w), ba.reshape(1, w),
      wx_bd.reshape(1, w, w), bx.reshape(1, w), lam.reshape(1, w), h0)


def _outproj_kernel(x_ref, gt_ref, oa_ref, ry_ref, rg_ref, rgn_ref, lam_ref, dgn_ref, draw_ref, hf_ref, hb_ref,
                    ly_ref, bd_ref, w_ref, o_ref, *, lam_init):
    oa = oa_ref[0]
    y = ry_ref[0, 0] + ry_ref[0, 1]
    y2 = y * y
    hi = y2.astype(BF16)
    lo = (y2 - hi.astype(F32)).astype(BF16)
    ms = _dot(hi, bd_ref[...]) + _dot(lo, bd_ref[...])
    ob = y * lax.rsqrt(ms + NORM_EPS) * rgn_ref[...] * _silu(rg_ref[0])
    lv = lam_ref[...]
    lam = (jnp.exp(jnp.sum(lv[0:1] * lv[1:2], axis=1, keepdims=True))
           - jnp.exp(jnp.sum(lv[2:3] * lv[3:4], axis=1, keepdims=True)) + lam_init)
    heads = []
    for hd in range(N_HEADS):
        a0 = draw_ref[0, hd, 0]
        a1 = draw_ref[0, hd, 1]
        oh = (a0[0:HEAD_DIM] / a0[HEAD_DIM:HEAD_DIM + 1]
              - lam * (a1[0:HEAD_DIM] / a1[HEAD_DIM:HEAD_DIM + 1]))
        msd = jnp.mean(oh * oh, axis=0, keepdims=True)
        heads.append(oh * lax.rsqrt(msd + NORM_EPS) * (dgn_ref[hd] * (1.0 - lam_init)))
    od = jnp.concatenate(heads, axis=0).T
    ol = (hf_ref[0] + hb_ref[0]) * jax.nn.gelu(ly_ref[0])
    acc = _dot(oa.astype(BF16), w_ref[0:256, :])
    acc = acc + _dot(ob.astype(BF16), w_ref[256:512, :])
    acc = acc + _dot(od.astype(BF16), w_ref[512:768, :])
    acc = acc + _dot(ol.astype(BF16), w_ref[768:1024, :])
    o_ref[0] = x_ref[0] + gt_ref[0] * acc


def _out_projection(x, gt, oa, ry, rg, ret_gn, lam_vecs, diff_gn, lam_init, draw, hf, hb, ly, w_out):
    b, l, d = x.shape
    tm = _row_tile(l, ROWS_PROJ)
    row = lambda wdt: pl.BlockSpec((1, tm, wdt), lambda i, j: (i, j, 0))
    hd = jnp.arange(GROUP) // HEAD_DIM
    bd = jnp.where(hd[:, None] == hd[None, :], 1.0 / HEAD_DIM, 0.0).astype(BF16)
    return pl.pallas_call(
        functools.partial(_outproj_kernel, lam_init=lam_init),
        out_shape=jax.ShapeDtypeStruct((b, l, d), F32),
        grid=(b, l // tm),
        in_specs=[row(d), pl.BlockSpec((1, 1, d), lambda i, j: (i, 0, 0)), row(256),
                  pl.BlockSpec((1, 2, tm, 256), lambda i, j: (i, 0, j, 0)), row(256),
                  pl.BlockSpec((1, 256), lambda i, j: (0, 0)),
                  pl.BlockSpec((4, DIFF_DIM), lambda i, j: (0, 0)),
                  pl.BlockSpec((N_HEADS, HEAD_DIM, 1), lambda i, j: (0, 0, 0)),
                  pl.BlockSpec((1, N_HEADS, 2, DIFF_VROWS, tm), lambda i, j: (i, 0, 0, 0, j)),
                  row(256), row(256), row(256),
                  pl.BlockSpec((256, 256), lambda i, j: (0, 0)),
                  pl.BlockSpec((d, d), lambda i, j: (0, 0))],
        out_specs=row(d),
        compiler_params=_params("arbitrary", "arbitrary"),
        name="out_projection",
    )(x, gt, oa, ry, rg, ret_gn.reshape(1, 256), lam_vecs, diff_gn.reshape(N_HEADS, HEAD_DIM, 1), draw,
      hf, hb, ly, bd, w_out)


def _norm_modulate(x_ref, g_ref, sh_ref, sc_ref):
    x = x_ref[0]
    ms = jnp.mean(x * x, axis=-1, keepdims=True)
    y = x * lax.rsqrt(ms + NORM_EPS) * g_ref[...]
    return y * (1.0 + sc_ref[0]) + sh_ref[0]


def _ffn_kernel(x_ref, g_ref, sh_ref, sc_ref, gt_ref, w1_ref, w3_ref, w2_ref, o_ref, hb_scr, acc_scr,
                *, n_f):
    f = pl.program_id(2)

    @pl.when(f == 0)
    def _():
        hb_scr[...] = _norm_modulate(x_ref, g_ref, sh_ref, sc_ref).astype(BF16)
        acc_scr[...] = jnp.zeros_like(acc_scr)

    hb = hb_scr[...]
    a = _silu(_dot(hb, w1_ref[...])) * _dot(hb, w3_ref[...])
    acc_scr[...] += _dot(a.astype(BF16), w2_ref[...])

    @pl.when(f == n_f - 1)
    def _():
        o_ref[0] = x_ref[0] + gt_ref[0] * acc_scr[...]


def _ffn(x, g, sh, sc, gt, w1, w3, w2):
    b, l, d = x.shape
    ff = w1.shape[1]
    tm = _row_tile(l, ROWS_FFN)
    n_f = 2
    fc = ff // n_f
    vec = pl.BlockSpec((1, 1, d), lambda i, j, f: (i, 0, 0))
    return pl.pallas_call(
        functools.partial(_ffn_kernel, n_f=n_f),
        out_shape=jax.ShapeDtypeStruct((b, l, d), F32),
        grid=(b, l // tm, n_f),
        in_specs=[pl.BlockSpec((1, tm, d), lambda i, j, f: (i, j, 0)),
                  pl.BlockSpec((1, d), lambda i, j, f: (0, 0)), vec, vec, vec,
                  pl.BlockSpec((d, fc), lambda i, j, f: (0, f)),
                  pl.BlockSpec((d, fc), lambda i, j, f: (0, f)),
                  pl.BlockSpec((fc, d), lambda i, j, f: (f, 0))],
        out_specs=pl.BlockSpec((1, tm, d), lambda i, j, f: (i, j, 0)),
        scratch_shapes=[pltpu.VMEM((tm, d), BF16), pltpu.VMEM((tm, d), F32)],
        compiler_params=_params("arbitrary", "arbitrary", "arbitrary"),
        name="dense_ffn",
    )(x, g.reshape(1, d), sh, sc, gt, w1, w3, w2)


MOE_RT = 128


def _moe_kernel(x_ref, g_ref, sh_ref, sc_ref, gt_ref, fg_ref, rt_ref, w1_ref, w3_ref, w2_ref, o_ref,
                hb_scr, gate_scr, pos_scr, gatet_scr, post_scr, x_scr, y_scr, gs_scr, cnt_smem,
                *, n_e, n_f, tm, final_norm):
    e = pl.program_id(2)
    f = pl.program_id(3)
    rt2 = 2 * MOE_RT

    @pl.when((e == 0) & (f == 0))
    def _():
        h = _norm_modulate(x_ref, g_ref, sh_ref, sc_ref)
        hb = h.astype(BF16)
        hb_scr[...] = hb
        o_ref[0] = jnp.zeros(o_ref.shape[1:], F32)
        rt = rt_ref[...]
        rhi = rt.astype(BF16)
        rlo = (rt - rhi.astype(F32)).astype(BF16)
        hlo = (h - hb.astype(F32)).astype(BF16)
        logits = _dot(hb, rhi) + _dot(hb, rlo) + _dot(hlo, rhi)
        lane = lax.broadcasted_iota(jnp.int32, logits.shape, 1)
        logits = jnp.where(lane < N_EXPERTS, logits, -jnp.inf)
        m1 = jnp.max(logits, axis=1, keepdims=True)
        i1 = jnp.min(jnp.where(logits == m1, lane, LANE), axis=1, keepdims=True)
        rest = jnp.where(lane == i1, -jnp.inf, logits)
        m2 = jnp.max(rest, axis=1, keepdims=True)
        i2 = jnp.min(jnp.where(rest == m2, lane, LANE), axis=1, keepdims=True)
        e2 = jnp.exp(m2 - m1)
        den = 1.0 + e2
        gate = jnp.where(lane == i1, 1.0 / den, 0.0) + jnp.where(lane == i2, e2 / den, 0.0)
        gate_scr[...] = gate
        sel = jnp.where(gate > 0.0, 1.0, 0.0)
        selb = sel.astype(BF16)
        strip = 256
        for r0 in range(0, tm, strip):
            tok_r = r0 + lax.broadcasted_iota(jnp.int32, (strip, tm), 0)
            tok_c = lax.broadcasted_iota(jnp.int32, (strip, tm), 1)
            earlier = jnp.where(tok_c < tok_r, 1.0, 0.0).astype(BF16)
            pos_scr[r0:r0 + strip, :] = _dot(earlier, selb)
        post_scr[...] = pos_scr[...].T
        gatet_scr[...] = gate.T
        counts = jnp.sum(sel, axis=0, keepdims=True)
        for ee in range(n_e):
            cnt_smem[ee] = jnp.sum(jnp.where(lane[0:1] == ee, counts, 0.0)).astype(jnp.int32)

    n_tiles = (cnt_smem[e] + (MOE_RT - 1)) // MOE_RT
    n_pairs = (n_tiles + 1) // 2

    @pl.when(f == 0)
    def _():
        sub = lax.broadcasted_iota(jnp.int32, (8, 1), 0)
        pos_e = jnp.sum(jnp.where(sub == e, post_scr[0:8, :], 0.0), axis=0, keepdims=True)
        gate_e = jnp.sum(jnp.where(sub == e, gatet_scr[0:8, :], 0.0), axis=0, keepdims=True)

        def gather(jj, carry):
            r0 = pl.multiple_of(jj * rt2, rt2)
            slot = (r0 + lax.broadcasted_iota(jnp.int32, (rt2, 1), 0)).astype(F32)
            hit = (pos_e == slot) & (gate_e > 0.0)
            onehot = jnp.where(hit, 1.0, 0.0).astype(BF16)
            x_scr[pl.ds(r0, rt2), :] = _dot(onehot, hb_scr[...]).astype(BF16)
            gs = jnp.sum(jnp.where(hit, gate_e, 0.0), axis=1, keepdims=True)
            gs_scr[pl.ds(r0, rt2), :] = jnp.broadcast_to(gs, (rt2, LANE))
            y_scr[pl.ds(r0, rt2), :] = jnp.zeros((rt2, y_scr.shape[1]), F32)
            return carry

        lax.fori_loop(0, n_pairs, gather, 0)

    def expert(r0, rows):
        xj = x_scr[pl.ds(r0, rows), :]
        a = _silu(_dot(xj, w1_ref[0])) * _dot(xj, w3_ref[0])
        y_scr[pl.ds(r0, rows), :] += _dot(a.astype(BF16), w2_ref[0])

    def expert_pair(jj, carry):
        expert(pl.multiple_of(jj * rt2, rt2), rt2)
        return carry

    lax.fori_loop(0, n_tiles // 2, expert_pair, 0)

    @pl.when(n_tiles % 2 == 1)
    def _():
        expert(pl.multiple_of((n_tiles - 1) * MOE_RT, MOE_RT), MOE_RT)

    @pl.when(f == n_f - 1)
    def _():
        lane = lax.broadcasted_iota(jnp.int32, (1, LANE), 1)
        pos_col = jnp.sum(jnp.where(lane == e, pos_scr[...], 0.0), axis=1, keepdims=True)
        gate_col = jnp.sum(jnp.where(lane == e, gate_scr[...], 0.0), axis=1, keepdims=True)

        def scatter(jj, carry):
            r0 = pl.multiple_of(jj * rt2, rt2)
            slot = (r0 + lax.broadcasted_iota(jnp.int32, (1, rt2), 1)).astype(F32)
            onehot = jnp.where((pos_col == slot) & (gate_col > 0.0), 1.0, 0.0).astype(BF16)
            ys = (y_scr[pl.ds(r0, rt2), :] * gs_scr[pl.ds(r0, rt2), 0:1]).astype(BF16)
            o_ref[0] += _dot(onehot, ys)
            return carry

        lax.fori_loop(0, n_pairs, scatter, 0)

    @pl.when((e == n_e - 1) & (f == n_f - 1))
    def _():
        out = x_ref[0] + gt_ref[0] * o_ref[0]
        if final_norm:
            ms = jnp.mean(out * out, axis=-1, keepdims=True)
            out = out * lax.rsqrt(ms + NORM_EPS) * fg_ref[...]
        o_ref[0] = out


def _moe_ffn(x, g, sh, sc, gt, final_g, w1, w3, w2, router, final_norm):
    b, l, d = x.shape
    n_e, _, ff = w1.shape
    tm = _row_tile(l, ROWS_FFN)
    n_f = 2
    fc = ff // n_f
    vec = pl.BlockSpec((1, 1, d), lambda i, j, e, f: (i, 0, 0))
    return pl.pallas_call(
        functools.partial(_moe_kernel, n_e=n_e, n_f=n_f, tm=tm, final_norm=final_norm),
        out_shape=jax.ShapeDtypeStruct((b, l, d), F32),
        grid=(b, l // tm, n_e, n_f),
        in_specs=[pl.BlockSpec((1, tm, d), lambda i, j, e, f: (i, j, 0)),
                  pl.BlockSpec((1, d), lambda i, j, e, f: (0, 0)), vec, vec, vec,
                  pl.BlockSpec((1, d), lambda i, j, e, f: (0, 0)),
                  pl.BlockSpec((d, LANE), lambda i, j, e, f: (0, 0)),
                  pl.BlockSpec((1, d, fc), lambda i, j, e, f: (e, 0, f)),
                  pl.BlockSpec((1, d, fc), lambda i, j, e, f: (e, 0, f)),
                  pl.BlockSpec((1, fc, d), lambda i, j, e, f: (e, f, 0))],
        out_specs=pl.BlockSpec((1, tm, d), lambda i, j, e, f: (i, j, 0)),
        scratch_shapes=[pltpu.VMEM((tm, d), BF16),
                        pltpu.VMEM((tm, LANE), F32),
                        pltpu.VMEM((tm, LANE), F32),
                        pltpu.VMEM((LANE, tm), F32),
                        pltpu.VMEM((LANE, tm), F32),
                        pltpu.VMEM((tm, d), BF16),
                        pltpu.VMEM((tm, d), F32),
                        pltpu.VMEM((tm, LANE), F32),
                        pltpu.SMEM((n_e,), jnp.int32)],
        compiler_params=_params("arbitrary", "arbitrary", "arbitrary", "arbitrary"),
        name="moe_ffn",
    )(x, g.reshape(1, d), sh, sc, gt, final_g.reshape(1, d), router, w1, w3, w2)


def _rope_tables(n):
    rows = n // GRID_W
    row = jnp.repeat(jnp.arange(rows, dtype=F32), GRID_W)
    col = jnp.tile(jnp.arange(GRID_W, dtype=F32), rows)

    def axial(dim):
        nf = dim // 4
        inv = ROPE_BASE ** (-jnp.arange(nf, dtype=F32) / nf)
        return jnp.concatenate([row[:, None] * inv, col[:, None] * inv], axis=-1)

    def line(dim):
        nf = dim // 2
        inv = ROPE_BASE ** (-jnp.arange(nf, dtype=F32) / nf)
        return jnp.arange(n, dtype=F32)[:, None] * inv

    def expand(ang, dim):
        c = jnp.cos(ang)
        s = jnp.sin(ang)
        c = jnp.tile(jnp.concatenate([c, c], axis=-1), (1, LANE // dim))
        s = jnp.tile(jnp.concatenate([-s, s], axis=-1), (1, LANE // dim))
        return c, s

    ca, sa = expand(axial(HEAD_DIM), HEAD_DIM)
    cr, sr = expand(line(HEAD_DIM), HEAD_DIM)
    cd, sd = expand(axial(DIFF_DIM), DIFF_DIM)
    return ca, sa, cr, sr, cd, sd


_ATTN_HEAD_ORDER = (0, 2, 1, 3)


def _attn_perm():
    return jnp.concatenate([jnp.arange(HEAD_DIM) + HEAD_DIM * h for h in _ATTN_HEAD_ORDER])


def _block_diag(w):
    nb, blk, _ = w.shape
    out = jnp.zeros((nb * blk, nb * blk), w.dtype)
    for i in range(nb):
        out = out.at[i * blk:(i + 1) * blk, i * blk:(i + 1) * blk].set(w[i])
    return out


def kernel(x, c, ctx, c_ctx, w_ada, b_ada, g_norm1, g_norm2, w_in, w_out, attn_sink, ret_decay_logit,
           ret_gn, diff_lambda, diff_gn, conv_w, conv_b, lru_wa, lru_ba, lru_wx, lru_bx, lru_lambda,
           ffn_w1, ffn_w3, ffn_w2, moe_router, moe_w1, moe_w3, moe_w2, final_norm):
    b, n, d = x.shape
    m = ctx.shape[1]
    tables = _rope_tables(n)
    perm = _attn_perm()

    cc = jnp.concatenate([c, c_ctx[None, :], jnp.zeros((8 - b - 1, d), F32)], axis=0)
    mods = _mod_vectors(cc, w_ada, b_ada)

    xc = ctx
    for layer in range(DEPTH):
        need_ctx = layer < DEPTH - 1
        lam_init = 0.8 - 0.6 * math.exp(-0.3 * layer)
        mod = mods[layer]
        sh1, sc1, gt1, sh2, sc2, gt2 = [mod[:b, None, i * d:(i + 1) * d] for i in range(6)]
        sh1c, sc1c, gt1c, sh2c, sc2c, gt2c = [
            jnp.broadcast_to(mod[b:b + 1, None, i * d:(i + 1) * d], (b, 1, d)) for i in range(6)]

        wi = w_in[layer]
        wi = jnp.concatenate([wi[:, :256][:, perm], wi[:, 256:]], axis=1).astype(BF16)
        wo = w_out[layer]
        wo = jnp.concatenate([wo[:256][perm], wo[256:]], axis=0).astype(BF16)
        sink_perm = attn_sink[layer][jnp.array(_ATTN_HEAD_ORDER)]

        (qa, ka, va, rq, rk, rv, rg, dqt, dk, dvt, lx, ly) = _in_projection(
            x, g_norm1[layer], sh1, sc1, wi, tables)
        (qac, kac, vac, rqc, rkc, rvc, rgc, dqtc, dkc, dvtc, lxc, lyc) = _in_projection(
            xc, g_norm1[layer], sh1c, sc1c, wi, None)

        oa = _swa_attention(sink_perm, qa, ka, va, kac, vac)
        dl = ret_decay_logit[layer]
        dl_lane = jnp.repeat(dl, HEAD_DIM, axis=1).reshape(2, 1, GROUP)
        dl_head = jnp.broadcast_to(dl[:, :, None, None], (2, N_HEADS, 8, RET_CHUNK))
        ryc, s_ctx = _retention(dl_lane, dl_head, rqc, rkc, rvc, jnp.zeros((b, 2, GROUP, GROUP), F32))
        ry, _ = _retention(dl_lane, dl_head, rq, rk, rv, s_ctx)
        k_all = jnp.concatenate([dk, dkc], axis=1)
        vt_all = jnp.concatenate([dvt, dvtc], axis=1)
        draw = _diff_attention(dqt, k_all, vt_all)
        assemble = functools.partial(_out_projection, ret_gn=ret_gn[layer], lam_vecs=diff_lambda[layer],
                                     diff_gn=diff_gn[layer], lam_init=lam_init, w_out=wo)
        lru = []
        lru_c = []
        for direction in range(2):
            prm = (conv_w[layer], conv_b[layer],
                   _block_diag(lru_wa[layer, direction]).astype(BF16), lru_ba[layer, direction],
                   _block_diag(lru_wx[layer, direction]).astype(BF16), lru_bx[layer, direction],
                   lru_lambda[layer, direction])
            rev = direction == 1
            hc = _lru_scan(lxc, *prm, jnp.zeros((b, 1, GROUP), F32), rev)
            h0 = hc[:, 0:1, :] if rev else hc[:, m - 1:m, :]
            lru.append(_lru_scan(lx, *prm, h0, rev))
            lru_c.append(hc)

        x = assemble(x, gt1, oa, ry, rg, draw=draw, hf=lru[0], hb=lru[1], ly=ly)
        j = layer // 2
        if layer % 2 == 0:
            ffn = functools.partial(_ffn, w1=ffn_w1[j].astype(BF16), w3=ffn_w3[j].astype(BF16),
                                    w2=ffn_w2[j].astype(BF16))
            x = ffn(x, g_norm2[layer], sh2, sc2, gt2)
        else:
            moe = functools.partial(_moe_ffn, final_g=final_norm, w1=moe_w1[j].astype(BF16),
                                    w3=moe_w3[j].astype(BF16), w2=moe_w2[j].astype(BF16),
                                    router=jnp.pad(moe_router[j], ((0, 0), (0, LANE - N_EXPERTS))))
            x = moe(x, g_norm2[layer], sh2, sc2, gt2, final_norm=layer == DEPTH - 1)
            ffn = functools.partial(moe, final_norm=False)

        if need_ctx:
            oac = _ctx_attention(sink_perm, qac, kac, vac)
            drawc = _diff_attention(dqtc, dkc, dvtc)
            xc = assemble(xc, gt1c, oac, ryc, rgc, draw=drawc, hf=lru_c[0], hb=lru_c[1], ly=lyc)
            xc = ffn(xc, g_norm2[layer], sh2c, sc2c, gt2c)
    assert DEPTH % 2 == 0, "the closing RMSNorm is fused into the last (routed) layer"
    return x
```

```python
import functools
import math

import jax
import jax.numpy as jnp
from jax import lax
from jax.experimental import pallas as pl
from jax.experimental.pallas import tpu as pltpu

F32 = jnp.float32
BF16 = jnp.bfloat16

D_MODEL = 1024
DEPTH = 2
GRID_W = 64
HEAD_DIM = 64
GROUP = D_MODEL // 4
N_HEADS = GROUP // HEAD_DIM
DIFF_DIM = HEAD_DIM // 2
WINDOW = 128
NORM_EPS = 1e-6
ROPE_BASE = 10000.0
LRU_BLOCKS = 4
LRU_C = 8.0
N_EXPERTS = 8
LANE = 128
LOG2E = 1.4426950408889634
DIFF_TQ = 256
DIFF_TK = 256
DIFF_VROWS = HEAD_DIM + 16

OFF_AQ, OFF_AK, OFF_AV = 0, 256, 384
OFF_RQ, OFF_RK, OFF_RV, OFF_RG = 512, 768, 1024, 1280
OFF_DQ, OFF_DK, OFF_DV = 1536, 1792, 2048
OFF_LX, OFF_LY = 2304, 2560
D_IN = 2816

VMEM_LIMIT = 56 * 1024 * 1024
ROWS_PROJ = 512
ROWS_FFN = 1024
ROWS_MIN = 256


def _row_tile(l, preferred):
    tm = preferred if l % preferred == 0 else ROWS_MIN
    assert l % tm == 0, (l, tm)
    return tm


def _params(*sem):
    return pltpu.CompilerParams(dimension_semantics=sem, vmem_limit_bytes=VMEM_LIMIT)


def _silu(x):
    return x * jax.nn.sigmoid(x)


def _softplus(x):
    return jnp.maximum(x, 0.0) + jnp.log(1.0 + jnp.exp(-jnp.abs(x)))


def _dot(a, b):
    return jnp.dot(a, b, preferred_element_type=F32)


def _dot_nt(a, b):
    return lax.dot_general(a, b, (((1,), (1,)), ((), ())), preferred_element_type=F32)


def _mod_kernel(c_ref, w_ref, b_ref, o_ref):
    s = _silu(c_ref[...])
    o_ref[0] = _dot(s.astype(BF16), w_ref[0].astype(BF16)) + b_ref[0]


def _mod_vectors(cc, w_ada, b_ada):
    depth, d, n6 = w_ada.shape
    tn = 1536
    return pl.pallas_call(
        _mod_kernel,
        out_shape=jax.ShapeDtypeStruct((depth, 8, n6), F32),
        grid=(depth, n6 // tn),
        in_specs=[pl.BlockSpec((8, d), lambda l, j: (0, 0)),
                  pl.BlockSpec((1, d, tn), lambda l, j: (l, 0, j)),
                  pl.BlockSpec((1, 1, tn), lambda l, j: (l, 0, j))],
        out_specs=pl.BlockSpec((1, 8, tn), lambda l, j: (l, 0, j)),
        compiler_params=_params("arbitrary", "arbitrary"),
        name="mod_vectors",
    )(cc, w_ada, b_ada.reshape(depth, 1, n6))


def _rope_tile(x, cos, sin_signed, d):
    lane = lax.broadcasted_iota(jnp.int32, x.shape, 1)
    first = (lane & (d - 1)) < (d // 2)
    partner = jnp.where(first, pltpu.roll(x, LANE - d // 2, axis=1), pltpu.roll(x, d // 2, axis=1))
    return x * cos + partner * sin_signed


def _inproj_kernel(*refs, rope, tm):
    if rope:
        (x_ref, g_ref, sh_ref, sc_ref, w_ref, ca_ref, sa_ref, cr_ref, sr_ref, cd_ref, sd_ref,
         qa_ref, ka_ref, va_ref, rq_ref, rk_ref, rv_ref, rg_ref,
         dqt_ref, dk_ref, dvt_ref, lx_ref, ly_ref) = refs
    else:
        (x_ref, g_ref, sh_ref, sc_ref, w_ref,
         qa_ref, ka_ref, va_ref, rq_ref, rk_ref, rv_ref, rg_ref,
         dqt_ref, dk_ref, dvt_ref, lx_ref, ly_ref) = refs
    x = x_ref[0]
    ms = jnp.mean(x * x, axis=-1, keepdims=True)
    y = x * lax.rsqrt(ms + NORM_EPS) * g_ref[...]
    h = y * (1.0 + sc_ref[0]) + sh_ref[0]
    hb = h.astype(BF16)

    def proj(off, width):
        return _dot(hb, w_ref[:, off:off + width])

    def roped(p, c_ref, s_ref, d, scale):
        tiles = []
        for t in range(p.shape[1] // LANE):
            pt = p[:, t * LANE:(t + 1) * LANE]
            if rope:
                pt = _rope_tile(pt, c_ref[...], s_ref[...], d)
            tiles.append(pt if scale == 1.0 else pt * scale)
        return tiles[0] if len(tiles) == 1 else jnp.concatenate(tiles, axis=1)

    ca = sa = cr = sr = cd = sd = None
    if rope:
        ca, sa, cr, sr, cd, sd = ca_ref, sa_ref, cr_ref, sr_ref, cd_ref, sd_ref
    qa_ref[0] = roped(proj(OFF_AQ, 256), ca, sa, HEAD_DIM, HEAD_DIM ** -0.5).astype(BF16)
    ka_ref[0] = roped(proj(OFF_AK, 128), ca, sa, HEAD_DIM, 1.0).astype(BF16)
    va_ref[0] = proj(OFF_AV, 128).astype(BF16)
    rq_ref[0] = roped(proj(OFF_RQ, 256), cr, sr, HEAD_DIM, 1.0).astype(BF16)
    rk_ref[0] = roped(proj(OFF_RK, 256), cr, sr, HEAD_DIM, HEAD_DIM ** -0.5).astype(BF16)
    rv_ref[0] = proj(OFF_RV, 256).astype(BF16)
    rg_ref[0] = proj(OFF_RG, 256)
    dq = roped(proj(OFF_DQ, 256), cd, sd, DIFF_DIM, DIFF_DIM ** -0.5 * LOG2E)
    dqt_ref[0] = dq.T.astype(BF16)
    dk_ref[0] = roped(proj(OFF_DK, 256), cd, sd, DIFF_DIM, 1.0).astype(BF16)
    dvt = proj(OFF_DV, 256).T.astype(BF16)
    ones_rows = (lax.broadcasted_iota(jnp.int32, (DIFF_VROWS - HEAD_DIM, DIFF_TK), 0) == 0).astype(BF16)
    for j in range(tm // DIFF_TK):
        for hd in range(N_HEADS):
            dvt_ref[0, j, hd, 0:HEAD_DIM, :] = dvt[hd * HEAD_DIM:(hd + 1) * HEAD_DIM, j * DIFF_TK:(j + 1) * DIFF_TK]
            dvt_ref[0, j, hd, HEAD_DIM:DIFF_VROWS, :] = ones_rows
    lx_ref[0] = proj(OFF_LX, 256)
    ly_ref[0] = proj(OFF_LY, 256)


def _in_projection(x, g, sh, sc, w_in, tables):
    b, l, d = x.shape
    rope = tables is not None
    tm = _row_tile(l, ROWS_PROJ)
    nt = l // tm
    row = lambda shape: pl.BlockSpec(shape, lambda i, j: (i, j, 0))
    vec = pl.BlockSpec((1, 1, d), lambda i, j: (i, 0, 0))
    in_specs = [row((1, tm, d)), pl.BlockSpec((1, d), lambda i, j: (0, 0)), vec, vec,
                pl.BlockSpec((d, D_IN), lambda i, j: (0, 0))]
    args = [x, g.reshape(1, d), sh, sc, w_in]
    if rope:
        in_specs += [pl.BlockSpec((tm, LANE), lambda i, j: (j, 0))] * 6
        args += list(tables)
    sds = jax.ShapeDtypeStruct
    out_shape = [sds((b, l, 256), BF16), sds((b, l, 128), BF16), sds((b, l, 128), BF16),
                 sds((b, l, 256), BF16), sds((b, l, 256), BF16), sds((b, l, 256), BF16),
                 sds((b, l, 256), F32),
                 sds((b, 256, l), BF16), sds((b, l, 256), BF16),
                 sds((b, l // DIFF_TK, N_HEADS, DIFF_VROWS, DIFF_TK), BF16),
                 sds((b, l, 256), F32), sds((b, l, 256), F32)]
    out_specs = [row((1, tm, 256)), row((1, tm, 128)), row((1, tm, 128)),
                 row((1, tm, 256)), row((1, tm, 256)), row((1, tm, 256)), row((1, tm, 256)),
                 pl.BlockSpec((1, 256, tm), lambda i, j: (i, 0, j)), row((1, tm, 256)),
                 pl.BlockSpec((1, tm // DIFF_TK, N_HEADS, DIFF_VROWS, DIFF_TK), lambda i, j: (i, j, 0, 0, 0)),
                 row((1, tm, 256)), row((1, tm, 256))]
    return pl.pallas_call(
        functools.partial(_inproj_kernel, rope=rope, tm=tm),
        out_shape=out_shape, grid=(b, nt), in_specs=in_specs, out_specs=out_specs,
        compiler_params=_params("arbitrary", "arbitrary"),
        name="in_projection_rope" if rope else "in_projection_ctx",
    )(*args)


def _swa_kernel(*refs, window, tq, n):
    if window:
        sink_ref, q_ref, k_ref, v_ref, kc_ref, vc_ref, o_ref, s_scr = refs
    else:
        sink_ref, q_ref, kc_ref, vc_ref, o_ref, s_scr = refs
    kc = kc_ref[0]
    vc = vc_ref[0]
    m = kc.shape[0]
    lane = lax.broadcasted_iota(jnp.int32, (1, LANE), 1)
    halves = [(lane >= half * HEAD_DIM) & (lane < (half + 1) * HEAD_DIM) for half in range(2)]

    def attend(q, kcat, vcat, valid):
        for t in range(2):
            qt = q[:, t * LANE:(t + 1) * LANE]
            for half in range(2):
                qm = jnp.where(halves[half], qt, jnp.zeros_like(qt))
                s = _dot_nt(qm, kcat)
                if valid is not None:
                    s = jnp.where(valid, s, -jnp.inf)
                s_scr[2 * t + half] = s
        out_tiles = []
        for t in range(2):
            out_t = None
            for half in range(2):
                sink = sink_ref[2 * t + half]
                s = s_scr[2 * t + half]
                mx = jnp.maximum(jnp.max(s, axis=1, keepdims=True), sink)
                e = jnp.exp(s - mx)
                den = jnp.sum(e, axis=1, keepdims=True) + jnp.exp(sink - mx)
                o = _dot(e.astype(BF16), vcat) / den
                out_t = o if out_t is None else jnp.where(halves[half], o, out_t)
            out_tiles.append(out_t)
        return jnp.concatenate(out_tiles, axis=1)

    if not window:
        o_ref[0] = attend(q_ref[0], kc, vc, None)
        return
    w = tq + 2 * WINDOW
    t0 = pl.program_id(1) * tq
    start = pl.multiple_of(jnp.clip(t0 - WINDOW, 0, n - w), WINDOW)
    kcat = jnp.concatenate([k_ref[0, pl.ds(start, w), :], kc], axis=0)
    vcat = jnp.concatenate([v_ref[0, pl.ds(start, w), :], vc], axis=0)
    qpos = t0 + lax.broadcasted_iota(jnp.int32, (tq, w + m), 0)
    col = lax.broadcasted_iota(jnp.int32, (tq, w + m), 1)
    valid = (jnp.abs(qpos - (start + col)) <= WINDOW) | (col >= w)
    o_ref[0] = attend(q_ref[0], kcat, vcat, valid)


def _swa_attention(sink_perm, q, k, v, kc, vc):
    b, n, _ = q.shape
    m = kc.shape[1]
    tq = 256
    smem = pl.BlockSpec(memory_space=pltpu.SMEM)
    ctx = pl.BlockSpec((1, m, 128), lambda i, j: (i, 0, 0))
    in_specs = [smem, pl.BlockSpec((1, tq, 256), lambda i, j: (i, j, 0)),
                pl.BlockSpec((1, n, 128), lambda i, j: (i, 0, 0)),
                pl.BlockSpec((1, n, 128), lambda i, j: (i, 0, 0)), ctx, ctx]
    return pl.pallas_call(
        functools.partial(_swa_kernel, window=True, tq=tq, n=n),
        out_shape=jax.ShapeDtypeStruct((b, n, 256), F32),
        grid=(b, n // tq), in_specs=in_specs,
        out_specs=pl.BlockSpec((1, tq, 256), lambda i, j: (i, j, 0)),
        scratch_shapes=[pltpu.VMEM((N_HEADS, tq, tq + 2 * WINDOW + m), F32)],
        compiler_params=_params("arbitrary", "arbitrary"),
        name="swa_attention",
    )(sink_perm, q, k, v, kc, vc)


def _ctx_attention(sink_perm, qc, kc, vc):
    b, m, _ = qc.shape
    smem = pl.BlockSpec(memory_space=pltpu.SMEM)
    ctx = pl.BlockSpec((1, m, 128), lambda i: (i, 0, 0))
    return pl.pallas_call(
        functools.partial(_swa_kernel, window=False, tq=m, n=m),
        out_shape=jax.ShapeDtypeStruct((b, m, 256), F32),
        grid=(b,), in_specs=[smem, pl.BlockSpec((1, m, 256), lambda i: (i, 0, 0)), ctx, ctx],
        out_specs=pl.BlockSpec((1, m, 256), lambda i: (i, 0, 0)),
        scratch_shapes=[pltpu.VMEM((N_HEADS, m, m), F32)],
        compiler_params=_params("arbitrary"),
        name="ctx_attention",
    )(sink_perm, qc, kc, vc)


RET_CHUNK = 128


def _ret_kernel(dl_lane_ref, dl_head_ref, q_ref, k_ref, v_ref, s0_ref, y_ref, sfin_ref,
                s_scr, dm_scr, xi_scr, zeta_scr, *, ns, group):
    c = RET_CHUNK
    nb = q_ref.shape[0]
    d = pl.program_id(0)
    si = pl.program_id(1)
    fwd = d == 0
    lane_head = lax.broadcasted_iota(jnp.int32, (1, GROUP), 1) // HEAD_DIM

    @pl.when(si == 0)
    def _():
        s_scr[...] = s0_ref[:, 0]
        lgl = -_softplus(-dl_lane_ref[0])
        row = lax.broadcasted_iota(jnp.int32, (c, 1), 0)
        xi_scr[...] = jnp.exp(lgl * jnp.where(fwd, row + 1, c - row).astype(F32))
        zeta_scr[...] = jnp.exp(lgl * jnp.where(fwd, c - 1 - row, row).astype(F32))
        ni = lax.broadcasted_iota(jnp.int32, (c, c), 0)
        mi = lax.broadcasted_iota(jnp.int32, (c, c), 1)
        dd = jnp.where(fwd, ni - mi, mi - ni)
        ddf = jnp.maximum(dd, 0).astype(F32)
        for h in range(N_HEADS):
            lgh = -_softplus(-dl_head_ref[0, h][0:1, :])
            dm_scr[h] = jnp.where(dd >= 0, jnp.exp(lgh * ddf), 0.0)

    dec = jnp.exp(-_softplus(-dl_lane_ref[0]) * float(c))
    rh = lax.broadcasted_iota(jnp.int32, (GROUP, GROUP), 0) // HEAD_DIM
    ch = lax.broadcasted_iota(jnp.int32, (GROUP, GROUP), 1) // HEAD_DIM
    same_head = rh == ch

    def chunk(g, carry):
        r0 = pl.multiple_of(jnp.where(fwd, g, group - 1 - g) * c, c)
        for bi in range(nb):
            q = q_ref[bi, pl.ds(r0, c), :]
            k = k_ref[bi, pl.ds(r0, c), :]
            v = v_ref[bi, pl.ds(r0, c), :]
            cross = _dot(q, s_scr[bi].astype(BF16)) * xi_scr[...]
            inner = jnp.zeros((c, GROUP), F32)
            for h in range(N_HEADS):
                qh = jnp.where(lane_head == h, q, jnp.zeros_like(q))
                s = _dot_nt(qh, k) * dm_scr[h]
                ih = _dot(s.astype(BF16), v)
                inner = jnp.where(lane_head == h, ih, inner)
            y_ref[bi, 0, pl.ds(r0, c), :] = inner + cross
            kz = (k.astype(F32) * zeta_scr[...]).T.astype(BF16)
            u = _dot(kz, v)
            s_scr[bi] = jnp.where(same_head, dec * s_scr[bi] + u, 0.0)
        return carry

    lax.fori_loop(0, group, chunk, 0, unroll=2)

    @pl.when(si == ns - 1)
    def _():
        sfin_ref[:, 0] = s_scr[...]


def _retention(dl_lane, dl_head, q, k, v, s0):
    b, l, _ = q.shape
    c = RET_CHUNK
    group = 8 if l % (8 * c) == 0 else 2
    ns = l // (group * c)
    sidx = lambda d, j: (0, jnp.where(d == 0, j, ns - 1 - j), 0)
    tok = pl.BlockSpec((b, group * c, 256), sidx)
    st = pl.BlockSpec((b, 1, 256, 256), lambda d, j: (0, d, 0, 0))
    return pl.pallas_call(
        functools.partial(_ret_kernel, ns=ns, group=group),
        out_shape=[jax.ShapeDtypeStruct((b, 2, l, 256), F32),
                   jax.ShapeDtypeStruct((b, 2, 256, 256), F32)],
        grid=(2, ns),
        in_specs=[pl.BlockSpec((1, 1, 256), lambda d, j: (d, 0, 0)),
                  pl.BlockSpec((1, N_HEADS, 8, c), lambda d, j: (d, 0, 0, 0)),
                  tok, tok, tok, st],
        out_specs=[pl.BlockSpec((b, 1, group * c, 256),
                                lambda d, j: (0, d, jnp.where(d == 0, j, ns - 1 - j), 0)),
                   st],
        scratch_shapes=[pltpu.VMEM((b, GROUP, GROUP), F32),
                        pltpu.VMEM((N_HEADS, c, c), F32),
                        pltpu.VMEM((c, GROUP), F32),
                        pltpu.VMEM((c, GROUP), F32)],
        compiler_params=_params("arbitrary", "arbitrary"),
        name="retention",
    )(dl_lane, dl_head, q, k, v, s0)


def _inner_tiles(tiles):
    for g in (8, 7, 6, 5, 4, 3, 2):
        if tiles % g == 0:
            return g
    return 1


def _diff_kernel(qt_ref, k_ref, vt_ref, o_ref, s_scr, *, tiles, nq):
    h = pl.program_id(1)
    inner = _inner_tiles(tiles)
    outer = tiles // inner
    total = nq * outer
    tq = DIFF_TQ
    rowi = lax.broadcasted_iota(jnp.int32, (LANE, 1), 0)
    hoff = (h % 2) * HEAD_DIM
    rowmask = [(rowi >= hoff + mp * DIFF_DIM) & (rowi < hoff + (mp + 1) * DIFF_DIM) for mp in range(2)]

    def step(mp, g_q, g_e, slot, m_e):
        mx = pv = None
        if g_q is not None:
            q0 = pl.multiple_of((g_q // outer) * tq, tq)
            qt = qt_ref[0, :, pl.ds(q0, tq)]
            qpad = jnp.where(rowmask[mp], qt, jnp.zeros_like(qt))
            kc = (g_q % outer) * inner
        if g_e is not None:
            vc = (g_e % outer) * inner
        for t in range(inner):
            sl = slice(t * DIFF_TK, (t + 1) * DIFF_TK)
            if g_q is not None:
                base = pl.multiple_of((kc + t) * DIFF_TK, DIFF_TK)
                s = _dot(k_ref[0, pl.ds(base, DIFF_TK), :], qpad)
                s_scr[mp, 1 - slot, sl, :] = s
                cm = jnp.max(s, axis=0, keepdims=True)
                mx = cm if mx is None else jnp.maximum(mx, cm)
            if g_e is not None:
                p = jnp.exp2(s_scr[mp, slot, sl, :] - m_e).astype(BF16)
                d = _dot(vt_ref[0, vc + t, 0], p)
                pv = d if pv is None else pv + d
        return mx, pv

    def advance(mp, g_q, g_e, slot, state):
        m_e, alpha, acc = state
        mx, pv = step(mp, g_q, g_e, slot, m_e)
        acc = alpha * acc + pv
        o0 = pl.multiple_of((g_e // outer) * tq, tq)
        o_ref[0, 0, mp, :, pl.ds(o0, tq)] = acc
        if g_q is None:
            return m_e, alpha, acc
        m_base = jnp.where(g_q % outer == 0, -jnp.inf, m_e)
        m_next = jnp.maximum(m_base, mx)
        return m_next, jnp.exp2(m_base - m_next), acc

    zeros = jnp.zeros((DIFF_VROWS, tq), F32)
    states = []
    for mp in range(2):
        m_first, _ = step(mp, 0, None, 1, None)
        states.append((m_first, jnp.zeros((1, tq), F32), zeros))
    states = tuple(states)

    def body(i, carry):
        carry = tuple(advance(mp, 2 * i + 1, 2 * i, 0, carry[mp]) for mp in range(2))
        return tuple(advance(mp, 2 * i + 2, 2 * i + 1, 1, carry[mp]) for mp in range(2))

    last = total - 1
    states = lax.fori_loop(0, last // 2, body, states, unroll=min(8, max(last // 2, 1)))
    if last % 2 == 1:
        states = tuple(advance(mp, last, last - 1, 0, states[mp]) for mp in range(2))
    for mp in range(2):
        advance(mp, None, last, last % 2, states[mp])


def _diff_attention(qt, k_all, vt_all):
    b, _, lq = qt.shape
    lk = k_all.shape[1]
    tiles = lk // DIFF_TK
    nq = lq // DIFF_TQ
    return pl.pallas_call(
        functools.partial(_diff_kernel, tiles=tiles, nq=nq),
        out_shape=jax.ShapeDtypeStruct((b, N_HEADS, 2, DIFF_VROWS, lq), F32),
        grid=(b, N_HEADS),
        in_specs=[pl.BlockSpec((1, LANE, lq), lambda i, h: (i, h // 2, 0)),
                  pl.BlockSpec((1, lk, LANE), lambda i, h: (i, 0, h // 2)),
                  pl.BlockSpec((1, tiles, 1, DIFF_VROWS, DIFF_TK), lambda i, h: (i, 0, h, 0, 0))],
        out_specs=pl.BlockSpec((1, 1, 2, DIFF_VROWS, lq), lambda i, h: (i, h, 0, 0, 0)),
        scratch_shapes=[pltpu.VMEM((2, 2, _inner_tiles(tiles) * DIFF_TK, DIFF_TQ), F32)],
        compiler_params=_params("arbitrary", "arbitrary"),
        name="diff_attention",
    )(qt, k_all, vt_all)


def _lru_kernel(x_ref, xp_ref, xn_ref, cw_ref, cb_ref, wa_ref, ba_ref, wx_ref, bx_ref, lam_ref,
                h0_ref, h_ref, carry, *, reverse, tm, nt):
    j = pl.program_id(1)
    tix = (nt - 1 - j) if reverse else j

    @pl.when(j == 0)
    def _():
        carry[...] = jnp.broadcast_to(h0_ref[0], carry.shape)

    x = x_ref[0]
    row = lax.broadcasted_iota(jnp.int32, (tm, 1), 0)
    prev = jnp.where(tix > 0, xp_ref[0][7:8, :], 0.0)
    nxt0 = jnp.where(tix < nt - 1, xn_ref[0][0:1, :], 0.0)
    nxt1 = jnp.where(tix < nt - 1, xn_ref[0][1:2, :], 0.0)
    xm1 = jnp.where(row == 0, prev, pltpu.roll(x, 1, axis=0))
    xp1 = jnp.where(row == tm - 1, nxt0, pltpu.roll(x, tm - 1, axis=0))
    xp2 = jnp.where(row == tm - 1, nxt1, jnp.where(row == tm - 2, nxt0, pltpu.roll(x, tm - 2, axis=0)))
    cw = cw_ref[...]
    u = cw[0:1] * xm1 + cw[1:2] * x + cw[2:3] * xp1 + cw[3:4] * xp2 + cb_ref[...]
    ub = u.astype(BF16)
    r = jax.nn.sigmoid(_dot(ub, wa_ref[0]) + ba_ref[0])
    gi = jax.nn.sigmoid(_dot(ub, wx_ref[0]) + bx_ref[0])
    log_a = -LRU_C * r * _softplus(-lam_ref[0])
    a = jnp.exp(log_a)
    th = jnp.tanh(log_a)
    bv = jnp.sqrt(-2.0 * th / (1.0 - th)) * (gi * u)
    k = 1
    while k < tm:
        if k < 8:
            keep = (row < tm - k) if reverse else (row >= k)
            shift = (tm - k) if reverse else k
            a_sh = jnp.where(keep, pltpu.roll(a, shift, axis=0), 1.0)
            b_sh = jnp.where(keep, pltpu.roll(bv, shift, axis=0), 0.0)
            bv = a * b_sh + bv
            a = a * a_sh
        elif reverse:
            head_b = a[:tm - k] * bv[k:] + bv[:tm - k]
            head_a = a[:tm - k] * a[k:]
            bv = jnp.concatenate([head_b, bv[tm - k:]], axis=0)
            a = jnp.concatenate([head_a, a[tm - k:]], axis=0)
        else:
            tail_b = a[k:] * bv[:tm - k] + bv[k:]
            tail_a = a[k:] * a[:tm - k]
            bv = jnp.concatenate([bv[:k], tail_b], axis=0)
            a = jnp.concatenate([a[:k], tail_a], axis=0)
        k *= 2
    hcur = a * carry[0:1, :] + bv
    h_ref[0] = hcur
    last = hcur[0:1, :] if reverse else hcur[tm - 1:tm, :]
    carry[...] = jnp.broadcast_to(last, carry.shape)


def _lru_scan(lx, conv_w, conv_b, wa_bd, ba, wx_bd, bx, lam, h0, reverse):
    b, l, w = lx.shape
    tm = _row_tile(l, ROWS_PROJ)
    nt = l // tm
    tb = tm // 8
    tidx = (lambda j: nt - 1 - j) if reverse else (lambda j: j)
    vec = pl.BlockSpec((1, w), lambda i, j: (0, 0))
    mat = pl.BlockSpec((1, w, w), lambda i, j: (0, 0, 0))
    return pl.pallas_call(
        functools.partial(_lru_kernel, reverse=reverse, tm=tm, nt=nt),
        out_shape=jax.ShapeDtypeStruct((b, l, w), F32),
        grid=(b, nt),
        in_specs=[pl.BlockSpec((1, tm, w), lambda i, j: (i, tidx(j), 0)),
                  pl.BlockSpec((1, 8, w), lambda i, j: (i, jnp.maximum(tidx(j) * tb - 1, 0), 0)),
                  pl.BlockSpec((1, 8, w), lambda i, j: (i, jnp.minimum((tidx(j) + 1) * tb, l // 8 - 1), 0)),
                  pl.BlockSpec((4, w), lambda i, j: (0, 0)), vec,
                  mat, vec, mat, vec, vec,
                  pl.BlockSpec((1, 1, w), lambda i, j: (i, 0, 0))],
        out_specs=pl.BlockSpec((1, tm, w), lambda i, j: (i, tidx(j), 0)),
        scratch_shapes=[pltpu.VMEM((8, w), F32)],
        compiler_params=_params("arbitrary", "arbitrary"),
        name="lru_scan_rev" if reverse else "lru_scan_fwd",
    )(lx, lx, lx, conv_w, conv_b.reshape(1, w), wa_bd.reshape(1, w, w), ba.reshape(1, w),
      wx_bd.reshape(1, w, w), bx.reshape(1, w), lam.reshape(1, w), h0)


def _outproj_kernel(x_ref, gt_ref, oa_ref, ry_ref, rg_ref, rgn_ref, lam_ref, dgn_ref, draw_ref, hf_ref, hb_ref,
                    ly_ref, bd_ref, w_ref, o_ref, *, lam_init):
    oa = oa_ref[0]
    y = ry_ref[0, 0] + ry_ref[0, 1]
    y2 = y * y
    hi = y2.astype(BF16)
    lo = (y2 - hi.astype(F32)).astype(BF16)
    ms = _dot(hi, bd_ref[...]) + _dot(lo, bd_ref[...])
    ob = y * lax.rsqrt(ms + NORM_EPS) * rgn_ref[...] * _silu(rg_ref[0])
    lv = lam_ref[...]
    lam = (jnp.exp(jnp.sum(lv[0:1] * lv[1:2], axis=1, keepdims=True))
           - jnp.exp(jnp.sum(lv[2:3] * lv[3:4], axis=1, keepdims=True)) + lam_init)
    heads = []
    for hd in range(N_HEADS):
        a0 = draw_ref[0, hd, 0]
        a1 = draw_ref[0, hd, 1]
        oh = (a0[0:HEAD_DIM] / a0[HEAD_DIM:HEAD_DIM + 1]
              - lam * (a1[0:HEAD_DIM] / a1[HEAD_DIM:HEAD_DIM + 1]))
        msd = jnp.mean(oh * oh, axis=0, keepdims=True)
        heads.append(oh * lax.rsqrt(msd + NORM_EPS) * (dgn_ref[hd] * (1.0 - lam_init)))
    od = jnp.concatenate(heads, axis=0).T
    ol = (hf_ref[0] + hb_ref[0]) * jax.nn.gelu(ly_ref[0])
    acc = _dot(oa.astype(BF16), w_ref[0:256, :])
    acc = acc + _dot(ob.astype(BF16), w_ref[256:512, :])
    acc = acc + _dot(od.astype(BF16), w_ref[512:768, :])
    acc = acc + _dot(ol.astype(BF16), w_ref[768:1024, :])
    o_ref[0] = x_ref[0] + gt_ref[0] * acc


def _out_projection(x, gt, oa, ry, rg, ret_gn, lam_vecs, diff_gn, lam_init, draw, hf, hb, ly, w_out):
    b, l, d = x.shape
    tm = _row_tile(l, ROWS_PROJ)
    row = lambda wdt: pl.BlockSpec((1, tm, wdt), lambda i, j: (i, j, 0))
    hd = jnp.arange(GROUP) // HEAD_DIM
    bd = jnp.where(hd[:, None] == hd[None, :], 1.0 / HEAD_DIM, 0.0).astype(BF16)
    return pl.pallas_call(
        functools.partial(_outproj_kernel, lam_init=lam_init),
        out_shape=jax.ShapeDtypeStruct((b, l, d), F32),
        grid=(b, l // tm),
        in_specs=[row(d), pl.BlockSpec((1, 1, d), lambda i, j: (i, 0, 0)), row(256),
                  pl.BlockSpec((1, 2, tm, 256), lambda i, j: (i, 0, j, 0)), row(256),
                  pl.BlockSpec((1, 256), lambda i, j: (0, 0)),
                  pl.BlockSpec((4, DIFF_DIM), lambda i, j: (0, 0)),
                  pl.BlockSpec((N_HEADS, HEAD_DIM, 1), lambda i, j: (0, 0, 0)),
                  pl.BlockSpec((1, N_HEADS, 2, DIFF_VROWS, tm), lambda i, j: (i, 0, 0, 0, j)),
                  row(256), row(256), row(256),
                  pl.BlockSpec((256, 256), lambda i, j: (0, 0)),
                  pl.BlockSpec((d, d), lambda i, j: (0, 0))],
        out_specs=row(d),
        compiler_params=_params("arbitrary", "arbitrary"),
        name="out_projection",
    )(x, gt, oa, ry, rg, ret_gn.reshape(1, 256), lam_vecs, diff_gn.reshape(N_HEADS, HEAD_DIM, 1), draw,
      hf, hb, ly, bd, w_out)


def _norm_modulate(x_ref, g_ref, sh_ref, sc_ref):
    x = x_ref[0]
    ms = jnp.mean(x * x, axis=-1, keepdims=True)
    y = x * lax.rsqrt(ms + NORM_EPS) * g_ref[...]
    return y * (1.0 + sc_ref[0]) + sh_ref[0]


def _ffn_kernel(x_ref, g_ref, sh_ref, sc_ref, gt_ref, w1_ref, w3_ref, w2_ref, o_ref, hb_scr, acc_scr,
                *, n_f):
    f = pl.program_id(2)

    @pl.when(f == 0)
    def _():
        hb_scr[...] = _norm_modulate(x_ref, g_ref, sh_ref, sc_ref).astype(BF16)
        acc_scr[...] = jnp.zeros_like(acc_scr)

    hb = hb_scr[...]
    a = _silu(_dot(hb, w1_ref[...])) * _dot(hb, w3_ref[...])
    acc_scr[...] += _dot(a.astype(BF16), w2_ref[...])

    @pl.when(f == n_f - 1)
    def _():
        o_ref[0] = x_ref[0] + gt_ref[0] * acc_scr[...]


def _ffn(x, g, sh, sc, gt, w1, w3, w2):
    b, l, d = x.shape
    ff = w1.shape[1]
    tm = _row_tile(l, ROWS_FFN)
    n_f = 2
    fc = ff // n_f
    vec = pl.BlockSpec((1, 1, d), lambda i, j, f: (i, 0, 0))
    return pl.pallas_call(
        functools.partial(_ffn_kernel, n_f=n_f),
        out_shape=jax.ShapeDtypeStruct((b, l, d), F32),
        grid=(b, l // tm, n_f),
        in_specs=[pl.BlockSpec((1, tm, d), lambda i, j, f: (i, j, 0)),
                  pl.BlockSpec((1, d), lambda i, j, f: (0, 0)), vec, vec, vec,
                  pl.BlockSpec((d, fc), lambda i, j, f: (0, f)),
                  pl.BlockSpec((d, fc), lambda i, j, f: (0, f)),
                  pl.BlockSpec((fc, d), lambda i, j, f: (f, 0))],
        out_specs=pl.BlockSpec((1, tm, d), lambda i, j, f: (i, j, 0)),
        scratch_shapes=[pltpu.VMEM((tm, d), BF16), pltpu.VMEM((tm, d), F32)],
        compiler_params=_params("arbitrary", "arbitrary", "arbitrary"),
        name="dense_ffn",
    )(x, g.reshape(1, d), sh, sc, gt, w1, w3, w2)


MOE_RT = 128


def _moe_kernel(x_ref, g_ref, sh_ref, sc_ref, gt_ref, fg_ref, rt_ref, w1_ref, w3_ref, w2_ref, o_ref,
                hb_scr, gate_scr, pos_scr, gatet_scr, post_scr, x_scr, y_scr, gs_scr, cnt_smem,
                *, n_e, n_f, tm, final_norm):
    e = pl.program_id(2)
    f = pl.program_id(3)
    rt2 = 2 * MOE_RT

    @pl.when((e == 0) & (f == 0))
    def _():
        h = _norm_modulate(x_ref, g_ref, sh_ref, sc_ref)
        hb = h.astype(BF16)
        hb_scr[...] = hb
        o_ref[0] = jnp.zeros(o_ref.shape[1:], F32)
        rt = rt_ref[...]
        rhi = rt.astype(BF16)
        rlo = (rt - rhi.astype(F32)).astype(BF16)
        hlo = (h - hb.astype(F32)).astype(BF16)
        logits = _dot(hb, rhi) + _dot(hb, rlo) + _dot(hlo, rhi)
        lane = lax.broadcasted_iota(jnp.int32, logits.shape, 1)
        logits = jnp.where(lane < N_EXPERTS, logits, -jnp.inf)
        m1 = jnp.max(logits, axis=1, keepdims=True)
        i1 = jnp.min(jnp.where(logits == m1, lane, LANE), axis=1, keepdims=True)
        rest = jnp.where(lane == i1, -jnp.inf, logits)
        m2 = jnp.max(rest, axis=1, keepdims=True)
        i2 = jnp.min(jnp.where(rest == m2, lane, LANE), axis=1, keepdims=True)
        e2 = jnp.exp(m2 - m1)
        den = 1.0 + e2
        gate = jnp.where(lane == i1, 1.0 / den, 0.0) + jnp.where(lane == i2, e2 / den, 0.0)
        gate_scr[...] = gate
        sel = jnp.where(gate > 0.0, 1.0, 0.0)
        selb = sel.astype(BF16)
        strip = 256
        for r0 in range(0, tm, strip):
            tok_r = r0 + lax.broadcasted_iota(jnp.int32, (strip, tm), 0)
            tok_c = lax.broadcasted_iota(jnp.int32, (strip, tm), 1)
            earlier = jnp.where(tok_c < tok_r, 1.0, 0.0).astype(BF16)
            pos_scr[r0:r0 + strip, :] = _dot(earlier, selb)
        post_scr[...] = pos_scr[...].T
        gatet_scr[...] = gate.T
        counts = jnp.sum(sel, axis=0, keepdims=True)
        for ee in range(n_e):
            cnt_smem[ee] = jnp.sum(jnp.where(lane[0:1] == ee, counts, 0.0)).astype(jnp.int32)

    n_tiles = (cnt_smem[e] + (MOE_RT - 1)) // MOE_RT
    n_pairs = (n_tiles + 1) // 2

    @pl.when(f == 0)
    def _():
        sub = lax.broadcasted_iota(jnp.int32, (8, 1), 0)
        pos_e = jnp.sum(jnp.where(sub == e, post_scr[0:8, :], 0.0), axis=0, keepdims=True)
        gate_e = jnp.sum(jnp.where(sub == e, gatet_scr[0:8, :], 0.0), axis=0, keepdims=True)

        def gather(jj, carry):
            r0 = pl.multiple_of(jj * rt2, rt2)
            slot = (r0 + lax.broadcasted_iota(jnp.int32, (rt2, 1), 0)).astype(F32)
            hit = (pos_e == slot) & (gate_e > 0.0)
            onehot = jnp.where(hit, 1.0, 0.0).astype(BF16)
            x_scr[pl.ds(r0, rt2), :] = _dot(onehot, hb_scr[...]).astype(BF16)
            gs = jnp.sum(jnp.where(hit, gate_e, 0.0), axis=1, keepdims=True)
            gs_scr[pl.ds(r0, rt2), :] = jnp.broadcast_to(gs, (rt2, LANE))
            y_scr[pl.ds(r0, rt2), :] = jnp.zeros((rt2, y_scr.shape[1]), F32)
            return carry

        lax.fori_loop(0, n_pairs, gather, 0)

    def expert(r0, rows):
        xj = x_scr[pl.ds(r0, rows), :]
        a = _silu(_dot(xj, w1_ref[0])) * _dot(xj, w3_ref[0])
        y_scr[pl.ds(r0, rows), :] += _dot(a.astype(BF16), w2_ref[0])

    def expert_pair(jj, carry):
        expert(pl.multiple_of(jj * rt2, rt2), rt2)
        return carry

    lax.fori_loop(0, n_tiles // 2, expert_pair, 0)

    @pl.when(n_tiles % 2 == 1)
    def _():
        expert(pl.multiple_of((n_tiles - 1) * MOE_RT, MOE_RT), MOE_RT)

    @pl.when(f == n_f - 1)
    def _():
        lane = lax.broadcasted_iota(jnp.int32, (1, LANE), 1)
        pos_col = jnp.sum(jnp.where(lane == e, pos_scr[...], 0.0), axis=1, keepdims=True)
        gate_col = jnp.sum(jnp.where(lane == e, gate_scr[...], 0.0), axis=1, keepdims=True)

        def scatter(jj, carry):
            r0 = pl.multiple_of(jj * rt2, rt2)
            slot = (r0 + lax.broadcasted_iota(jnp.int32, (1, rt2), 1)).astype(F32)
            onehot = jnp.where((pos_col == slot) & (gate_col > 0.0), 1.0, 0.0).astype(BF16)
            ys = (y_scr[pl.ds(r0, rt2), :] * gs_scr[pl.ds(r0, rt2), 0:1]).astype(BF16)
            o_ref[0] += _dot(onehot, ys)
            return carry

        lax.fori_loop(0, n_pairs, scatter, 0)

    @pl.when((e == n_e - 1) & (f == n_f - 1))
    def _():
        out = x_ref[0] + gt_ref[0] * o_ref[0]
        if final_norm:
            ms = jnp.mean(out * out, axis=-1, keepdims=True)
            out = out * lax.rsqrt(ms + NORM_EPS) * fg_ref[...]
        o_ref[0] = out


def _moe_ffn(x, g, sh, sc, gt, final_g, w1, w3, w2, router, final_norm):
    b, l, d = x.shape
    n_e, _, ff = w1.shape
    tm = _row_tile(l, ROWS_FFN)
    n_f = 2
    fc = ff // n_f
    vec = pl.BlockSpec((1, 1, d), lambda i, j, e, f: (i, 0, 0))
    return pl.pallas_call(
        functools.partial(_moe_kernel, n_e=n_e, n_f=n_f, tm=tm, final_norm=final_norm),
        out_shape=jax.ShapeDtypeStruct((b, l, d), F32),
        grid=(b, l // tm, n_e, n_f),
        in_specs=[pl.BlockSpec((1, tm, d), lambda i, j, e, f: (i, j, 0)),
                  pl.BlockSpec((1, d), lambda i, j, e, f: (0, 0)), vec, vec, vec,
                  pl.BlockSpec((1, d), lambda i, j, e, f: (0, 0)),
                  pl.BlockSpec((d, LANE), lambda i, j, e, f: (0, 0)),
                  pl.BlockSpec((1, d, fc), lambda i, j, e, f: (e, 0, f)),
                  pl.BlockSpec((1, d, fc), lambda i, j, e, f: (e, 0, f)),
                  pl.BlockSpec((1, fc, d), lambda i, j, e, f: (e, f, 0))],
        out_specs=pl.BlockSpec((1, tm, d), lambda i, j, e, f: (i, j, 0)),
        scratch_shapes=[pltpu.VMEM((tm, d), BF16),
                        pltpu.VMEM((tm, LANE), F32),
                        pltpu.VMEM((tm, LANE), F32),
                        pltpu.VMEM((LANE, tm), F32),
                        pltpu.VMEM((LANE, tm), F32),
                        pltpu.VMEM((tm, d), BF16),
                        pltpu.VMEM((tm, d), F32),
                        pltpu.VMEM((tm, LANE), F32),
                        pltpu.SMEM((n_e,), jnp.int32)],
        compiler_params=_params("arbitrary", "arbitrary", "arbitrary", "arbitrary"),
        name="moe_ffn",
    )(x, g.reshape(1, d), sh, sc, gt, final_g.reshape(1, d), router, w1, w3, w2)


def _rope_tables(n):
    rows = n // GRID_W
    row = jnp.repeat(jnp.arange(rows, dtype=F32), GRID_W)
    col = jnp.tile(jnp.arange(GRID_W, dtype=F32), rows)

    def axial(dim):
        nf = dim // 4
        inv = ROPE_BASE ** (-jnp.arange(nf, dtype=F32) / nf)
        return jnp.concatenate([row[:, None] * inv, col[:, None] * inv], axis=-1)

    def line(dim):
        nf = dim // 2
        inv = ROPE_BASE ** (-jnp.arange(nf, dtype=F32) / nf)
        return jnp.arange(n, dtype=F32)[:, None] * inv

    def expand(ang, dim):
        c = jnp.cos(ang)
        s = jnp.sin(ang)
        c = jnp.tile(jnp.concatenate([c, c], axis=-1), (1, LANE // dim))
        s = jnp.tile(jnp.concatenate([-s, s], axis=-1), (1, LANE // dim))
        return c, s

    ca, sa = expand(axial(HEAD_DIM), HEAD_DIM)
    cr, sr = expand(line(HEAD_DIM), HEAD_DIM)
    cd, sd = expand(axial(DIFF_DIM), DIFF_DIM)
    return ca, sa, cr, sr, cd, sd


_ATTN_HEAD_ORDER = (0, 2, 1, 3)


def _attn_perm():
    return jnp.concatenate([jnp.arange(HEAD_DIM) + HEAD_DIM * h for h in _ATTN_HEAD_ORDER])


def _block_diag(w):
    nb, blk, _ = w.shape
    out = jnp.zeros((nb * blk, nb * blk), w.dtype)
    for i in range(nb):
        out = out.at[i * blk:(i + 1) * blk, i * blk:(i + 1) * blk].set(w[i])
    return out


def kernel(x, c, ctx, c_ctx, w_ada, b_ada, g_norm1, g_norm2, w_in, w_out, attn_sink, ret_decay_logit,
           ret_gn, diff_lambda, diff_gn, conv_w, conv_b, lru_wa, lru_ba, lru_wx, lru_bx, lru_lambda,
           ffn_w1, ffn_w3, ffn_w2, moe_router, moe_w1, moe_w3, moe_w2, final_norm):
    b, n, d = x.shape
    m = ctx.shape[1]
    tables = _rope_tables(n)
    perm = _attn_perm()

    cc = jnp.concatenate([c, c_ctx[None, :], jnp.zeros((8 - b - 1, d), F32)], axis=0)
    mods = _mod_vectors(cc, w_ada, b_ada)

    xc = ctx
    for layer in range(DEPTH):
        need_ctx = layer < DEPTH - 1
        lam_init = 0.8 - 0.6 * math.exp(-0.3 * layer)
        mod = mods[layer]
        sh1, sc1, gt1, sh2, sc2, gt2 = [mod[:b, None, i * d:(i + 1) * d] for i in range(6)]
        sh1c, sc1c, gt1c, sh2c, sc2c, gt2c = [
            jnp.broadcast_to(mod[b:b + 1, None, i * d:(i + 1) * d], (b, 1, d)) for i in range(6)]

        wi = w_in[layer]
        wi = jnp.concatenate([wi[:, :256][:, perm], wi[:, 256:]], axis=1).astype(BF16)
        wo = w_out[layer]
        wo = jnp.concatenate([wo[:256][perm], wo[256:]], axis=0).astype(BF16)
        sink_perm = attn_sink[layer][jnp.array(_ATTN_HEAD_ORDER)]

        (qa, ka, va, rq, rk, rv, rg, dqt, dk, dvt, lx, ly) = _in_projection(
            x, g_norm1[layer], sh1, sc1, wi, tables)
        (qac, kac, vac, rqc, rkc, rvc, rgc, dqtc, dkc, dvtc, lxc, lyc) = _in_projection(
            xc, g_norm1[layer], sh1c, sc1c, wi, None)

        oa = _swa_attention(sink_perm, qa, ka, va, kac, vac)
        dl = ret_decay_logit[layer]
        dl_lane = jnp.repeat(dl, HEAD_DIM, axis=1).reshape(2, 1, GROUP)
        dl_head = jnp.broadcast_to(dl[:, :, None, None], (2, N_HEADS, 8, RET_CHUNK))
        ryc, s_ctx = _retention(dl_lane, dl_head, rqc, rkc, rvc, jnp.zeros((b, 2, GROUP, GROUP), F32))
        ry, _ = _retention(dl_lane, dl_head, rq, rk, rv, s_ctx)
        k_all = jnp.concatenate([dk, dkc], axis=1)
        vt_all = jnp.concatenate([dvt, dvtc], axis=1)
        draw = _diff_attention(dqt, k_all, vt_all)
        assemble = functools.partial(_out_projection, ret_gn=ret_gn[layer], lam_vecs=diff_lambda[layer],
                                     diff_gn=diff_gn[layer], lam_init=lam_init, w_out=wo)
        lru = []
        lru_c = []
        for direction in range(2):
            prm = (conv_w[layer], conv_b[layer],
                   _block_diag(lru_wa[layer, direction]).astype(BF16), lru_ba[layer, direction],
                   _block_diag(lru_wx[layer, direction]).astype(BF16), lru_bx[layer, direction],
                   lru_lambda[layer, direction])
            rev = direction == 1
            hc = _lru_scan(lxc, *prm, jnp.zeros((b, 1, GROUP), F32), rev)
            h0 = hc[:, 0:1, :] if rev else hc[:, m - 1:m, :]
            lru.append(_lru_scan(lx, *prm, h0, rev))
            lru_c.append(hc)

        x = assemble(x, gt1, oa, ry, rg, draw=draw, hf=lru[0], hb=lru[1], ly=ly)
        j = layer // 2
        if layer % 2 == 0:
            ffn = functools.partial(_ffn, w1=ffn_w1[j].astype(BF16), w3=ffn_w3[j].astype(BF16),
                                    w2=ffn_w2[j].astype(BF16))
            x = ffn(x, g_norm2[layer], sh2, sc2, gt2)
        else:
            moe = functools.partial(_moe_ffn, final_g=final_norm, w1=moe_w1[j].astype(BF16),
                                    w3=moe_w3[j].astype(BF16), w2=moe_w2[j].astype(BF16),
                                    router=jnp.pad(moe_router[j], ((0, 0), (0, LANE - N_EXPERTS))))
            x = moe(x, g_norm2[layer], sh2, sc2, gt2, final_norm=layer == DEPTH - 1)
            ffn = functools.partial(moe, final_norm=False)

        if need_ctx:
            oac = _ctx_attention(sink_perm, qac, kac, vac)
            drawc = _diff_attention(dqtc, dkc, dvtc)
            xc = assemble(xc, gt1c, oac, ryc, rgc, draw=drawc, hf=lru_c[0], hb=lru_c[1], ly=lyc)
            xc = ffn(xc, g_norm2[layer], sh2c, sc2c, gt2c)
    assert DEPTH % 2 == 0, "the closing RMSNorm is fused into the last (routed) layer"
    return x
```

```python
import functools
import math

import jax
import jax.numpy as jnp
from jax import lax
from jax.experimental import pallas as pl
from jax.experimental.pallas import tpu as pltpu

F32 = jnp.float32
BF16 = jnp.bfloat16

D_MODEL = 1024
DEPTH = 2
GRID_W = 64
HEAD_DIM = 64
GROUP = D_MODEL // 4
N_HEADS = GROUP // HEAD_DIM
DIFF_DIM = HEAD_DIM // 2
WINDOW = 128
NORM_EPS = 1e-6
ROPE_BASE = 10000.0
LRU_BLOCKS = 4
LRU_C = 8.0
N_EXPERTS = 8
LANE = 128
LOG2E = 1.4426950408889634
DIFF_TQ = 256
DIFF_TK = 256
DIFF_VROWS = HEAD_DIM + 16

OFF_AQ, OFF_AK, OFF_AV = 0, 256, 384
OFF_RQ, OFF_RK, OFF_RV, OFF_RG = 512, 768, 1024, 1280
OFF_DQ, OFF_DK, OFF_DV = 1536, 1792, 2048
OFF_LX, OFF_LY = 2304, 2560
D_IN = 2816

VMEM_LIMIT = 56 * 1024 * 1024
ROWS_PROJ = 512
ROWS_FFN = 1024
ROWS_MIN = 256


def _row_tile(l, preferred):
    tm = preferred if l % preferred == 0 else ROWS_MIN
    assert l % tm == 0, (l, tm)
    return tm


def _params(*sem):
    return pltpu.CompilerParams(dimension_semantics=sem, vmem_limit_bytes=VMEM_LIMIT)


def _silu(x):
    return x * jax.nn.sigmoid(x)


def _softplus(x):
    return jnp.maximum(x, 0.0) + jnp.log(1.0 + jnp.exp(-jnp.abs(x)))


def _dot(a, b):
    return jnp.dot(a, b, preferred_element_type=F32)


def _dot_nt(a, b):
    return lax.dot_general(a, b, (((1,), (1,)), ((), ())), preferred_element_type=F32)


def _mod_kernel(c_ref, w_ref, b_ref, o_ref):
    s = _silu(c_ref[...])
    o_ref[0] = _dot(s.astype(BF16), w_ref[0].astype(BF16)) + b_ref[0]


def _mod_vectors(cc, w_ada, b_ada):
    depth, d, n6 = w_ada.shape
    tn = 1536
    return pl.pallas_call(
        _mod_kernel,
        out_shape=jax.ShapeDtypeStruct((depth, 8, n6), F32),
        grid=(depth, n6 // tn),
        in_specs=[pl.BlockSpec((8, d), lambda l, j: (0, 0)),
                  pl.BlockSpec((1, d, tn), lambda l, j: (l, 0, j)),
                  pl.BlockSpec((1, 1, tn), lambda l, j: (l, 0, j))],
        out_specs=pl.BlockSpec((1, 8, tn), lambda l, j: (l, 0, j)),
        compiler_params=_params("arbitrary", "arbitrary"),
        name="mod_vectors",
    )(cc, w_ada, b_ada.reshape(depth, 1, n6))


def _rope_tile(x, cos, sin_signed, d):
    lane = lax.broadcasted_iota(jnp.int32, x.shape, 1)
    first = (lane & (d - 1)) < (d // 2)
    partner = jnp.where(first, pltpu.roll(x, LANE - d // 2, axis=1), pltpu.roll(x, d // 2, axis=1))
    return x * cos + partner * sin_signed


def _inproj_kernel(*refs, rope, tm):
    if rope:
        (x_ref, g_ref, sh_ref, sc_ref, w_ref, ca_ref, sa_ref, cr_ref, sr_ref, cd_ref, sd_ref,
         qa_ref, ka_ref, va_ref, rq_ref, rk_ref, rv_ref, rg_ref,
         dqt_ref, dk_ref, dvt_ref, lx_ref, ly_ref) = refs
    else:
        (x_ref, g_ref, sh_ref, sc_ref, w_ref,
         qa_ref, ka_ref, va_ref, rq_ref, rk_ref, rv_ref, rg_ref,
         dqt_ref, dk_ref, dvt_ref, lx_ref, ly_ref) = refs
    x = x_ref[0]
    ms = jnp.mean(x * x, axis=-1, keepdims=True)
    y = x * lax.rsqrt(ms + NORM_EPS) * g_ref[...]
    h = y * (1.0 + sc_ref[0]) + sh_ref[0]
    hb = h.astype(BF16)

    def proj(off, width):
        return _dot(hb, w_ref[:, off:off + width])

    def roped(p, c_ref, s_ref, d, scale):
        tiles = []
        for t in range(p.shape[1] // LANE):
            pt = p[:, t * LANE:(t + 1) * LANE]
            if rope:
                pt = _rope_tile(pt, c_ref[...], s_ref[...], d)
            tiles.append(pt if scale == 1.0 else pt * scale)
        return tiles[0] if len(tiles) == 1 else jnp.concatenate(tiles, axis=1)

    ca = sa = cr = sr = cd = sd = None
    if rope:
        ca, sa, cr, sr, cd, sd = ca_ref, sa_ref, cr_ref, sr_ref, cd_ref, sd_ref
    qa_ref[0] = roped(proj(OFF_AQ, 256), ca, sa, HEAD_DIM, HEAD_DIM ** -0.5).astype(BF16)
    ka_ref[0] = roped(proj(OFF_AK, 128), ca, sa, HEAD_DIM, 1.0).astype(BF16)
    va_ref[0] = proj(OFF_AV, 128).astype(BF16)
    rq_ref[0] = roped(proj(OFF_RQ, 256), cr, sr, HEAD_DIM, 1.0).astype(BF16)
    rk_ref[0] = roped(proj(OFF_RK, 256), cr, sr, HEAD_DIM, HEAD_DIM ** -0.5).astype(BF16)
    rv_ref[0] = proj(OFF_RV, 256).astype(BF16)
    rg_ref[0] = proj(OFF_RG, 256)
    dq = roped(proj(OFF_DQ, 256), cd, sd, DIFF_DIM, DIFF_DIM ** -0.5 * LOG2E)
    dqt_ref[0] = dq.T.astype(BF16)
    dk_ref[0] = roped(proj(OFF_DK, 256), cd, sd, DIFF_DIM, 1.0).astype(BF16)
    dvt = proj(OFF_DV, 256).T.astype(BF16)
    ones_rows = (lax.broadcasted_iota(jnp.int32, (DIFF_VROWS - HEAD_DIM, DIFF_TK), 0) == 0).astype(BF16)
    for j in range(tm // DIFF_TK):
        for hd in range(N_HEADS):
            dvt_ref[0, j, hd, 0:HEAD_DIM, :] = dvt[hd * HEAD_DIM:(hd + 1) * HEAD_DIM, j * DIFF_TK:(j + 1) * DIFF_TK]
            dvt_ref[0, j, hd, HEAD_DIM:DIFF_VROWS, :] = ones_rows
    lx_ref[0] = proj(OFF_LX, 256)
    ly_ref[0] = proj(OFF_LY, 256)


def _in_projection(x, g, sh, sc, w_in, tables):
    b, l, d = x.shape
    rope = tables is not None
    tm = _row_tile(l, ROWS_PROJ)
    nt = l // tm
    row = lambda shape: pl.BlockSpec(shape, lambda i, j: (i, j, 0))
    vec = pl.BlockSpec((1, 1, d), lambda i, j: (i, 0, 0))
    in_specs = [row((1, tm, d)), pl.BlockSpec((1, d), lambda i, j: (0, 0)), vec, vec,
                pl.BlockSpec((d, D_IN), lambda i, j: (0, 0))]
    args = [x, g.reshape(1, d), sh, sc, w_in]
    if rope:
        in_specs += [pl.BlockSpec((tm, LANE), lambda i, j: (j, 0))] * 6
        args += list(tables)
    sds = jax.ShapeDtypeStruct
    out_shape = [sds((b, l, 256), BF16), sds((b, l, 128), BF16), sds((b, l, 128), BF16),
                 sds((b, l, 256), BF16), sds((b, l, 256), BF16), sds((b, l, 256), BF16),
                 sds((b, l, 256), F32),
                 sds((b, 256, l), BF16), sds((b, l, 256), BF16),
                 sds((b, l // DIFF_TK, N_HEADS, DIFF_VROWS, DIFF_TK), BF16),
                 sds((b, l, 256), F32), sds((b, l, 256), F32)]
    out_specs = [row((1, tm, 256)), row((1, tm, 128)), row((1, tm, 128)),
                 row((1, tm, 256)), row((1, tm, 256)), row((1, tm, 256)), row((1, tm, 256)),
                 pl.BlockSpec((1, 256, tm), lambda i, j: (i, 0, j)), row((1, tm, 256)),
                 pl.BlockSpec((1, tm // DIFF_TK, N_HEADS, DIFF_VROWS, DIFF_TK), lambda i, j: (i, j, 0, 0, 0)),
                 row((1, tm, 256)), row((1, tm, 256))]
    return pl.pallas_call(
        functools.partial(_inproj_kernel, rope=rope, tm=tm),
        out_shape=out_shape, grid=(b, nt), in_specs=in_specs, out_specs=out_specs,
        compiler_params=_params("arbitrary", "arbitrary"),
        name="in_projection_rope" if rope else "in_projection_ctx",
    )(*args)


def _swa_kernel(*refs, window, tq, n):
    if window:
        sink_ref, q_ref, k_ref, v_ref, kc_ref, vc_ref, o_ref, s_scr = refs
    else:
        sink_ref, q_ref, kc_ref, vc_ref, o_ref, s_scr = refs
    kc = kc_ref[0]
    vc = vc_ref[0]
    m = kc.shape[0]
    lane = lax.broadcasted_iota(jnp.int32, (1, LANE), 1)
    halves = [(lane >= half * HEAD_DIM) & (lane < (half + 1) * HEAD_DIM) for half in range(2)]

    def attend(q, kcat, vcat, valid):
        for t in range(2):
            qt = q[:, t * LANE:(t + 1) * LANE]
            for half in range(2):
                qm = jnp.where(halves[half], qt, jnp.zeros_like(qt))
                s = _dot_nt(qm, kcat)
                if valid is not None:
                    s = jnp.where(valid, s, -jnp.inf)
                s_scr[2 * t + half] = s
        out_tiles = []
        for t in range(2):
            out_t = None
            for half in range(2):
                sink = sink_ref[2 * t + half]
                s = s_scr[2 * t + half]
                mx = jnp.maximum(jnp.max(s, axis=1, keepdims=True), sink)
                e = jnp.exp(s - mx)
                den = jnp.sum(e, axis=1, keepdims=True) + jnp.exp(sink - mx)
                o = _dot(e.astype(BF16), vcat) / den
                out_t = o if out_t is None else jnp.where(halves[half], o, out_t)
            out_tiles.append(out_t)
        return jnp.concatenate(out_tiles, axis=1)

    if not window:
        o_ref[0] = attend(q_ref[0], kc, vc, None)
        return
    w = tq + 2 * WINDOW
    t0 = pl.program_id(1) * tq
    start = pl.multiple_of(jnp.clip(t0 - WINDOW, 0, n - w), WINDOW)
    kcat = jnp.concatenate([k_ref[0, pl.ds(start, w), :], kc], axis=0)
    vcat = jnp.concatenate([v_ref[0, pl.ds(start, w), :], vc], axis=0)
    qpos = t0 + lax.broadcasted_iota(jnp.int32, (tq, w + m), 0)
    col = lax.broadcasted_iota(jnp.int32, (tq, w + m), 1)
    valid = (jnp.abs(qpos - (start + col)) <= WINDOW) | (col >= w)
    o_ref[0] = attend(q_ref[0], kcat, vcat, valid)


def _swa_attention(sink_perm, q, k, v, kc, vc):
    b, n, _ = q.shape
    m = kc.shape[1]
    tq = 256
    smem = pl.BlockSpec(memory_space=pltpu.SMEM)
    ctx = pl.BlockSpec((1, m, 128), lambda i, j: (i, 0, 0))
    in_specs = [smem, pl.BlockSpec((1, tq, 256), lambda i, j: (i, j, 0)),
                pl.BlockSpec((1, n, 128), lambda i, j: (i, 0, 0)),
                pl.BlockSpec((1, n, 128), lambda i, j: (i, 0, 0)), ctx, ctx]
    return pl.pallas_call(
        functools.partial(_swa_kernel, window=True, tq=tq, n=n),
        out_shape=jax.ShapeDtypeStruct((b, n, 256), F32),
        grid=(b, n // tq), in_specs=in_specs,
        out_specs=pl.BlockSpec((1, tq, 256), lambda i, j: (i, j, 0)),
        scratch_shapes=[pltpu.VMEM((N_HEADS, tq, tq + 2 * WINDOW + m), F32)],
        compiler_params=_params("arbitrary", "arbitrary"),
        name="swa_attention",
    )(sink_perm, q, k, v, kc, vc)


def _ctx_attention(sink_perm, qc, kc, vc):
    b, m, _ = qc.shape
    smem = pl.BlockSpec(memory_space=pltpu.SMEM)
    ctx = pl.BlockSpec((1, m, 128), lambda i: (i, 0, 0))
    return pl.pallas_call(
        functools.partial(_swa_kernel, window=False, tq=m, n=m),
        out_shape=jax.ShapeDtypeStruct((b, m, 256), F32),
        grid=(b,), in_specs=[smem, pl.BlockSpec((1, m, 256), lambda i: (i, 0, 0)), ctx, ctx],
        out_specs=pl.BlockSpec((1, m, 256), lambda i: (i, 0, 0)),
        scratch_shapes=[pltpu.VMEM((N_HEADS, m, m), F32)],
        compiler_params=_params("arbitrary"),
        name="ctx_attention",
    )(sink_perm, qc, kc, vc)


RET_CHUNK = 128


def _ret_kernel(dl_lane_ref, dl_head_ref, q_ref, k_ref, v_ref, s0_ref, y_ref, sfin_ref,
                s_scr, dm_scr, xi_scr, zeta_scr, *, ns, group):
    c = RET_CHUNK
    nb = q_ref.shape[0]
    d = pl.program_id(0)
    si = pl.program_id(1)
    fwd = d == 0
    lane_head = lax.broadcasted_iota(jnp.int32, (1, GROUP), 1) // HEAD_DIM

    @pl.when(si == 0)
    def _():
        s_scr[...] = s0_ref[:, 0]
        lgl = -_softplus(-dl_lane_ref[0])
        row = lax.broadcasted_iota(jnp.int32, (c, 1), 0)
        xi_scr[...] = jnp.exp(lgl * jnp.where(fwd, row + 1, c - row).astype(F32))
        zeta_scr[...] = jnp.exp(lgl * jnp.where(fwd, c - 1 - row, row).astype(F32))
        ni = lax.broadcasted_iota(jnp.int32, (c, c), 0)
        mi = lax.broadcasted_iota(jnp.int32, (c, c), 1)
        dd = jnp.where(fwd, ni - mi, mi - ni)
        ddf = jnp.maximum(dd, 0).astype(F32)
        for h in range(N_HEADS):
            lgh = -_softplus(-dl_head_ref[0, h][0:1, :])
            dm_scr[h] = jnp.where(dd >= 0, jnp.exp(lgh * ddf), 0.0)

    dec = jnp.exp(-_softplus(-dl_lane_ref[0]) * float(c))
    rh = lax.broadcasted_iota(jnp.int32, (GROUP, GROUP), 0) // HEAD_DIM
    ch = lax.broadcasted_iota(jnp.int32, (GROUP, GROUP), 1) // HEAD_DIM
    same_head = rh == ch

    def chunk(g, carry):
        r0 = pl.multiple_of(jnp.where(fwd, g, group - 1 - g) * c, c)
        for bi in range(nb):
            q = q_ref[bi, pl.ds(r0, c), :]
            k = k_ref[bi, pl.ds(r0, c), :]
            v = v_ref[bi, pl.ds(r0, c), :]
            cross = _dot(q, s_scr[bi].astype(BF16)) * xi_scr[...]
            inner = jnp.zeros((c, GROUP), F32)
            for h in range(N_HEADS):
                qh = jnp.where(lane_head == h, q, jnp.zeros_like(q))
                s = _dot_nt(qh, k) * dm_scr[h]
                ih = _dot(s.astype(BF16), v)
                inner = jnp.where(lane_head == h, ih, inner)
            y_ref[bi, 0, pl.ds(r0, c), :] = inner + cross
            kz = (k.astype(F32) * zeta_scr[...]).T.astype(BF16)
            u = _dot(kz, v)
            s_scr[bi] = jnp.where(same_head, dec * s_scr[bi] + u, 0.0)
        return carry

    lax.fori_loop(0, group, chunk, 0, unroll=2)

    @pl.when(si == ns - 1)
    def _():
        sfin_ref[:, 0] = s_scr[...]


def _retention(dl_lane, dl_head, q, k, v, s0):
    b, l, _ = q.shape
    c = RET_CHUNK
    group = 8 if l % (8 * c) == 0 else 2
    ns = l // (group * c)
    sidx = lambda d, j: (0, jnp.where(d == 0, j, ns - 1 - j), 0)
    tok = pl.BlockSpec((b, group * c, 256), sidx)
    st = pl.BlockSpec((b, 1, 256, 256), lambda d, j: (0, d, 0, 0))
    return pl.pallas_call(
        functools.partial(_ret_kernel, ns=ns, group=group),
        out_shape=[jax.ShapeDtypeStruct((b, 2, l, 256), F32),
                   jax.ShapeDtypeStruct((b, 2, 256, 256), F32)],
        grid=(2, ns),
        in_specs=[pl.BlockSpec((1, 1, 256), lambda d, j: (d, 0, 0)),
                  pl.BlockSpec((1, N_HEADS, 8, c), lambda d, j: (d, 0, 0, 0)),
                  tok, tok, tok, st],
        out_specs=[pl.BlockSpec((b, 1, group * c, 256),
                                lambda d, j: (0, d, jnp.where(d == 0, j, ns - 1 - j), 0)),
                   st],
        scratch_shapes=[pltpu.VMEM((b, GROUP, GROUP), F32),
                        pltpu.VMEM((N_HEADS, c, c), F32),
                        pltpu.VMEM((c, GROUP), F32),
                        pltpu.VMEM((c, GROUP), F32)],
        compiler_params=_params("arbitrary", "arbitrary"),
        name="retention",
    )(dl_lane, dl_head, q, k, v, s0)


def _inner_tiles(tiles):
    for g in (8, 7, 6, 5, 4, 3, 2):
        if tiles % g == 0:
            return g
    return 1


def _diff_kernel(qt_ref, k_ref, vt_ref, o_ref, s_scr, *, tiles, nq):
    h = pl.program_id(1)
    inner = _inner_tiles(tiles)
    outer = tiles // inner
    total = nq * outer
    tq = DIFF_TQ
    rowi = lax.broadcasted_iota(jnp.int32, (LANE, 1), 0)
    hoff = (h % 2) * HEAD_DIM
    rowmask = [(rowi >= hoff + mp * DIFF_DIM) & (rowi < hoff + (mp + 1) * DIFF_DIM) for mp in range(2)]

    def step(mp, g_q, g_e, slot, m_e):
        mx = pv = None
        if g_q is not None:
            q0 = pl.multiple_of((g_q // outer) * tq, tq)
            qt = qt_ref[0, :, pl.ds(q0, tq)]
            qpad = jnp.where(rowmask[mp], qt, jnp.zeros_like(qt))
            kc = (g_q % outer) * inner
        if g_e is not None:
            vc = (g_e % outer) * inner
        for t in range(inner):
            sl = slice(t * DIFF_TK, (t + 1) * DIFF_TK)
            if g_q is not None:
                base = pl.multiple_of((kc + t) * DIFF_TK, DIFF_TK)
                s = _dot(k_ref[0, pl.ds(base, DIFF_TK), :], qpad)
                s_scr[mp, 1 - slot, sl, :] = s
                cm = jnp.max(s, axis=0, keepdims=True)
                mx = cm if mx is None else jnp.maximum(mx, cm)
            if g_e is not None:
                p = jnp.exp2(s_scr[mp, slot, sl, :] - m_e).astype(BF16)
                d = _dot(vt_ref[0, vc + t, 0], p)
                pv = d if pv is None else pv + d
        return mx, pv

    def advance(mp, g_q, g_e, slot, state):
        m_e, alpha, acc = state
        mx, pv = step(mp, g_q, g_e, slot, m_e)
        acc = alpha * acc + pv
        o0 = pl.multiple_of((g_e // outer) * tq, tq)
        o_ref[0, 0, mp, :, pl.ds(o0, tq)] = acc
        if g_q is None:
            return m_e, alpha, acc
        m_base = jnp.where(g_q % outer == 0, -jnp.inf, m_e)
        m_next = jnp.maximum(m_base, mx)
        return m_next, jnp.exp2(m_base - m_next), acc

    zeros = jnp.zeros((DIFF_VROWS, tq), F32)
    states = []
    for mp in range(2):
        m_first, _ = step(mp, 0, None, 1, None)
        states.append((m_first, jnp.zeros((1, tq), F32), zeros))
    states = tuple(states)

    def body(i, carry):
        carry = tuple(advance(mp, 2 * i + 1, 2 * i, 0, carry[mp]) for mp in range(2))
        return tuple(advance(mp, 2 * i + 2, 2 * i + 1, 1, carry[mp]) for mp in range(2))

    last = total - 1
    states = lax.fori_loop(0, last // 2, body, states, unroll=min(16, max(last // 2, 1)))
    if last % 2 == 1:
        states = tuple(advance(mp, last, last - 1, 0, states[mp]) for mp in range(2))
    for mp in range(2):
        advance(mp, None, last, last % 2, states[mp])


def _diff_attention(qt, k_all, vt_all):
    b, _, lq = qt.shape
    lk = k_all.shape[1]
    tiles = lk // DIFF_TK
    nq = lq // DIFF_TQ
    return pl.pallas_call(
        functools.partial(_diff_kernel, tiles=tiles, nq=nq),
        out_shape=jax.ShapeDtypeStruct((b, N_HEADS, 2, DIFF_VROWS, lq), F32),
        grid=(b, N_HEADS),
        in_specs=[pl.BlockSpec((1, LANE, lq), lambda i, h: (i, h // 2, 0)),
                  pl.BlockSpec((1, lk, LANE), lambda i, h: (i, 0, h // 2)),
                  pl.BlockSpec((1, tiles, 1, DIFF_VROWS, DIFF_TK), lambda i, h: (i, 0, h, 0, 0))],
        out_specs=pl.BlockSpec((1, 1, 2, DIFF_VROWS, lq), lambda i, h: (i, h, 0, 0, 0)),
        scratch_shapes=[pltpu.VMEM((2, 2, _inner_tiles(tiles) * DIFF_TK, DIFF_TQ), F32)],
        compiler_params=_params("arbitrary", "arbitrary"),
        name="diff_attention",
    )(qt, k_all, vt_all)


def _lru_kernel(x_ref, xp_ref, xn_ref, cw_ref, cb_ref, wa_ref, ba_ref, wx_ref, bx_ref, lam_ref,
                h0_ref, h_ref, carry, *, reverse, tm, nt):
    j = pl.program_id(1)
    tix = (nt - 1 - j) if reverse else j

    @pl.when(j == 0)
    def _():
        carry[...] = jnp.broadcast_to(h0_ref[0], carry.shape)

    x = x_ref[0]
    row = lax.broadcasted_iota(jnp.int32, (tm, 1), 0)
    prev = jnp.where(tix > 0, xp_ref[0][7:8, :], 0.0)
    nxt0 = jnp.where(tix < nt - 1, xn_ref[0][0:1, :], 0.0)
    nxt1 = jnp.where(tix < nt - 1, xn_ref[0][1:2, :], 0.0)
    xm1 = jnp.where(row == 0, prev, pltpu.roll(x, 1, axis=0))
    xp1 = jnp.where(row == tm - 1, nxt0, pltpu.roll(x, tm - 1, axis=0))
    xp2 = jnp.where(row == tm - 1, nxt1, jnp.where(row == tm - 2, nxt0, pltpu.roll(x, tm - 2, axis=0)))
    cw = cw_ref[...]
    u = cw[0:1] * xm1 + cw[1:2] * x + cw[2:3] * xp1 + cw[3:4] * xp2 + cb_ref[...]
    ub = u.astype(BF16)
    r = jax.nn.sigmoid(_dot(ub, wa_ref[0]) + ba_ref[0])
    gi = jax.nn.sigmoid(_dot(ub, wx_ref[0]) + bx_ref[0])
    log_a = -LRU_C * r * _softplus(-lam_ref[0])
    a = jnp.exp(log_a)
    th = jnp.tanh(log_a)
    bv = jnp.sqrt(-2.0 * th / (1.0 - th)) * (gi * u)
    k = 1
    while k < tm:
        if k < 8:
            keep = (row < tm - k) if reverse else (row >= k)
            shift = (tm - k) if reverse else k
            a_sh = jnp.where(keep, pltpu.roll(a, shift, axis=0), 1.0)
            b_sh = jnp.where(keep, pltpu.roll(bv, shift, axis=0), 0.0)
            bv = a * b_sh + bv
            a = a * a_sh
        elif reverse:
            head_b = a[:tm - k] * bv[k:] + bv[:tm - k]
            head_a = a[:tm - k] * a[k:]
            bv = jnp.concatenate([head_b, bv[tm - k:]], axis=0)
            a = jnp.concatenate([head_a, a[tm - k:]], axis=0)
        else:
            tail_b = a[k:] * bv[:tm - k] + bv[k:]
            tail_a = a[k:] * a[:tm - k]
            bv = jnp.concatenate([bv[:k], tail_b], axis=0)
            a = jnp.concatenate([a[:k], tail_a], axis=0)
        k *= 2
    hcur = a * carry[0:1, :] + bv
    h_ref[0] = hcur
    last = hcur[0:1, :] if reverse else hcur[tm - 1:tm, :]
    carry[...] = jnp.broadcast_to(last, carry.shape)


def _lru_scan(lx, conv_w, conv_b, wa_bd, ba, wx_bd, bx, lam, h0, reverse):
    b, l, w = lx.shape
    tm = _row_tile(l, ROWS_PROJ)
    nt = l // tm
    tb = tm // 8
    tidx = (lambda j: nt - 1 - j) if reverse else (lambda j: j)
    vec = pl.BlockSpec((1, w), lambda i, j: (0, 0))
    mat = pl.BlockSpec((1, w, w), lambda i, j: (0, 0, 0))
    return pl.pallas_call(
        functools.partial(_lru_kernel, reverse=reverse, tm=tm, nt=nt),
        out_shape=jax.ShapeDtypeStruct((b, l, w), F32),
        grid=(b, nt),
        in_specs=[pl.BlockSpec((1, tm, w), lambda i, j: (i, tidx(j), 0)),
                  pl.BlockSpec((1, 8, w), lambda i, j: (i, jnp.maximum(tidx(j) * tb - 1, 0), 0)),
                  pl.BlockSpec((1, 8, w), lambda i, j: (i, jnp.minimum((tidx(j) + 1) * tb, l // 8 - 1), 0)),
                  pl.BlockSpec((4, w), lambda i, j: (0, 0)), vec,
                  mat, vec, mat, vec, vec,
                  pl.BlockSpec((1, 1, w), lambda i, j: (i, 0, 0))],
        out_specs=pl.BlockSpec((1, tm, w), lambda i, j: (i, tidx(j), 0)),
        scratch_shapes=[pltpu.VMEM((8, w), F32)],
        compiler_params=_params("arbitrary", "arbitrary"),
        name="lru_scan_rev" if reverse else "lru_scan_fwd",
    )(lx, lx, lx, conv_w, conv_b.reshape(1, w), wa_bd.reshape(1, w, w), ba.reshape(1, w),
      wx_bd.reshape(1, w, w), bx.reshape(1, w), lam.reshape(1, w), h0)


def _outproj_kernel(x_ref, gt_ref, oa_ref, ry_ref, rg_ref, rgn_ref, lam_ref, dgn_ref, draw_ref, hf_ref, hb_ref,
                    ly_ref, bd_ref, w_ref, o_ref, *, lam_init):
    oa = oa_ref[0]
    y = ry_ref[0, 0] + ry_ref[0, 1]
    y2 = y * y
    hi = y2.astype(BF16)
    lo = (y2 - hi.astype(F32)).astype(BF16)
    ms = _dot(hi, bd_ref[...]) + _dot(lo, bd_ref[...])
    ob = y * lax.rsqrt(ms + NORM_EPS) * rgn_ref[...] * _silu(rg_ref[0])
    lv = lam_ref[...]
    lam = (jnp.exp(jnp.sum(lv[0:1] * lv[1:2], axis=1, keepdims=True))
           - jnp.exp(jnp.sum(lv[2:3] * lv[3:4], axis=1, keepdims=True)) + lam_init)
    heads = []
    for hd in range(N_HEADS):
        a0 = draw_ref[0, hd, 0]
        a1 = draw_ref[0, hd, 1]
        oh = (a0[0:HEAD_DIM] / a0[HEAD_DIM:HEAD_DIM + 1]
              - lam * (a1[0:HEAD_DIM] / a1[HEAD_DIM:HEAD_DIM + 1]))
        msd = jnp.mean(oh * oh, axis=0, keepdims=True)
        heads.append(oh * lax.rsqrt(msd + NORM_EPS) * (dgn_ref[hd] * (1.0 - lam_init)))
    od = jnp.concatenate(heads, axis=0).T
    ol = (hf_ref[0] + hb_ref[0]) * jax.nn.gelu(ly_ref[0])
    acc = _dot(oa.astype(BF16), w_ref[0:256, :])
    acc = acc + _dot(ob.astype(BF16), w_ref[256:512, :])
    acc = acc + _dot(od.astype(BF16), w_ref[512:768, :])
    acc = acc + _dot(ol.astype(BF16), w_ref[768:1024, :])
    o_ref[0] = x_ref[0] + gt_ref[0] * acc


def _out_projection(x, gt, oa, ry, rg, ret_gn, lam_vecs, diff_gn, lam_init, draw, hf, hb, ly, w_out):
    b, l, d = x.shape
    tm = _row_tile(l, ROWS_PROJ)
    row = lambda wdt: pl.BlockSpec((1, tm, wdt), lambda i, j: (i, j, 0))
    hd = jnp.arange(GROUP) // HEAD_DIM
    bd = jnp.where(hd[:, None] == hd[None, :], 1.0 / HEAD_DIM, 0.0).astype(BF16)
    return pl.pallas_call(
        functools.partial(_outproj_kernel, lam_init=lam_init),
        out_shape=jax.ShapeDtypeStruct((b, l, d), F32),
        grid=(b, l // tm),
        in_specs=[row(d), pl.BlockSpec((1, 1, d), lambda i, j: (i, 0, 0)), row(256),
                  pl.BlockSpec((1, 2, tm, 256), lambda i, j: (i, 0, j, 0)), row(256),
                  pl.BlockSpec((1, 256), lambda i, j: (0, 0)),
                  pl.BlockSpec((4, DIFF_DIM), lambda i, j: (0, 0)),
                  pl.BlockSpec((N_HEADS, HEAD_DIM, 1), lambda i, j: (0, 0, 0)),
                  pl.BlockSpec((1, N_HEADS, 2, DIFF_VROWS, tm), lambda i, j: (i, 0, 0, 0, j)),
                  row(256), row(256), row(256),
                  pl.BlockSpec((256, 256), lambda i, j: (0, 0)),
                  pl.BlockSpec((d, d), lambda i, j: (0, 0))],
        out_specs=row(d),
        compiler_params=_params("arbitrary", "arbitrary"),
        name="out_projection",
    )(x, gt, oa, ry, rg, ret_gn.reshape(1, 256), lam_vecs, diff_gn.reshape(N_HEADS, HEAD_DIM, 1), draw,
      hf, hb, ly, bd, w_out)


def _norm_modulate(x_ref, g_ref, sh_ref, sc_ref):
    x = x_ref[0]
    ms = jnp.mean(x * x, axis=-1, keepdims=True)
    y = x * lax.rsqrt(ms + NORM_EPS) * g_ref[...]
    return y * (1.0 + sc_ref[0]) + sh_ref[0]


def _ffn_kernel(x_ref, g_ref, sh_ref, sc_ref, gt_ref, w1_ref, w3_ref, w2_ref, o_ref, hb_scr, acc_scr,
                *, n_f):
    f = pl.program_id(2)

    @pl.when(f == 0)
    def _():
        hb_scr[...] = _norm_modulate(x_ref, g_ref, sh_ref, sc_ref).astype(BF16)
        acc_scr[...] = jnp.zeros_like(acc_scr)

    hb = hb_scr[...]
    a = _silu(_dot(hb, w1_ref[...])) * _dot(hb, w3_ref[...])
    acc_scr[...] += _dot(a.astype(BF16), w2_ref[...])

    @pl.when(f == n_f - 1)
    def _():
        o_ref[0] = x_ref[0] + gt_ref[0] * acc_scr[...]


def _ffn(x, g, sh, sc, gt, w1, w3, w2):
    b, l, d = x.shape
    ff = w1.shape[1]
    tm = _row_tile(l, ROWS_FFN)
    n_f = 2
    fc = ff // n_f
    vec = pl.BlockSpec((1, 1, d), lambda i, j, f: (i, 0, 0))
    return pl.pallas_call(
        functools.partial(_ffn_kernel, n_f=n_f),
        out_shape=jax.ShapeDtypeStruct((b, l, d), F32),
        grid=(b, l // tm, n_f),
        in_specs=[pl.BlockSpec((1, tm, d), lambda i, j, f: (i, j, 0)),
                  pl.BlockSpec((1, d), lambda i, j, f: (0, 0)), vec, vec, vec,
                  pl.BlockSpec((d, fc), lambda i, j, f: (0, f)),
                  pl.BlockSpec((d, fc), lambda i, j, f: (0, f)),
                  pl.BlockSpec((fc, d), lambda i, j, f: (f, 0))],
        out_specs=pl.BlockSpec((1, tm, d), lambda i, j, f: (i, j, 0)),
        scratch_shapes=[pltpu.VMEM((tm, d), BF16), pltpu.VMEM((tm, d), F32)],
        compiler_params=_params("arbitrary", "arbitrary", "arbitrary"),
        name="dense_ffn",
    )(x, g.reshape(1, d), sh, sc, gt, w1, w3, w2)


MOE_RT = 128


def _moe_kernel(x_ref, g_ref, sh_ref, sc_ref, gt_ref, fg_ref, rt_ref, w1_ref, w3_ref, w2_ref, o_ref,
                hb_scr, gate_scr, pos_scr, gatet_scr, post_scr, x_scr, y_scr, gs_scr, cnt_smem,
                *, n_e, n_f, tm, final_norm):
    e = pl.program_id(2)
    f = pl.program_id(3)
    rt2 = 2 * MOE_RT

    @pl.when((e == 0) & (f == 0))
    def _():
        h = _norm_modulate(x_ref, g_ref, sh_ref, sc_ref)
        hb = h.astype(BF16)
        hb_scr[...] = hb
        o_ref[0] = jnp.zeros(o_ref.shape[1:], F32)
        rt = rt_ref[...]
        rhi = rt.astype(BF16)
        rlo = (rt - rhi.astype(F32)).astype(BF16)
        hlo = (h - hb.astype(F32)).astype(BF16)
        logits = _dot(hb, rhi) + _dot(hb, rlo) + _dot(hlo, rhi)
        lane = lax.broadcasted_iota(jnp.int32, logits.shape, 1)
        logits = jnp.where(lane < N_EXPERTS, logits, -jnp.inf)
        m1 = jnp.max(logits, axis=1, keepdims=True)
        i1 = jnp.min(jnp.where(logits == m1, lane, LANE), axis=1, keepdims=True)
        rest = jnp.where(lane == i1, -jnp.inf, logits)
        m2 = jnp.max(rest, axis=1, keepdims=True)
        i2 = jnp.min(jnp.where(rest == m2, lane, LANE), axis=1, keepdims=True)
        e2 = jnp.exp(m2 - m1)
        den = 1.0 + e2
        gate = jnp.where(lane == i1, 1.0 / den, 0.0) + jnp.where(lane == i2, e2 / den, 0.0)
        gate_scr[...] = gate
        sel = jnp.where(gate > 0.0, 1.0, 0.0)
        selb = sel.astype(BF16)
        strip = 256
        for r0 in range(0, tm, strip):
            tok_r = r0 + lax.broadcasted_iota(jnp.int32, (strip, tm), 0)
            tok_c = lax.broadcasted_iota(jnp.int32, (strip, tm), 1)
            earlier = jnp.where(tok_c < tok_r, 1.0, 0.0).astype(BF16)
            pos_scr[r0:r0 + strip, :] = _dot(earlier, selb)
        post_scr[...] = pos_scr[...].T
        gatet_scr[...] = gate.T
        counts = jnp.sum(sel, axis=0, keepdims=True)
        for ee in range(n_e):
            cnt_smem[ee] = jnp.sum(jnp.where(lane[0:1] == ee, counts, 0.0)).astype(jnp.int32)

    n_tiles = (cnt_smem[e] + (MOE_RT - 1)) // MOE_RT
    n_pairs = (n_tiles + 1) // 2

    @pl.when(f == 0)
    def _():
        sub = lax.broadcasted_iota(jnp.int32, (8, 1), 0)
        pos_e = jnp.sum(jnp.where(sub == e, post_scr[0:8, :], 0.0), axis=0, keepdims=True)
        gate_e = jnp.sum(jnp.where(sub == e, gatet_scr[0:8, :], 0.0), axis=0, keepdims=True)

        def gather(jj, carry):
            r0 = pl.multiple_of(jj * rt2, rt2)
            slot = (r0 + lax.broadcasted_iota(jnp.int32, (rt2, 1), 0)).astype(F32)
            hit = (pos_e == slot) & (gate_e > 0.0)
            onehot = jnp.where(hit, 1.0, 0.0).astype(BF16)
            x_scr[pl.ds(r0, rt2), :] = _dot(onehot, hb_scr[...]).astype(BF16)
            gs = jnp.sum(jnp.where(hit, gate_e, 0.0), axis=1, keepdims=True)
            gs_scr[pl.ds(r0, rt2), :] = jnp.broadcast_to(gs, (rt2, LANE))
            y_scr[pl.ds(r0, rt2), :] = jnp.zeros((rt2, y_scr.shape[1]), F32)
            return carry

        lax.fori_loop(0, n_pairs, gather, 0)

    def expert(r0, rows):
        xj = x_scr[pl.ds(r0, rows), :]
        a = _silu(_dot(xj, w1_ref[0])) * _dot(xj, w3_ref[0])
        y_scr[pl.ds(r0, rows), :] += _dot(a.astype(BF16), w2_ref[0])

    def expert_pair(jj, carry):
        expert(pl.multiple_of(jj * rt2, rt2), rt2)
        return carry

    lax.fori_loop(0, n_tiles // 2, expert_pair, 0)

    @pl.when(n_tiles % 2 == 1)
    def _():
        expert(pl.multiple_of((n_tiles - 1) * MOE_RT, MOE_RT), MOE_RT)

    @pl.when(f == n_f - 1)
    def _():
        lane = lax.broadcasted_iota(jnp.int32, (1, LANE), 1)
        pos_col = jnp.sum(jnp.where(lane == e, pos_scr[...], 0.0), axis=1, keepdims=True)
        gate_col = jnp.sum(jnp.where(lane == e, gate_scr[...], 0.0), axis=1, keepdims=True)

        def scatter(jj, carry):
            r0 = pl.multiple_of(jj * rt2, rt2)
            slot = (r0 + lax.broadcasted_iota(jnp.int32, (1, rt2), 1)).astype(F32)
            onehot = jnp.where((pos_col == slot) & (gate_col > 0.0), 1.0, 0.0).astype(BF16)
            ys = (y_scr[pl.ds(r0, rt2), :] * gs_scr[pl.ds(r0, rt2), 0:1]).astype(BF16)
            o_ref[0] += _dot(onehot, ys)
            return carry

        lax.fori_loop(0, n_pairs, scatter, 0)

    @pl.when((e == n_e - 1) & (f == n_f - 1))
    def _():
        out = x_ref[0] + gt_ref[0] * o_ref[0]
        if final_norm:
            ms = jnp.mean(out * out, axis=-1, keepdims=True)
            out = out * lax.rsqrt(ms + NORM_EPS) * fg_ref[...]
        o_ref[0] = out


def _moe_ffn(x, g, sh, sc, gt, final_g, w1, w3, w2, router, final_norm):
    b, l, d = x.shape
    n_e, _, ff = w1.shape
    tm = _row_tile(l, ROWS_FFN)
    n_f = 2
    fc = ff // n_f
    vec = pl.BlockSpec((1, 1, d), lambda i, j, e, f: (i, 0, 0))
    return pl.pallas_call(
        functools.partial(_moe_kernel, n_e=n_e, n_f=n_f, tm=tm, final_norm=final_norm),
        out_shape=jax.ShapeDtypeStruct((b, l, d), F32),
        grid=(b, l // tm, n_e, n_f),
        in_specs=[pl.BlockSpec((1, tm, d), lambda i, j, e, f: (i, j, 0)),
                  pl.BlockSpec((1, d), lambda i, j, e, f: (0, 0)), vec, vec, vec,
                  pl.BlockSpec((1, d), lambda i, j, e, f: (0, 0)),
                  pl.BlockSpec((d, LANE), lambda i, j, e, f: (0, 0)),
                  pl.BlockSpec((1, d, fc), lambda i, j, e, f: (e, 0, f)),
                  pl.BlockSpec((1, d, fc), lambda i, j, e, f: (e, 0, f)),
                  pl.BlockSpec((1, fc, d), lambda i, j, e, f: (e, f, 0))],
        out_specs=pl.BlockSpec((1, tm, d), lambda i, j, e, f: (i, j, 0)),
        scratch_shapes=[pltpu.VMEM((tm, d), BF16),
                        pltpu.VMEM((tm, LANE), F32),
                        pltpu.VMEM((tm, LANE), F32),
                        pltpu.VMEM((LANE, tm), F32),
                        pltpu.VMEM((LANE, tm), F32),
                        pltpu.VMEM((tm, d), BF16),
                        pltpu.VMEM((tm, d), F32),
                        pltpu.VMEM((tm, LANE), F32),
                        pltpu.SMEM((n_e,), jnp.int32)],
        compiler_params=_params("arbitrary", "arbitrary", "arbitrary", "arbitrary"),
        name="moe_ffn",
    )(x, g.reshape(1, d), sh, sc, gt, final_g.reshape(1, d), router, w1, w3, w2)


def _rope_tables(n):
    rows = n // GRID_W
    row = jnp.repeat(jnp.arange(rows, dtype=F32), GRID_W)
    col = jnp.tile(jnp.arange(GRID_W, dtype=F32), rows)

    def axial(dim):
        nf = dim // 4
        inv = ROPE_BASE ** (-jnp.arange(nf, dtype=F32) / nf)
        return jnp.concatenate([row[:, None] * inv, col[:, None] * inv], axis=-1)

    def line(dim):
        nf = dim // 2
        inv = ROPE_BASE ** (-jnp.arange(nf, dtype=F32) / nf)
        return jnp.arange(n, dtype=F32)[:, None] * inv

    def expand(ang, dim):
        c = jnp.cos(ang)
        s = jnp.sin(ang)
        c = jnp.tile(jnp.concatenate([c, c], axis=-1), (1, LANE // dim))
        s = jnp.tile(jnp.concatenate([-s, s], axis=-1), (1, LANE // dim))
        return c, s

    ca, sa = expand(axial(HEAD_DIM), HEAD_DIM)
    cr, sr = expand(line(HEAD_DIM), HEAD_DIM)
    cd, sd = expand(axial(DIFF_DIM), DIFF_DIM)
    return ca, sa, cr, sr, cd, sd


_ATTN_HEAD_ORDER = (0, 2, 1, 3)


def _attn_perm():
    return jnp.concatenate([jnp.arange(HEAD_DIM) + HEAD_DIM * h for h in _ATTN_HEAD_ORDER])


def _block_diag(w):
    nb, blk, _ = w.shape
    out = jnp.zeros((nb * blk, nb * blk), w.dtype)
    for i in range(nb):
        out = out.at[i * blk:(i + 1) * blk, i * blk:(i + 1) * blk].set(w[i])
    return out


def kernel(x, c, ctx, c_ctx, w_ada, b_ada, g_norm1, g_norm2, w_in, w_out, attn_sink, ret_decay_logit,
           ret_gn, diff_lambda, diff_gn, conv_w, conv_b, lru_wa, lru_ba, lru_wx, lru_bx, lru_lambda,
           ffn_w1, ffn_w3, ffn_w2, moe_router, moe_w1, moe_w3, moe_w2, final_norm):
    b, n, d = x.shape
    m = ctx.shape[1]
    tables = _rope_tables(n)
    perm = _attn_perm()

    cc = jnp.concatenate([c, c_ctx[None, :], jnp.zeros((8 - b - 1, d), F32)], axis=0)
    mods = _mod_vectors(cc, w_ada, b_ada)

    xc = ctx
    for layer in range(DEPTH):
        need_ctx = layer < DEPTH - 1
        lam_init = 0.8 - 0.6 * math.exp(-0.3 * layer)
        mod = mods[layer]
        sh1, sc1, gt1, sh2, sc2, gt2 = [mod[:b, None, i * d:(i + 1) * d] for i in range(6)]
        sh1c, sc1c, gt1c, sh2c, sc2c, gt2c = [
            jnp.broadcast_to(mod[b:b + 1, None, i * d:(i + 1) * d], (b, 1, d)) for i in range(6)]

        wi = w_in[layer]
        wi = jnp.concatenate([wi[:, :256][:, perm], wi[:, 256:]], axis=1).astype(BF16)
        wo = w_out[layer]
        wo = jnp.concatenate([wo[:256][perm], wo[256:]], axis=0).astype(BF16)
        sink_perm = attn_sink[layer][jnp.array(_ATTN_HEAD_ORDER)]

        (qa, ka, va, rq, rk, rv, rg, dqt, dk, dvt, lx, ly) = _in_projection(
            x, g_norm1[layer], sh1, sc1, wi, tables)
        (qac, kac, vac, rqc, rkc, rvc, rgc, dqtc, dkc, dvtc, lxc, lyc) = _in_projection(
            xc, g_norm1[layer], sh1c, sc1c, wi, None)

        oa = _swa_attention(sink_perm, qa, ka, va, kac, vac)
        dl = ret_decay_logit[layer]
        dl_lane = jnp.repeat(dl, HEAD_DIM, axis=1).reshape(2, 1, GROUP)
        dl_head = jnp.broadcast_to(dl[:, :, None, None], (2, N_HEADS, 8, RET_CHUNK))
        ryc, s_ctx = _retention(dl_lane, dl_head, rqc, rkc, rvc, jnp.zeros((b, 2, GROUP, GROUP), F32))
        ry, _ = _retention(dl_lane, dl_head, rq, rk, rv, s_ctx)
        k_all = jnp.concatenate([dk, dkc], axis=1)
        vt_all = jnp.concatenate([dvt, dvtc], axis=1)
        draw = _diff_attention(dqt, k_all, vt_all)
        assemble = functools.partial(_out_projection, ret_gn=ret_gn[layer], lam_vecs=diff_lambda[layer],
                                     diff_gn=diff_gn[layer], lam_init=lam_init, w_out=wo)
        lru = []
        lru_c = []
        for direction in range(2):
            prm = (conv_w[layer], conv_b[layer],
                   _block_diag(lru_wa[layer, direction]).astype(BF16), lru_ba[layer, direction],
                   _block_diag(lru_wx[layer, direction]).astype(BF16), lru_bx[layer, direction],
                   lru_lambda[layer, direction])
            rev = direction == 1
            hc = _lru_scan(lxc, *prm, jnp.zeros((b, 1, GROUP), F32), rev)
            h0 = hc[:, 0:1, :] if rev else hc[:, m - 1:m, :]
            lru.append(_lru_scan(lx, *prm, h0, rev))
            lru_c.append(hc)

        x = assemble(x, gt1, oa, ry, rg, draw=draw, hf=lru[0], hb=lru[1], ly=ly)
        j = layer // 2
        if layer % 2 == 0:
            ffn = functools.partial(_ffn, w1=ffn_w1[j].astype(BF16), w3=ffn_w3[j].astype(BF16),
                                    w2=ffn_w2[j].astype(BF16))
            x = ffn(x, g_norm2[layer], sh2, sc2, gt2)
        else:
            moe = functools.partial(_moe_ffn, final_g=final_norm, w1=moe_w1[j].astype(BF16),
                                    w3=moe_w3[j].astype(BF16), w2=moe_w2[j].astype(BF16),
                                    router=jnp.pad(moe_router[j], ((0, 0), (0, LANE - N_EXPERTS))))
            x = moe(x, g_norm2[layer], sh2, sc2, gt2, final_norm=layer == DEPTH - 1)
            ffn = functools.partial(moe, final_norm=False)

        if need_ctx:
            oac = _ctx_attention(sink_perm, qac, kac, vac)
            drawc = _diff_attention(dqtc, dkc, dvtc)
            xc = assemble(xc, gt1c, oac, ryc, rgc, draw=drawc, hf=lru_c[0], hb=lru_c[1], ly=lyc)
            xc = ffn(xc, g_norm2[layer], sh2c, sc2c, gt2c)
    assert DEPTH % 2 == 0, "the closing RMSNorm is fused into the last (routed) layer"
    return x
```

```python
import functools
import math

import jax
import jax.numpy as jnp
from jax import lax
from jax.experimental import pallas as pl
from jax.experimental.pallas import tpu as pltpu

F32 = jnp.float32
BF16 = jnp.bfloat16

D_MODEL = 1024
DEPTH = 2
GRID_W = 64
HEAD_DIM = 64
GROUP = D_MODEL // 4
N_HEADS = GROUP // HEAD_DIM
DIFF_DIM = HEAD_DIM // 2
WINDOW = 128
NORM_EPS = 1e-6
ROPE_BASE = 10000.0
LRU_BLOCKS = 4
LRU_C = 8.0
N_EXPERTS = 8
LANE = 128
LOG2E = 1.4426950408889634
DIFF_TQ = 256
DIFF_TK = 256
DIFF_VROWS = HEAD_DIM + 16

OFF_AQ, OFF_AK, OFF_AV = 0, 256, 384
OFF_RQ, OFF_RK, OFF_RV, OFF_RG = 512, 768, 1024, 1280
OFF_DQ, OFF_DK, OFF_DV = 1536, 1792, 2048
OFF_LX, OFF_LY = 2304, 2560
D_IN = 2816

VMEM_LIMIT = 56 * 1024 * 1024
ROWS_PROJ = 512
ROWS_FFN = 1024
ROWS_MIN = 256


def _row_tile(l, preferred):
    tm = preferred if l % preferred == 0 else ROWS_MIN
    assert l % tm == 0, (l, tm)
    return tm


def _params(*sem):
    return pltpu.CompilerParams(dimension_semantics=sem, vmem_limit_bytes=VMEM_LIMIT)


def _silu(x):
    return x * jax.nn.sigmoid(x)


def _softplus(x):
    return jnp.maximum(x, 0.0) + jnp.log(1.0 + jnp.exp(-jnp.abs(x)))


def _dot(a, b):
    return jnp.dot(a, b, preferred_element_type=F32)


def _dot_nt(a, b):
    return lax.dot_general(a, b, (((1,), (1,)), ((), ())), preferred_element_type=F32)


def _mod_kernel(c_ref, w_ref, b_ref, o_ref):
    s = _silu(c_ref[...])
    o_ref[0] = _dot(s.astype(BF16), w_ref[0].astype(BF16)) + b_ref[0]


def _mod_vectors(cc, w_ada, b_ada):
    depth, d, n6 = w_ada.shape
    tn = 1536
    return pl.pallas_call(
        _mod_kernel,
        out_shape=jax.ShapeDtypeStruct((depth, 8, n6), F32),
        grid=(depth, n6 // tn),
        in_specs=[pl.BlockSpec((8, d), lambda l, j: (0, 0)),
                  pl.BlockSpec((1, d, tn), lambda l, j: (l, 0, j)),
                  pl.BlockSpec((1, 1, tn), lambda l, j: (l, 0, j))],
        out_specs=pl.BlockSpec((1, 8, tn), lambda l, j: (l, 0, j)),
        compiler_params=_params("arbitrary", "arbitrary"),
        name="mod_vectors",
    )(cc, w_ada, b_ada.reshape(depth, 1, n6))


def _rope_tile(x, cos, sin_signed, d):
    lane = lax.broadcasted_iota(jnp.int32, x.shape, 1)
    first = (lane & (d - 1)) < (d // 2)
    partner = jnp.where(first, pltpu.roll(x, LANE - d // 2, axis=1), pltpu.roll(x, d // 2, axis=1))
    return x * cos + partner * sin_signed


def _inproj_kernel(*refs, rope, tm):
    if rope:
        (x_ref, g_ref, sh_ref, sc_ref, w_ref, ca_ref, sa_ref, cr_ref, sr_ref, cd_ref, sd_ref,
         qa_ref, ka_ref, va_ref, rq_ref, rk_ref, rv_ref, rg_ref,
         dqt_ref, dk_ref, dvt_ref, lx_ref, ly_ref) = refs
    else:
        (x_ref, g_ref, sh_ref, sc_ref, w_ref,
         qa_ref, ka_ref, va_ref, rq_ref, rk_ref, rv_ref, rg_ref,
         dqt_ref, dk_ref, dvt_ref, lx_ref, ly_ref) = refs
    x = x_ref[0]
    ms = jnp.mean(x * x, axis=-1, keepdims=True)
    y = x * lax.rsqrt(ms + NORM_EPS) * g_ref[...]
    h = y * (1.0 + sc_ref[0]) + sh_ref[0]
    hb = h.astype(BF16)

    def proj(off, width):
        return _dot(hb, w_ref[:, off:off + width])

    def roped(p, c_ref, s_ref, d, scale):
        tiles = []
        for t in range(p.shape[1] // LANE):
            pt = p[:, t * LANE:(t + 1) * LANE]
            if rope:
                pt = _rope_tile(pt, c_ref[...], s_ref[...], d)
            tiles.append(pt if scale == 1.0 else pt * scale)
        return tiles[0] if len(tiles) == 1 else jnp.concatenate(tiles, axis=1)

    ca = sa = cr = sr = cd = sd = None
    if rope:
        ca, sa, cr, sr, cd, sd = ca_ref, sa_ref, cr_ref, sr_ref, cd_ref, sd_ref
    qa_ref[0] = roped(proj(OFF_AQ, 256), ca, sa, HEAD_DIM, HEAD_DIM ** -0.5).astype(BF16)
    ka_ref[0] = roped(proj(OFF_AK, 128), ca, sa, HEAD_DIM, 1.0).astype(BF16)
    va_ref[0] = proj(OFF_AV, 128).astype(BF16)
    rq_ref[0] = roped(proj(OFF_RQ, 256), cr, sr, HEAD_DIM, 1.0).astype(BF16)
    rk_ref[0] = roped(proj(OFF_RK, 256), cr, sr, HEAD_DIM, HEAD_DIM ** -0.5).astype(BF16)
    rv_ref[0] = proj(OFF_RV, 256).astype(BF16)
    rg_ref[0] = proj(OFF_RG, 256)
    dq = roped(proj(OFF_DQ, 256), cd, sd, DIFF_DIM, DIFF_DIM ** -0.5 * LOG2E)
    dqt_ref[0] = dq.T.astype(BF16)
    dk_ref[0] = roped(proj(OFF_DK, 256), cd, sd, DIFF_DIM, 1.0).astype(BF16)
    dvt = proj(OFF_DV, 256).T.astype(BF16)
    ones_rows = (lax.broadcasted_iota(jnp.int32, (DIFF_VROWS - HEAD_DIM, DIFF_TK), 0) == 0).astype(BF16)
    for j in range(tm // DIFF_TK):
        for hd in range(N_HEADS):
            dvt_ref[0, j, hd, 0:HEAD_DIM, :] = dvt[hd * HEAD_DIM:(hd + 1) * HEAD_DIM, j * DIFF_TK:(j + 1) * DIFF_TK]
            dvt_ref[0, j, hd, HEAD_DIM:DIFF_VROWS, :] = ones_rows
    lx_ref[0] = proj(OFF_LX, 256)
    ly_ref[0] = proj(OFF_LY, 256)


def _in_projection(x, g, sh, sc, w_in, tables):
    b, l, d = x.shape
    rope = tables is not None
    tm = _row_tile(l, ROWS_PROJ)
    nt = l // tm
    row = lambda shape: pl.BlockSpec(shape, lambda i, j: (i, j, 0))
    vec = pl.BlockSpec((1, 1, d), lambda i, j: (i, 0, 0))
    in_specs = [row((1, tm, d)), pl.BlockSpec((1, d), lambda i, j: (0, 0)), vec, vec,
                pl.BlockSpec((d, D_IN), lambda i, j: (0, 0))]
    args = [x, g.reshape(1, d), sh, sc, w_in]
    if rope:
        in_specs += [pl.BlockSpec((tm, LANE), lambda i, j: (j, 0))] * 6
        args += list(tables)
    sds = jax.ShapeDtypeStruct
    out_shape = [sds((b, l, 256), BF16), sds((b, l, 128), BF16), sds((b, l, 128), BF16),
                 sds((b, l, 256), BF16), sds((b, l, 256), BF16), sds((b, l, 256), BF16),
                 sds((b, l, 256), F32),
                 sds((b, 256, l), BF16), sds((b, l, 256), BF16),
                 sds((b, l // DIFF_TK, N_HEADS, DIFF_VROWS, DIFF_TK), BF16),
                 sds((b, l, 256), F32), sds((b, l, 256), F32)]
    out_specs = [row((1, tm, 256)), row((1, tm, 128)), row((1, tm, 128)),
                 row((1, tm, 256)), row((1, tm, 256)), row((1, tm, 256)), row((1, tm, 256)),
                 pl.BlockSpec((1, 256, tm), lambda i, j: (i, 0, j)), row((1, tm, 256)),
                 pl.BlockSpec((1, tm // DIFF_TK, N_HEADS, DIFF_VROWS, DIFF_TK), lambda i, j: (i, j, 0, 0, 0)),
                 row((1, tm, 256)), row((1, tm, 256))]
    return pl.pallas_call(
        functools.partial(_inproj_kernel, rope=rope, tm=tm),
        out_shape=out_shape, grid=(b, nt), in_specs=in_specs, out_specs=out_specs,
        compiler_params=_params("arbitrary", "arbitrary"),
        name="in_projection_rope" if rope else "in_projection_ctx",
    )(*args)


def _swa_kernel(*refs, window, tq, n):
    if window:
        sink_ref, q_ref, k_ref, v_ref, kc_ref, vc_ref, o_ref, s_scr = refs
    else:
        sink_ref, q_ref, kc_ref, vc_ref, o_ref, s_scr = refs
    kc = kc_ref[0]
    vc = vc_ref[0]
    m = kc.shape[0]
    lane = lax.broadcasted_iota(jnp.int32, (1, LANE), 1)
    halves = [(lane >= half * HEAD_DIM) & (lane < (half + 1) * HEAD_DIM) for half in range(2)]

    def attend(q, kcat, vcat, valid):
        for t in range(2):
            qt = q[:, t * LANE:(t + 1) * LANE]
            for half in range(2):
                qm = jnp.where(halves[half], qt, jnp.zeros_like(qt))
                s = _dot_nt(qm, kcat)
                if valid is not None:
                    s = jnp.where(valid, s, -jnp.inf)
                s_scr[2 * t + half] = s
        out_tiles = []
        for t in range(2):
            out_t = None
            for half in range(2):
                sink = sink_ref[2 * t + half]
                s = s_scr[2 * t + half]
                mx = jnp.maximum(jnp.max(s, axis=1, keepdims=True), sink)
                e = jnp.exp(s - mx)
                den = jnp.sum(e, axis=1, keepdims=True) + jnp.exp(sink - mx)
                o = _dot(e.astype(BF16), vcat) / den
                out_t = o if out_t is None else jnp.where(halves[half], o, out_t)
            out_tiles.append(out_t)
        return jnp.concatenate(out_tiles, axis=1)

    if not window:
        o_ref[0] = attend(q_ref[0], kc, vc, None).astype(o_ref.dtype)
        return
    w = tq + 2 * WINDOW
    t0 = pl.program_id(1) * tq
    start = pl.multiple_of(jnp.clip(t0 - WINDOW, 0, n - w), WINDOW)
    kcat = jnp.concatenate([k_ref[0, pl.ds(start, w), :], kc], axis=0)
    vcat = jnp.concatenate([v_ref[0, pl.ds(start, w), :], vc], axis=0)
    qpos = t0 + lax.broadcasted_iota(jnp.int32, (tq, w + m), 0)
    col = lax.broadcasted_iota(jnp.int32, (tq, w + m), 1)
    valid = (jnp.abs(qpos - (start + col)) <= WINDOW) | (col >= w)
    o_ref[0] = attend(q_ref[0], kcat, vcat, valid).astype(o_ref.dtype)


def _swa_attention(sink_perm, q, k, v, kc, vc):
    b, n, _ = q.shape
    m = kc.shape[1]
    tq = 256
    smem = pl.BlockSpec(memory_space=pltpu.SMEM)
    ctx = pl.BlockSpec((1, m, 128), lambda i, j: (i, 0, 0))
    in_specs = [smem, pl.BlockSpec((1, tq, 256), lambda i, j: (i, j, 0)),
                pl.BlockSpec((1, n, 128), lambda i, j: (i, 0, 0)),
                pl.BlockSpec((1, n, 128), lambda i, j: (i, 0, 0)), ctx, ctx]
    return pl.pallas_call(
        functools.partial(_swa_kernel, window=True, tq=tq, n=n),
        out_shape=jax.ShapeDtypeStruct((b, n, 256), BF16),
        grid=(b, n // tq), in_specs=in_specs,
        out_specs=pl.BlockSpec((1, tq, 256), lambda i, j: (i, j, 0)),
        scratch_shapes=[pltpu.VMEM((N_HEADS, tq, tq + 2 * WINDOW + m), F32)],
        compiler_params=_params("arbitrary", "arbitrary"),
        name="swa_attention",
    )(sink_perm, q, k, v, kc, vc)


def _ctx_attention(sink_perm, qc, kc, vc):
    b, m, _ = qc.shape
    smem = pl.BlockSpec(memory_space=pltpu.SMEM)
    ctx = pl.BlockSpec((1, m, 128), lambda i: (i, 0, 0))
    return pl.pallas_call(
        functools.partial(_swa_kernel, window=False, tq=m, n=m),
        out_shape=jax.ShapeDtypeStruct((b, m, 256), BF16),
        grid=(b,), in_specs=[smem, pl.BlockSpec((1, m, 256), lambda i: (i, 0, 0)), ctx, ctx],
        out_specs=pl.BlockSpec((1, m, 256), lambda i: (i, 0, 0)),
        scratch_shapes=[pltpu.VMEM((N_HEADS, m, m), F32)],
        compiler_params=_params("arbitrary"),
        name="ctx_attention",
    )(sink_perm, qc, kc, vc)


RET_CHUNK = 128


def _ret_kernel(dl_lane_ref, dl_head_ref, q_ref, k_ref, v_ref, s0_ref, y_ref, sfin_ref,
                s_scr, dm_scr, xi_scr, zeta_scr, *, ns, group):
    c = RET_CHUNK
    nb = q_ref.shape[0]
    d = pl.program_id(0)
    si = pl.program_id(1)
    fwd = d == 0
    lane_head = lax.broadcasted_iota(jnp.int32, (1, GROUP), 1) // HEAD_DIM

    @pl.when(si == 0)
    def _():
        s_scr[...] = s0_ref[:, 0]
        lgl = -_softplus(-dl_lane_ref[0])
        row = lax.broadcasted_iota(jnp.int32, (c, 1), 0)
        xi_scr[...] = jnp.exp(lgl * jnp.where(fwd, row + 1, c - row).astype(F32))
        zeta_scr[...] = jnp.exp(lgl * jnp.where(fwd, c - 1 - row, row).astype(F32))
        ni = lax.broadcasted_iota(jnp.int32, (c, c), 0)
        mi = lax.broadcasted_iota(jnp.int32, (c, c), 1)
        dd = jnp.where(fwd, ni - mi, mi - ni)
        ddf = jnp.maximum(dd, 0).astype(F32)
        for h in range(N_HEADS):
            lgh = -_softplus(-dl_head_ref[0, h][0:1, :])
            dm_scr[h] = jnp.where(dd >= 0, jnp.exp(lgh * ddf), 0.0)

    dec = jnp.exp(-_softplus(-dl_lane_ref[0]) * float(c))
    rh = lax.broadcasted_iota(jnp.int32, (GROUP, GROUP), 0) // HEAD_DIM
    ch = lax.broadcasted_iota(jnp.int32, (GROUP, GROUP), 1) // HEAD_DIM
    same_head = rh == ch

    def chunk(g, carry):
        r0 = pl.multiple_of(jnp.where(fwd, g, group - 1 - g) * c, c)
        for bi in range(nb):
            q = q_ref[bi, pl.ds(r0, c), :]
            k = k_ref[bi, pl.ds(r0, c), :]
            v = v_ref[bi, pl.ds(r0, c), :]
            cross = _dot(q, s_scr[bi].astype(BF16)) * xi_scr[...]
            inner = jnp.zeros((c, GROUP), F32)
            for h in range(N_HEADS):
                qh = jnp.where(lane_head == h, q, jnp.zeros_like(q))
                s = _dot_nt(qh, k) * dm_scr[h]
                ih = _dot(s.astype(BF16), v)
                inner = jnp.where(lane_head == h, ih, inner)
            y_ref[bi, 0, pl.ds(r0, c), :] = inner + cross
            kz = (k.astype(F32) * zeta_scr[...]).T.astype(BF16)
            u = _dot(kz, v)
            s_scr[bi] = jnp.where(same_head, dec * s_scr[bi] + u, 0.0)
        return carry

    lax.fori_loop(0, group, chunk, 0, unroll=2)

    @pl.when(si == ns - 1)
    def _():
        sfin_ref[:, 0] = s_scr[...]


def _retention(dl_lane, dl_head, q, k, v, s0):
    b, l, _ = q.shape
    c = RET_CHUNK
    group = 8 if l % (8 * c) == 0 else 2
    ns = l // (group * c)
    sidx = lambda d, j: (0, jnp.where(d == 0, j, ns - 1 - j), 0)
    tok = pl.BlockSpec((b, group * c, 256), sidx)
    st = pl.BlockSpec((b, 1, 256, 256), lambda d, j: (0, d, 0, 0))
    return pl.pallas_call(
        functools.partial(_ret_kernel, ns=ns, group=group),
        out_shape=[jax.ShapeDtypeStruct((b, 2, l, 256), F32),
                   jax.ShapeDtypeStruct((b, 2, 256, 256), F32)],
        grid=(2, ns),
        in_specs=[pl.BlockSpec((1, 1, 256), lambda d, j: (d, 0, 0)),
                  pl.BlockSpec((1, N_HEADS, 8, c), lambda d, j: (d, 0, 0, 0)),
                  tok, tok, tok, st],
        out_specs=[pl.BlockSpec((b, 1, group * c, 256),
                                lambda d, j: (0, d, jnp.where(d == 0, j, ns - 1 - j), 0)),
                   st],
        scratch_shapes=[pltpu.VMEM((b, GROUP, GROUP), F32),
                        pltpu.VMEM((N_HEADS, c, c), F32),
                        pltpu.VMEM((c, GROUP), F32),
                        pltpu.VMEM((c, GROUP), F32)],
        compiler_params=_params("arbitrary", "arbitrary"),
        name="retention",
    )(dl_lane, dl_head, q, k, v, s0)


def _inner_tiles(tiles):
    for g in (8, 7, 6, 5, 4, 3, 2):
        if tiles % g == 0:
            return g
    return 1


def _diff_kernel(qt_ref, k_ref, vt_ref, o_ref, s_scr, *, tiles, nq):
    h = pl.program_id(1)
    inner = _inner_tiles(tiles)
    outer = tiles // inner
    total = nq * outer
    tq = DIFF_TQ
    rowi = lax.broadcasted_iota(jnp.int32, (LANE, 1), 0)
    hoff = (h % 2) * HEAD_DIM
    rowmask = [(rowi >= hoff + mp * DIFF_DIM) & (rowi < hoff + (mp + 1) * DIFF_DIM) for mp in range(2)]

    def step(mp, g_q, g_e, slot, m_e):
        mx = pv = None
        if g_q is not None:
            q0 = pl.multiple_of((g_q // outer) * tq, tq)
            qt = qt_ref[0, :, pl.ds(q0, tq)]
            qpad = jnp.where(rowmask[mp], qt, jnp.zeros_like(qt))
            kc = (g_q % outer) * inner
        if g_e is not None:
            vc = (g_e % outer) * inner
        for t in range(inner):
            sl = slice(t * DIFF_TK, (t + 1) * DIFF_TK)
            if g_q is not None:
                base = pl.multiple_of((kc + t) * DIFF_TK, DIFF_TK)
                s = _dot(k_ref[0, pl.ds(base, DIFF_TK), :], qpad)
                s_scr[mp, 1 - slot, sl, :] = s
                cm = jnp.max(s, axis=0, keepdims=True)
                mx = cm if mx is None else jnp.maximum(mx, cm)
            if g_e is not None:
                p = jnp.exp2(s_scr[mp, slot, sl, :] - m_e).astype(BF16)
                d = _dot(vt_ref[0, vc + t, 0], p)
                pv = d if pv is None else pv + d
        return mx, pv

    def advance(mp, g_q, g_e, slot, state):
        m_e, alpha, acc = state
        mx, pv = step(mp, g_q, g_e, slot, m_e)
        acc = alpha * acc + pv
        o0 = pl.multiple_of((g_e // outer) * tq, tq)
        o_ref[0, 0, mp, :, pl.ds(o0, tq)] = acc
        if g_q is None:
            return m_e, alpha, acc
        m_base = jnp.where(g_q % outer == 0, -jnp.inf, m_e)
        m_next = jnp.maximum(m_base, mx)
        return m_next, jnp.exp2(m_base - m_next), acc

    zeros = jnp.zeros((DIFF_VROWS, tq), F32)
    states = []
    for mp in range(2):
        m_first, _ = step(mp, 0, None, 1, None)
        states.append((m_first, jnp.zeros((1, tq), F32), zeros))
    states = tuple(states)

    def body(i, carry):
        carry = tuple(advance(mp, 2 * i + 1, 2 * i, 0, carry[mp]) for mp in range(2))
        return tuple(advance(mp, 2 * i + 2, 2 * i + 1, 1, carry[mp]) for mp in range(2))

    last = total - 1
    states = lax.fori_loop(0, last // 2, body, states, unroll=min(8, max(last // 2, 1)))
    if last % 2 == 1:
        states = tuple(advance(mp, last, last - 1, 0, states[mp]) for mp in range(2))
    for mp in range(2):
        advance(mp, None, last, last % 2, states[mp])


def _diff_attention(qt, k_all, vt_all):
    b, _, lq = qt.shape
    lk = k_all.shape[1]
    tiles = lk // DIFF_TK
    nq = lq // DIFF_TQ
    return pl.pallas_call(
        functools.partial(_diff_kernel, tiles=tiles, nq=nq),
        out_shape=jax.ShapeDtypeStruct((b, N_HEADS, 2, DIFF_VROWS, lq), F32),
        grid=(b, N_HEADS),
        in_specs=[pl.BlockSpec((1, LANE, lq), lambda i, h: (i, h // 2, 0)),
                  pl.BlockSpec((1, lk, LANE), lambda i, h: (i, 0, h // 2)),
                  pl.BlockSpec((1, tiles, 1, DIFF_VROWS, DIFF_TK), lambda i, h: (i, 0, h, 0, 0))],
        out_specs=pl.BlockSpec((1, 1, 2, DIFF_VROWS, lq), lambda i, h: (i, h, 0, 0, 0)),
        scratch_shapes=[pltpu.VMEM((2, 2, _inner_tiles(tiles) * DIFF_TK, DIFF_TQ), F32)],
        compiler_params=_params("arbitrary", "arbitrary"),
        name="diff_attention",
    )(qt, k_all, vt_all)


def _lru_kernel(x_ref, xp_ref, xn_ref, cw_ref, cb_ref, wa_ref, ba_ref, wx_ref, bx_ref, lam_ref,
                h0_ref, h_ref, carry, *, reverse, tm, nt):
    j = pl.program_id(1)
    tix = (nt - 1 - j) if reverse else j

    @pl.when(j == 0)
    def _():
        carry[...] = jnp.broadcast_to(h0_ref[0], carry.shape)

    x = x_ref[0]
    row = lax.broadcasted_iota(jnp.int32, (tm, 1), 0)
    prev = jnp.where(tix > 0, xp_ref[0][7:8, :], 0.0)
    nxt0 = jnp.where(tix < nt - 1, xn_ref[0][0:1, :], 0.0)
    nxt1 = jnp.where(tix < nt - 1, xn_ref[0][1:2, :], 0.0)
    xm1 = jnp.where(row == 0, prev, pltpu.roll(x, 1, axis=0))
    xp1 = jnp.where(row == tm - 1, nxt0, pltpu.roll(x, tm - 1, axis=0))
    xp2 = jnp.where(row == tm - 1, nxt1, jnp.where(row == tm - 2, nxt0, pltpu.roll(x, tm - 2, axis=0)))
    cw = cw_ref[...]
    u = cw[0:1] * xm1 + cw[1:2] * x + cw[2:3] * xp1 + cw[3:4] * xp2 + cb_ref[...]
    ub = u.astype(BF16)
    r = jax.nn.sigmoid(_dot(ub, wa_ref[0]) + ba_ref[0])
    gi = jax.nn.sigmoid(_dot(ub, wx_ref[0]) + bx_ref[0])
    log_a = -LRU_C * r * _softplus(-lam_ref[0])
    a = jnp.exp(log_a)
    th = jnp.tanh(log_a)
    bv = jnp.sqrt(-2.0 * th / (1.0 - th)) * (gi * u)
    k = 1
    while k < tm:
        if k < 8:
            keep = (row < tm - k) if reverse else (row >= k)
            shift = (tm - k) if reverse else k
            a_sh = jnp.where(keep, pltpu.roll(a, shift, axis=0), 1.0)
            b_sh = jnp.where(keep, pltpu.roll(bv, shift, axis=0), 0.0)
            bv = a * b_sh + bv
            a = a * a_sh
        elif reverse:
            head_b = a[:tm - k] * bv[k:] + bv[:tm - k]
            head_a = a[:tm - k] * a[k:]
            bv = jnp.concatenate([head_b, bv[tm - k:]], axis=0)
            a = jnp.concatenate([head_a, a[tm - k:]], axis=0)
        else:
            tail_b = a[k:] * bv[:tm - k] + bv[k:]
            tail_a = a[k:] * a[:tm - k]
            bv = jnp.concatenate([bv[:k], tail_b], axis=0)
            a = jnp.concatenate([a[:k], tail_a], axis=0)
        k *= 2
    hcur = a * carry[0:1, :] + bv
    h_ref[0] = hcur
    last = hcur[0:1, :] if reverse else hcur[tm - 1:tm, :]
    carry[...] = jnp.broadcast_to(last, carry.shape)


def _lru_scan(lx, conv_w, conv_b, wa_bd, ba, wx_bd, bx, lam, h0, reverse):
    b, l, w = lx.shape
    tm = _row_tile(l, ROWS_PROJ)
    nt = l // tm
    tb = tm // 8
    tidx = (lambda j: nt - 1 - j) if reverse else (lambda j: j)
    vec = pl.BlockSpec((1, w), lambda i, j: (0, 0))
    mat = pl.BlockSpec((1, w, w), lambda i, j: (0, 0, 0))
    return pl.pallas_call(
        functools.partial(_lru_kernel, reverse=reverse, tm=tm, nt=nt),
        out_shape=jax.ShapeDtypeStruct((b, l, w), F32),
        grid=(b, nt),
        in_specs=[pl.BlockSpec((1, tm, w), lambda i, j: (i, tidx(j), 0)),
                  pl.BlockSpec((1, 8, w), lambda i, j: (i, jnp.maximum(tidx(j) * tb - 1, 0), 0)),
                  pl.BlockSpec((1, 8, w), lambda i, j: (i, jnp.minimum((tidx(j) + 1) * tb, l // 8 - 1), 0)),
                  pl.BlockSpec((4, w), lambda i, j: (0, 0)), vec,
                  mat, vec, mat, vec, vec,
                  pl.BlockSpec((1, 1, w), lambda i, j: (i, 0, 0))],
        out_specs=pl.BlockSpec((1, tm, w), lambda i, j: (i, tidx(j), 0)),
        scratch_shapes=[pltpu.VMEM((8, w), F32)],
        compiler_params=_params("arbitrary", "arbitrary"),
        name="lru_scan_rev" if reverse else "lru_scan_fwd",
    )(lx, lx, lx, conv_w, conv_b.reshape(1, w), wa_bd.reshape(1, w, w), ba.reshape(1, w),
      wx_bd.reshape(1, w, w), bx.reshape(1, w), lam.reshape(1, w), h0)


def _outproj_kernel(x_ref, gt_ref, oa_ref, ry_ref, rg_ref, rgn_ref, lam_ref, dgn_ref, draw_ref, hf_ref, hb_ref,
                    ly_ref, bd_ref, w_ref, o_ref, *, lam_init):
    oa = oa_ref[0]
    y = ry_ref[0, 0] + ry_ref[0, 1]
    y2 = y * y
    hi = y2.astype(BF16)
    lo = (y2 - hi.astype(F32)).astype(BF16)
    ms = _dot(hi, bd_ref[...]) + _dot(lo, bd_ref[...])
    ob = y * lax.rsqrt(ms + NORM_EPS) * rgn_ref[...] * _silu(rg_ref[0])
    lv = lam_ref[...]
    lam = (jnp.exp(jnp.sum(lv[0:1] * lv[1:2], axis=1, keepdims=True))
           - jnp.exp(jnp.sum(lv[2:3] * lv[3:4], axis=1, keepdims=True)) + lam_init)
    heads = []
    for hd in range(N_HEADS):
        a0 = draw_ref[0, hd, 0]
        a1 = draw_ref[0, hd, 1]
        oh = (a0[0:HEAD_DIM] / a0[HEAD_DIM:HEAD_DIM + 1]
              - lam * (a1[0:HEAD_DIM] / a1[HEAD_DIM:HEAD_DIM + 1]))
        msd = jnp.mean(oh * oh, axis=0, keepdims=True)
        heads.append(oh * lax.rsqrt(msd + NORM_EPS) * (dgn_ref[hd] * (1.0 - lam_init)))
    od = jnp.concatenate(heads, axis=0).T
    ol = (hf_ref[0] + hb_ref[0]) * jax.nn.gelu(ly_ref[0])
    acc = _dot(oa, w_ref[0:256, :])
    acc = acc + _dot(ob.astype(BF16), w_ref[256:512, :])
    acc = acc + _dot(od.astype(BF16), w_ref[512:768, :])
    acc = acc + _dot(ol.astype(BF16), w_ref[768:1024, :])
    o_ref[0] = x_ref[0] + gt_ref[0] * acc


def _out_projection(x, gt, oa, ry, rg, ret_gn, lam_vecs, diff_gn, lam_init, draw, hf, hb, ly, w_out):
    b, l, d = x.shape
    tm = _row_tile(l, ROWS_PROJ)
    row = lambda wdt: pl.BlockSpec((1, tm, wdt), lambda i, j: (i, j, 0))
    hd = jnp.arange(GROUP) // HEAD_DIM
    bd = jnp.where(hd[:, None] == hd[None, :], 1.0 / HEAD_DIM, 0.0).astype(BF16)
    return pl.pallas_call(
        functools.partial(_outproj_kernel, lam_init=lam_init),
        out_shape=jax.ShapeDtypeStruct((b, l, d), F32),
        grid=(b, l // tm),
        in_specs=[row(d), pl.BlockSpec((1, 1, d), lambda i, j: (i, 0, 0)), row(256),
                  pl.BlockSpec((1, 2, tm, 256), lambda i, j: (i, 0, j, 0)), row(256),
                  pl.BlockSpec((1, 256), lambda i, j: (0, 0)),
                  pl.BlockSpec((4, DIFF_DIM), lambda i, j: (0, 0)),
                  pl.BlockSpec((N_HEADS, HEAD_DIM, 1), lambda i, j: (0, 0, 0)),
                  pl.BlockSpec((1, N_HEADS, 2, DIFF_VROWS, tm), lambda i, j: (i, 0, 0, 0, j)),
                  row(256), row(256), row(256),
                  pl.BlockSpec((256, 256), lambda i, j: (0, 0)),
                  pl.BlockSpec((d, d), lambda i, j: (0, 0))],
        out_specs=row(d),
        compiler_params=_params("arbitrary", "arbitrary"),
        name="out_projection",
    )(x, gt, oa, ry, rg, ret_gn.reshape(1, 256), lam_vecs, diff_gn.reshape(N_HEADS, HEAD_DIM, 1), draw,
      hf, hb, ly, bd, w_out)


def _norm_modulate(x_ref, g_ref, sh_ref, sc_ref):
    x = x_ref[0]
    ms = jnp.mean(x * x, axis=-1, keepdims=True)
    y = x * lax.rsqrt(ms + NORM_EPS) * g_ref[...]
    return y * (1.0 + sc_ref[0]) + sh_ref[0]


def _ffn_kernel(x_ref, g_ref, sh_ref, sc_ref, gt_ref, w1_ref, w3_ref, w2_ref, o_ref, hb_scr, acc_scr,
                *, n_f):
    f = pl.program_id(2)

    @pl.when(f == 0)
    def _():
        hb_scr[...] = _norm_modulate(x_ref, g_ref, sh_ref, sc_ref).astype(BF16)
        acc_scr[...] = jnp.zeros_like(acc_scr)

    hb = hb_scr[...]
    a = _silu(_dot(hb, w1_ref[...])) * _dot(hb, w3_ref[...])
    acc_scr[...] += _dot(a.astype(BF16), w2_ref[...])

    @pl.when(f == n_f - 1)
    def _():
        o_ref[0] = x_ref[0] + gt_ref[0] * acc_scr[...]


def _ffn(x, g, sh, sc, gt, w1, w3, w2):
    b, l, d = x.shape
    ff = w1.shape[1]
    tm = _row_tile(l, ROWS_FFN)
    n_f = 2
    fc = ff // n_f
    vec = pl.BlockSpec((1, 1, d), lambda i, j, f: (i, 0, 0))
    return pl.pallas_call(
        functools.partial(_ffn_kernel, n_f=n_f),
        out_shape=jax.ShapeDtypeStruct((b, l, d), F32),
        grid=(b, l // tm, n_f),
        in_specs=[pl.BlockSpec((1, tm, d), lambda i, j, f: (i, j, 0)),
                  pl.BlockSpec((1, d), lambda i, j, f: (0, 0)), vec, vec, vec,
                  pl.BlockSpec((d, fc), lambda i, j, f: (0, f)),
                  pl.BlockSpec((d, fc), lambda i, j, f: (0, f)),
                  pl.BlockSpec((fc, d), lambda i, j, f: (f, 0))],
        out_specs=pl.BlockSpec((1, tm, d), lambda i, j, f: (i, j, 0)),
        scratch_shapes=[pltpu.VMEM((tm, d), BF16), pltpu.VMEM((tm, d), F32)],
        compiler_params=_params("arbitrary", "arbitrary", "arbitrary"),
        name="dense_ffn",
    )(x, g.reshape(1, d), sh, sc, gt, w1, w3, w2)


MOE_RT = 128


def _moe_kernel(x_ref, g_ref, sh_ref, sc_ref, gt_ref, fg_ref, rt_ref, w1_ref, w3_ref, w2_ref, o_ref,
                hb_scr, gate_scr, pos_scr, gatet_scr, post_scr, x_scr, y_scr, gs_scr, cnt_smem,
                *, n_e, n_f, tm, final_norm):
    e = pl.program_id(2)
    f = pl.program_id(3)
    rt2 = 2 * MOE_RT

    @pl.when((e == 0) & (f == 0))
    def _():
        h = _norm_modulate(x_ref, g_ref, sh_ref, sc_ref)
        hb = h.astype(BF16)
        hb_scr[...] = hb
        o_ref[0] = jnp.zeros(o_ref.shape[1:], F32)
        rt = rt_ref[...]
        rhi = rt.astype(BF16)
        rlo = (rt - rhi.astype(F32)).astype(BF16)
        hlo = (h - hb.astype(F32)).astype(BF16)
        logits = _dot(hb, rhi) + _dot(hb, rlo) + _dot(hlo, rhi)
        lane = lax.broadcasted_iota(jnp.int32, logits.shape, 1)
        logits = jnp.where(lane < N_EXPERTS, logits, -jnp.inf)
        m1 = jnp.max(logits, axis=1, keepdims=True)
        i1 = jnp.min(jnp.where(logits == m1, lane, LANE), axis=1, keepdims=True)
        rest = jnp.where(lane == i1, -jnp.inf, logits)
        m2 = jnp.max(rest, axis=1, keepdims=True)
        i2 = jnp.min(jnp.where(rest == m2, lane, LANE), axis=1, keepdims=True)
        e2 = jnp.exp(m2 - m1)
        den = 1.0 + e2
        gate = jnp.where(lane == i1, 1.0 / den, 0.0) + jnp.where(lane == i2, e2 / den, 0.0)
        gate_scr[...] = gate
        sel = jnp.where(gate > 0.0, 1.0, 0.0)
        selb = sel.astype(BF16)
        strip = 256
        for r0 in range(0, tm, strip):
            tok_r = r0 + lax.broadcasted_iota(jnp.int32, (strip, tm), 0)
            tok_c = lax.broadcasted_iota(jnp.int32, (strip, tm), 1)
            earlier = jnp.where(tok_c < tok_r, 1.0, 0.0).astype(BF16)
            pos_scr[r0:r0 + strip, :] = _dot(earlier, selb)
        post_scr[...] = pos_scr[...].T
        gatet_scr[...] = gate.T
        counts = jnp.sum(sel, axis=0, keepdims=True)
        for ee in range(n_e):
            cnt_smem[ee] = jnp.sum(jnp.where(lane[0:1] == ee, counts, 0.0)).astype(jnp.int32)

    n_tiles = (cnt_smem[e] + (MOE_RT - 1)) // MOE_RT
    n_pairs = (n_tiles + 1) // 2

    @pl.when(f == 0)
    def _():
        sub = lax.broadcasted_iota(jnp.int32, (8, 1), 0)
        pos_e = jnp.sum(jnp.where(sub == e, post_scr[0:8, :], 0.0), axis=0, keepdims=True)
        gate_e = jnp.sum(jnp.where(sub == e, gatet_scr[0:8, :], 0.0), axis=0, keepdims=True)

        def gather(jj, carry):
            r0 = pl.multiple_of(jj * rt2, rt2)
            slot = (r0 + lax.broadcasted_iota(jnp.int32, (rt2, 1), 0)).astype(F32)
            hit = (pos_e == slot) & (gate_e > 0.0)
            onehot = jnp.where(hit, 1.0, 0.0).astype(BF16)
            x_scr[pl.ds(r0, rt2), :] = _dot(onehot, hb_scr[...]).astype(BF16)
            gs = jnp.sum(jnp.where(hit, gate_e, 0.0), axis=1, keepdims=True)
            gs_scr[pl.ds(r0, rt2), :] = jnp.broadcast_to(gs, (rt2, LANE))
            y_scr[pl.ds(r0, rt2), :] = jnp.zeros((rt2, y_scr.shape[1]), F32)
            return carry

        lax.fori_loop(0, n_pairs, gather, 0)

    def expert(r0, rows):
        xj = x_scr[pl.ds(r0, rows), :]
        a = _silu(_dot(xj, w1_ref[0])) * _dot(xj, w3_ref[0])
        y_scr[pl.ds(r0, rows), :] += _dot(a.astype(BF16), w2_ref[0])

    def expert_pair(jj, carry):
        expert(pl.multiple_of(jj * rt2, rt2), rt2)
        return carry

    lax.fori_loop(0, n_tiles // 2, expert_pair, 0)

    @pl.when(n_tiles % 2 == 1)
    def _():
        expert(pl.multiple_of((n_tiles - 1) * MOE_RT, MOE_RT), MOE_RT)

    @pl.when(f == n_f - 1)
    def _():
        lane = lax.broadcasted_iota(jnp.int32, (1, LANE), 1)
        pos_col = jnp.sum(jnp.where(lane == e, pos_scr[...], 0.0), axis=1, keepdims=True)
        gate_col = jnp.sum(jnp.where(lane == e, gate_scr[...], 0.0), axis=1, keepdims=True)

        def scatter(jj, carry):
            r0 = pl.multiple_of(jj * rt2, rt2)
            slot = (r0 + lax.broadcasted_iota(jnp.int32, (1, rt2), 1)).astype(F32)
            onehot = jnp.where((pos_col == slot) & (gate_col > 0.0), 1.0, 0.0).astype(BF16)
            ys = (y_scr[pl.ds(r0, rt2), :] * gs_scr[pl.ds(r0, rt2), 0:1]).astype(BF16)
            o_ref[0] += _dot(onehot, ys)
            return carry

        lax.fori_loop(0, n_pairs, scatter, 0)

    @pl.when((e == n_e - 1) & (f == n_f - 1))
    def _():
        out = x_ref[0] + gt_ref[0] * o_ref[0]
        if final_norm:
            ms = jnp.mean(out * out, axis=-1, keepdims=True)
            out = out * lax.rsqrt(ms + NORM_EPS) * fg_ref[...]
        o_ref[0] = out


def _moe_ffn(x, g, sh, sc, gt, final_g, w1, w3, w2, router, final_norm):
    b, l, d = x.shape
    n_e, _, ff = w1.shape
    tm = _row_tile(l, ROWS_FFN)
    n_f = 2
    fc = ff // n_f
    vec = pl.BlockSpec((1, 1, d), lambda i, j, e, f: (i, 0, 0))
    return pl.pallas_call(
        functools.partial(_moe_kernel, n_e=n_e, n_f=n_f, tm=tm, final_norm=final_norm),
        out_shape=jax.ShapeDtypeStruct((b, l, d), F32),
        grid=(b, l // tm, n_e, n_f),
        in_specs=[pl.BlockSpec((1, tm, d), lambda i, j, e, f: (i, j, 0)),
                  pl.BlockSpec((1, d), lambda i, j, e, f: (0, 0)), vec, vec, vec,
                  pl.BlockSpec((1, d), lambda i, j, e, f: (0, 0)),
                  pl.BlockSpec((d, LANE), lambda i, j, e, f: (0, 0)),
                  pl.BlockSpec((1, d, fc), lambda i, j, e, f: (e, 0, f)),
                  pl.BlockSpec((1, d, fc), lambda i, j, e, f: (e, 0, f)),
                  pl.BlockSpec((1, fc, d), lambda i, j, e, f: (e, f, 0))],
        out_specs=pl.BlockSpec((1, tm, d), lambda i, j, e, f: (i, j, 0)),
        scratch_shapes=[pltpu.VMEM((tm, d), BF16),
                        pltpu.VMEM((tm, LANE), F32),
                        pltpu.VMEM((tm, LANE), F32),
                        pltpu.VMEM((LANE, tm), F32),
                        pltpu.VMEM((LANE, tm), F32),
                        pltpu.VMEM((tm, d), BF16),
                        pltpu.VMEM((tm, d), F32),
                        pltpu.VMEM((tm, LANE), F32),
                        pltpu.SMEM((n_e,), jnp.int32)],
        compiler_params=_params("arbitrary", "arbitrary", "arbitrary", "arbitrary"),
        name="moe_ffn",
    )(x, g.reshape(1, d), sh, sc, gt, final_g.reshape(1, d), router, w1, w3, w2)


def _rope_tables(n):
    rows = n // GRID_W
    row = jnp.repeat(jnp.arange(rows, dtype=F32), GRID_W)
    col = jnp.tile(jnp.arange(GRID_W, dtype=F32), rows)

    def axial(dim):
        nf = dim // 4
        inv = ROPE_BASE ** (-jnp.arange(nf, dtype=F32) / nf)
        return jnp.concatenate([row[:, None] * inv, col[:, None] * inv], axis=-1)

    def line(dim):
        nf = dim // 2
        inv = ROPE_BASE ** (-jnp.arange(nf, dtype=F32) / nf)
        return jnp.arange(n, dtype=F32)[:, None] * inv

    def expand(ang, dim):
        c = jnp.cos(ang)
        s = jnp.sin(ang)
        c = jnp.tile(jnp.concatenate([c, c], axis=-1), (1, LANE // dim))
        s = jnp.tile(jnp.concatenate([-s, s], axis=-1), (1, LANE // dim))
        return c, s

    ca, sa = expand(axial(HEAD_DIM), HEAD_DIM)
    cr, sr = expand(line(HEAD_DIM), HEAD_DIM)
    cd, sd = expand(axial(DIFF_DIM), DIFF_DIM)
    return ca, sa, cr, sr, cd, sd


_ATTN_HEAD_ORDER = (0, 2, 1, 3)


def _attn_perm():
    return jnp.concatenate([jnp.arange(HEAD_DIM) + HEAD_DIM * h for h in _ATTN_HEAD_ORDER])


def _block_diag(w):
    nb, blk, _ = w.shape
    out = jnp.zeros((nb * blk, nb * blk), w.dtype)
    for i in range(nb):
        out = out.at[i * blk:(i + 1) * blk, i * blk:(i + 1) * blk].set(w[i])
    return out


def kernel(x, c, ctx, c_ctx, w_ada, b_ada, g_norm1, g_norm2, w_in, w_out, attn_sink, ret_decay_logit,
           ret_gn, diff_lambda, diff_gn, conv_w, conv_b, lru_wa, lru_ba, lru_wx, lru_bx, lru_lambda,
           ffn_w1, ffn_w3, ffn_w2, moe_router, moe_w1, moe_w3, moe_w2, final_norm):
    b, n, d = x.shape
    m = ctx.shape[1]
    tables = _rope_tables(n)
    perm = _attn_perm()

    cc = jnp.concatenate([c, c_ctx[None, :], jnp.zeros((8 - b - 1, d), F32)], axis=0)
    mods = _mod_vectors(cc, w_ada, b_ada)

    xc = ctx
    for layer in range(DEPTH):
        need_ctx = layer < DEPTH - 1
        lam_init = 0.8 - 0.6 * math.exp(-0.3 * layer)
        mod = mods[layer]
        sh1, sc1, gt1, sh2, sc2, gt2 = [mod[:b, None, i * d:(i + 1) * d] for i in range(6)]
        sh1c, sc1c, gt1c, sh2c, sc2c, gt2c = [
            jnp.broadcast_to(mod[b:b + 1, None, i * d:(i + 1) * d], (b, 1, d)) for i in range(6)]

        wi = w_in[layer]
        wi = jnp.concatenate([wi[:, :256][:, perm], wi[:, 256:]], axis=1).astype(BF16)
        wo = w_out[layer]
        wo = jnp.concatenate([wo[:256][perm], wo[256:]], axis=0).astype(BF16)
        sink_perm = attn_sink[layer][jnp.array(_ATTN_HEAD_ORDER)]

        (qa, ka, va, rq, rk, rv, rg, dqt, dk, dvt, lx, ly) = _in_projection(
            x, g_norm1[layer], sh1, sc1, wi, tables)
        (qac, kac, vac, rqc, rkc, rvc, rgc, dqtc, dkc, dvtc, lxc, lyc) = _in_projection(
            xc, g_norm1[layer], sh1c, sc1c, wi, None)

        oa = _swa_attention(sink_perm, qa, ka, va, kac, vac)
        dl = ret_decay_logit[layer]
        dl_lane = jnp.repeat(dl, HEAD_DIM, axis=1).reshape(2, 1, GROUP)
        dl_head = jnp.broadcast_to(dl[:, :, None, None], (2, N_HEADS, 8, RET_CHUNK))
        ryc, s_ctx = _retention(dl_lane, dl_head, rqc, rkc, rvc, jnp.zeros((b, 2, GROUP, GROUP), F32))
        ry, _ = _retention(dl_lane, dl_head, rq, rk, rv, s_ctx)
        k_all = jnp.concatenate([dk, dkc], axis=1)
        vt_all = jnp.concatenate([dvt, dvtc], axis=1)
        draw = _diff_attention(dqt, k_all, vt_all)
        assemble = functools.partial(_out_projection, ret_gn=ret_gn[layer], lam_vecs=diff_lambda[layer],
                                     diff_gn=diff_gn[layer], lam_init=lam_init, w_out=wo)
        lru = []
        lru_c = []
        for direction in range(2):
            prm = (conv_w[layer], conv_b[layer],
                   _block_diag(lru_wa[layer, direction]).astype(BF16), lru_ba[layer, direction],
                   _block_diag(lru_wx[layer, direction]).astype(BF16), lru_bx[layer, direction],
                   lru_lambda[layer, direction])
            rev = direction == 1
            hc = _lru_scan(lxc, *prm, jnp.zeros((b, 1, GROUP), F32), rev)
            h0 = hc[:, 0:1, :] if rev else hc[:, m - 1:m, :]
            lru.append(_lru_scan(lx, *prm, h0, rev))
            lru_c.append(hc)

        x = assemble(x, gt1, oa, ry, rg, draw=draw, hf=lru[0], hb=lru[1], ly=ly)
        j = layer // 2
        if layer % 2 == 0:
            ffn = functools.partial(_ffn, w1=ffn_w1[j].astype(BF16), w3=ffn_w3[j].astype(BF16),
                                    w2=ffn_w2[j].astype(BF16))
            x = ffn(x, g_norm2[layer], sh2, sc2, gt2)
        else:
            moe = functools.partial(_moe_ffn, final_g=final_norm, w1=moe_w1[j].astype(BF16),
                                    w3=moe_w3[j].astype(BF16), w2=moe_w2[j].astype(BF16),
                                    router=jnp.pad(moe_router[j], ((0, 0), (0, LANE - N_EXPERTS))))
            x = moe(x, g_norm2[layer], sh2, sc2, gt2, final_norm=layer == DEPTH - 1)
            ffn = functools.partial(moe, final_norm=False)

        if need_ctx:
            oac = _ctx_attention(sink_perm, qac, kac, vac)
            drawc = _diff_attention(dqtc, dkc, dvtc)
            xc = assemble(xc, gt1c, oac, ryc, rgc, draw=drawc, hf=lru_c[0], hb=lru_c[1], ly=lyc)
            xc = ffn(xc, g_norm2[layer], sh2c, sc2c, gt2c)
    assert DEPTH % 2 == 0, "the closing RMSNorm is fused into the last (routed) layer"
    return x
```
